```python
import math
import jax, jax.numpy as jnp
from jax import lax
import numpy as np

D_MODEL = 1024
BATCH = 2
SEQ = 8192
DEPTH = 2
DEC_BATCH = 32
DEC_SEQ = 8
PAST_LEN = 16384
PAGE_SIZE = 128

N_A_LAYERS = DEPTH // 2
N_B_LAYERS = DEPTH - N_A_LAYERS
D_FF = 2816
SSM_GROUP = 16
N_SSM_GROUPS = D_MODEL // SSM_GROUP
SSM_STATE = 64
HEAD_DIM = 64
WINDOWS = (128, 512, 2048)
DILATIONS = (1, 4, 16)
N_BRANCHES = 3
HEADS_PER_BRANCH = 8
N_HEADS = N_BRANCHES * HEADS_PER_BRANCH
ATTN_WIDTH = N_HEADS * HEAD_DIM
MERGED_WIDTH = HEADS_PER_BRANCH * HEAD_DIM
MAX_WINDOW = max(WINDOWS)
ROT_DIM = HEAD_DIM // 4
ROPE_THETA = 500000.0
Q_BLOCK = 128
RMS_EPS = 1e-6
NEG_INF = -1e30

kernel_name = "yoco_s5_dilated_swa_decoder_step"


def _rms(x, g):
    x32 = x.astype(jnp.float32)
    y = x32 * lax.rsqrt(jnp.mean(x32 * x32, axis=-1, keepdims=True) + RMS_EPS) * g.astype(jnp.float32)
    return y.astype(x.dtype)


def _swiglu(h, w_in, w_out):
    g, u = jnp.split(h @ w_in, 2, axis=-1)
    return (jax.nn.silu(g) * u) @ w_out


def _rope_partial(x, pos):
    half = ROT_DIM // 2
    inv = ROPE_THETA ** (-jnp.arange(0, ROT_DIM, 2, dtype=jnp.float32) / ROT_DIM)
    ang = pos.astype(jnp.float32)[:, None] * inv[None, :]
    cos = jnp.cos(ang)[None, :, None, :]
    sin = jnp.sin(ang)[None, :, None, :]
    x32 = x.astype(jnp.float32)
    x1 = x32[..., :half]
    x2 = x32[..., half:ROT_DIM]
    out = jnp.concatenate([x1 * cos - x2 * sin, x2 * cos + x1 * sin, x32[..., ROT_DIM:]], axis=-1)
    return out.astype(x.dtype)


def _lin_combine(e1, e2):
    a1, b1 = e1
    a2, b2 = e2
    return a1 * a2, a2 * b1 + b2


def _s5_mixer(u, h0, a_re, a_im, log_dt, b_re, b_im, c_re, c_im, d_skip, w_glu):
    bsz, seq_len, _ = u.shape
    f32 = jnp.float32
    lam = lax.complex(a_re.astype(f32), a_im.astype(f32))
    dt = jnp.exp(log_dt.astype(f32))[:, None]
    lam_bar = jnp.exp(lam * dt)
    b_bar = ((lam_bar - 1.0) / lam)[:, :, None] * lax.complex(b_re.astype(f32), b_im.astype(f32))
    c = lax.complex(c_re.astype(f32), c_im.astype(f32))
    ug = u.astype(f32).reshape(bsz, seq_len, N_SSM_GROUPS, SSM_GROUP)
    bu = jnp.einsum('blgh,gph->blgp', ug.astype(jnp.complex64), b_bar)
    h0c = lax.complex(h0[..., 0].astype(f32), h0[..., 1].astype(f32))
    bu = bu.at[:, 0].add(lam_bar * h0c)
    a = jnp.broadcast_to(lam_bar, bu.shape)
    _, h = lax.associative_scan(_lin_combine, (a, bu), axis=1)
    y = jnp.real(jnp.einsum('blgp,ghp->blgh', h, c)) + d_skip.astype(f32).reshape(N_SSM_GROUPS, SSM_GROUP) * ug
    z = jax.nn.gelu(y.reshape(bsz, seq_len, D_MODEL)).astype(u.dtype)
    za, zb = jnp.split(z @ w_glu, 2, axis=-1)
    out = za * jax.nn.sigmoid(zb)
    h_last = h[:, -1]
    new_state = jnp.stack([jnp.real(h_last), jnp.imag(h_last)], axis=-1).astype(h0.dtype)
    return out, new_state


def _branch_attend(q_b, loc_b, k_br, v_br, dilation, n_taps):
    idx = loc_b[:, None] - dilation * jnp.arange(n_taps, dtype=jnp.int32)[None, :]
    valid = idx >= 0
    idx = jnp.maximum(idx, 0)
    kg = jnp.take(k_br, idx, axis=1)
    vg = jnp.take(v_br, idx, axis=1)
    s = jnp.einsum('bqhd,bqthd->bhqt', q_b.astype(jnp.float32), kg.astype(jnp.float32)) * (HEAD_DIM ** -0.5)
    s = jnp.where(valid[None, None], s, NEG_INF)
    lse = jax.nn.logsumexp(s, axis=-1)
    p = jnp.exp(s - lse[..., None])
    o = jnp.einsum('bhqt,bqthd->bqhd', p, vg.astype(jnp.float32))
    return o, lse


def _dilated_block(q_b, loc_b, k_brs, v_brs):
    outs, lses = [], []
    for i in range(N_BRANCHES):
        hs = slice(i * HEADS_PER_BRANCH, (i + 1) * HEADS_PER_BRANCH)
        o, lse = _branch_attend(q_b[:, :, hs], loc_b, k_brs[i], v_brs[i], DILATIONS[i], WINDOWS[i] // DILATIONS[i] + 1)
        outs.append(o)
        lses.append(lse)
    w = jax.nn.softmax(jnp.stack(lses, axis=0), axis=0)
    w = jnp.transpose(w, (0, 1, 3, 2))[..., None]
    return jnp.sum(w * jnp.stack(outs, axis=0), axis=0)


def _dilated_attention(q, k_src, v_src, q_loc):
    bsz, nq = q.shape[0], q.shape[1]
    k_brs = [k_src[:, :, i * HEADS_PER_BRANCH:(i + 1) * HEADS_PER_BRANCH] for i in range(N_BRANCHES)]
    v_brs = [v_src[:, :, i * HEADS_PER_BRANCH:(i + 1) * HEADS_PER_BRANCH] for i in range(N_BRANCHES)]
    if nq > Q_BLOCK:
        nb = nq // Q_BLOCK
        qb = q.reshape(bsz, nb, Q_BLOCK, N_HEADS, HEAD_DIM).transpose(1, 0, 2, 3, 4)
        lb = q_loc.reshape(nb, Q_BLOCK)
        ob = lax.map(lambda a: _dilated_block(a[0], a[1], k_brs, v_brs), (qb, lb))
        o = ob.transpose(1, 0, 2, 3, 4).reshape(bsz, nq, HEADS_PER_BRANCH, HEAD_DIM)
    else:
        o = _dilated_block(q, q_loc, k_brs, v_brs)
    return o.reshape(bsz, nq, MERGED_WIDTH)


def _trunk(x, pos, ssm_h0, past_k, past_v,
           ffn_norm1, ffn_w1_in, ffn_w1_out, ffn_norm2, ffn_w2_in, ffn_w2_out, mix_norm,
           ssm_a_re, ssm_a_im, ssm_log_dt, ssm_b_re, ssm_b_im, ssm_c_re, ssm_c_im, ssm_d, ssm_w_glu,
           kv_norm, w_kv, w_q, w_o, final_norm):
    bsz, nq = x.shape[0], x.shape[1]
    new_ssm = []
    k_src = v_src = q_loc = None
    for layer in range(DEPTH):
        x = x + 0.5 * _swiglu(_rms(x, ffn_norm1[layer]), ffn_w1_in[layer], ffn_w1_out[layer])
        h = _rms(x, mix_norm[layer])
        if layer < N_A_LAYERS:
            out, st = _s5_mixer(h, ssm_h0[layer], ssm_a_re[layer], ssm_a_im[layer], ssm_log_dt[layer],
                                ssm_b_re[layer], ssm_b_im[layer], ssm_c_re[layer], ssm_c_im[layer],
                                ssm_d[layer], ssm_w_glu[layer])
            new_ssm.append(st)
            x = x + out
        else:
            j = layer - N_A_LAYERS
            q = _rope_partial((h @ w_q[j]).reshape(bsz, nq, N_HEADS, HEAD_DIM), pos)
            o = _dilated_attention(q, k_src, v_src, q_loc)
            x = x + o.astype(x.dtype) @ w_o[j]
        x = x + 0.5 * _swiglu(_rms(x, ffn_norm2[layer]), ffn_w2_in[layer], ffn_w2_out[layer])
        if layer == N_A_LAYERS - 1:
            hk = _rms(x, kv_norm)
            k_new, v_new = jnp.split(hk @ w_kv, 2, axis=-1)
            k_new = _rope_partial(k_new.reshape(bsz, nq, N_HEADS, HEAD_DIM), pos)
            v_new = v_new.reshape(bsz, nq, N_HEADS, HEAD_DIM)
            k_src = jnp.concatenate([past_k.astype(k_new.dtype), k_new], axis=1)
            v_src = jnp.concatenate([past_v.astype(v_new.dtype), v_new], axis=1)
            q_loc = past_k.shape[1] + jnp.arange(nq, dtype=jnp.int32)
    keep = min(MAX_WINDOW, k_src.shape[1])
    return _rms(x, final_norm), jnp.stack(new_ssm, axis=0), k_src[:, -keep:], v_src[:, -keep:]


def setup_inputs(seed: int = 0) -> dict:
    key = jax.random.key(seed)
    ks = jax.random.split(key, 32)
    f32 = jnp.float32
    nrm = lambda k, shape, scale: jax.random.normal(k, shape, f32) * scale
    buf = min(MAX_WINDOW, PAST_LEN)
    G, P, H = N_SSM_GROUPS, SSM_STATE, SSM_GROUP
    a_im0 = math.pi * jnp.arange(P, dtype=f32)
    return {
        "x_prompt": nrm(ks[0], (BATCH, SEQ, D_MODEL), 1.0),
        "x_sample": nrm(ks[1], (DEC_BATCH, DEC_SEQ, D_MODEL), 1.0),
        "state_ssm": nrm(ks[2], (N_A_LAYERS, DEC_BATCH, G, P, 2), 0.3),
        "cache_k": nrm(ks[3], (DEC_BATCH, buf, N_HEADS, HEAD_DIM), 1.0),
        "cache_v": nrm(ks[4], (DEC_BATCH, buf, N_HEADS, HEAD_DIM), 1.0),
        "ffn_norm1": 1.0 + nrm(ks[5], (DEPTH, D_MODEL), 0.02),
        "ffn_w1_in": nrm(ks[6], (DEPTH, D_MODEL, 2 * D_FF), D_MODEL ** -0.5),
        "ffn_w1_out": nrm(ks[7], (DEPTH, D_FF, D_MODEL), D_FF ** -0.5),
        "ffn_norm2": 1.0 + nrm(ks[8], (DEPTH, D_MODEL), 0.02),
        "ffn_w2_in": nrm(ks[9], (DEPTH, D_MODEL, 2 * D_FF), D_MODEL ** -0.5),
        "ffn_w2_out": nrm(ks[10], (DEPTH, D_FF, D_MODEL), D_FF ** -0.5),
        "mix_norm": 1.0 + nrm(ks[11], (DEPTH, D_MODEL), 0.02),
        "ssm_a_re": -0.5 + nrm(ks[12], (N_A_LAYERS, G, P), 0.01),
        "ssm_a_im": a_im0 + nrm(ks[13], (N_A_LAYERS, G, P), 0.01),
        "ssm_log_dt": jax.random.uniform(ks[14], (N_A_LAYERS, G), f32, math.log(1e-3), math.log(1e-1)),
        "ssm_b_re": nrm(ks[15], (N_A_LAYERS, G, P, H), (2 * H) ** -0.5),
        "ssm_b_im": nrm(ks[16], (N_A_LAYERS, G, P, H), (2 * H) ** -0.5),
        "ssm_c_re": nrm(ks[17], (N_A_LAYERS, G, H, P), P ** -0.5),
        "ssm_c_im": nrm(ks[18], (N_A_LAYERS, G, H, P), P ** -0.5),
        "ssm_d": nrm(ks[19], (N_A_LAYERS, D_MODEL), 1.0),
        "ssm_w_glu": nrm(ks[20], (N_A_LAYERS, D_MODEL, 2 * D_MODEL), D_MODEL ** -0.5),
        "kv_norm": 1.0 + nrm(ks[21], (D_MODEL,), 0.02),
        "w_kv": nrm(ks[22], (D_MODEL, 2 * ATTN_WIDTH), D_MODEL ** -0.5),
        "w_q": nrm(ks[23], (N_B_LAYERS, D_MODEL, ATTN_WIDTH), D_MODEL ** -0.5),
        "w_o": nrm(ks[24], (N_B_LAYERS, MERGED_WIDTH, D_MODEL), MERGED_WIDTH ** -0.5),
        "final_norm": 1.0 + nrm(ks[25], (D_MODEL,), 0.02),
    }


def reference(x_prompt, x_sample, state_ssm, cache_k, cache_v,
              ffn_norm1, ffn_w1_in, ffn_w1_out, ffn_norm2, ffn_w2_in, ffn_w2_out, mix_norm,
              ssm_a_re, ssm_a_im, ssm_log_dt, ssm_b_re, ssm_b_im, ssm_c_re, ssm_c_im, ssm_d, ssm_w_glu,
              kv_norm, w_kv, w_q, w_o, final_norm):
    bp, sp = x_prompt.shape[0], x_prompt.shape[1]
    bs, ss = x_sample.shape[0], x_sample.shape[1]
    pos_p = jnp.arange(sp, dtype=jnp.int32)
    h0_p = jnp.zeros((N_A_LAYERS, bp, N_SSM_GROUPS, SSM_STATE, 2), x_prompt.dtype)
    past_p = jnp.zeros((bp, 0, N_HEADS, HEAD_DIM), x_prompt.dtype)
    y_prompt, state_ssm_prompt, cache_k_prompt, cache_v_prompt = _trunk(
        x_prompt, pos_p, h0_p, past_p, past_p,
        ffn_norm1, ffn_w1_in, ffn_w1_out, ffn_norm2, ffn_w2_in, ffn_w2_out, mix_norm,
        ssm_a_re, ssm_a_im, ssm_log_dt, ssm_b_re, ssm_b_im, ssm_c_re, ssm_c_im, ssm_d, ssm_w_glu,
        kv_norm, w_kv, w_q, w_o, final_norm)
    pos_s = PAST_LEN + jnp.arange(ss, dtype=jnp.int32)
    y_sample, state_ssm_sample, cache_k_sample, cache_v_sample = _trunk(
        x_sample, pos_s, state_ssm, cache_k, cache_v,
        ffn_norm1, ffn_w1_in, ffn_w1_out, ffn_norm2, ffn_w2_in, ffn_w2_out, mix_norm,
        ssm_a_re, ssm_a_im, ssm_log_dt, ssm_b_re, ssm_b_im, ssm_c_re, ssm_c_im, ssm_d, ssm_w_glu,
        kv_norm, w_kv, w_q, w_o, final_norm)
    return (y_prompt, y_sample, state_ssm_prompt, cache_k_prompt, cache_v_prompt,
            state_ssm_sample, cache_k_sample, cache_v_sample)
```

```python
import functools
import math

import jax
import jax.numpy as jnp
from jax import lax
from jax.experimental import pallas as pl
from jax.experimental.pallas import tpu as pltpu

F32 = jnp.float32
BF16 = jnp.bfloat16

D_MODEL = 1024
D_FF = 2816
SSM_GROUP = 16
N_SSM_GROUPS = D_MODEL // SSM_GROUP
SSM_STATE = 64
HEAD_DIM = 64
WINDOWS = (128, 512, 2048)
DILATIONS = (1, 4, 16)
N_BRANCHES = 3
HEADS_PER_BRANCH = 8
N_HEADS = N_BRANCHES * HEADS_PER_BRANCH
ATTN_WIDTH = N_HEADS * HEAD_DIM
MERGED_WIDTH = HEADS_PER_BRANCH * HEAD_DIM
MAX_WINDOW = max(WINDOWS)
ROT_DIM = HEAD_DIM // 4
ROPE_THETA = 500000.0
PAST_LEN = 16384
RMS_EPS = 1e-6
NEG_INF = -1e30
N_TAPS = WINDOWS[0] // DILATIONS[0] + 1

LANES = 128
SSM_BLOCKS = 4
SSM_GROUPS_PER_BLOCK = N_SSM_GROUPS // SSM_BLOCKS
SSM_BLOCK_IN = SSM_GROUPS_PER_BLOCK * SSM_GROUP
SSM_BLOCK_HALF = SSM_GROUPS_PER_BLOCK * SSM_STATE
SSM_BLOCK_STATE = 2 * SSM_BLOCK_HALF
SSM_STATE_LANES = SSM_BLOCKS * SSM_BLOCK_STATE

FF_CHUNK = 1408
SWA_BLOCK = 128

VMEM_LIMIT = 56 * 1024 * 1024


def _params(semantics):
    return pltpu.CompilerParams(dimension_semantics=semantics, vmem_limit_bytes=VMEM_LIMIT)


def _const_spec(shape):
    nd = len(shape)
    return pl.BlockSpec(shape, lambda *_: (0,) * nd, pipeline_mode=pl.Buffered(1))


def _rms(x, g):
    return x * lax.rsqrt(jnp.mean(x * x, axis=-1, keepdims=True) + RMS_EPS) * g


def _dot(a, b):
    return jnp.dot(a, b, preferred_element_type=F32)


def _ffn_body(x_ref, g_ref, win_ref, wout_ref, *rest, final):
    if final:
        fg_ref, o_ref = rest
    else:
        (o_ref,) = rest
    x = x_ref[...]
    h = _rms(x, g_ref[...]).astype(BF16)
    acc = x
    for j in range(D_FF // FF_CHUNK):
        lo = j * FF_CHUNK
        gate = _dot(h, win_ref[:, lo:lo + FF_CHUNK])
        up = _dot(h, win_ref[:, D_FF + lo:D_FF + lo + FF_CHUNK])
        act = gate * jax.nn.sigmoid(gate) * up
        acc = acc + _dot((0.5 * act).astype(BF16), wout_ref[lo:lo + FF_CHUNK, :])
    if final:
        acc = _rms(acc, fg_ref[...])
    o_ref[...] = acc


def _ffn(x, g, w_in, w_out, final_g=None, tm=512):
    n = x.shape[0]
    tm = min(tm, n)
    row = pl.BlockSpec((tm, D_MODEL), lambda i: (i, 0))
    in_specs = [row, _const_spec((1, D_MODEL)), _const_spec(w_in.shape), _const_spec(w_out.shape)]
    args = [x, g.reshape(1, D_MODEL), w_in, w_out]
    if final_g is not None:
        in_specs.append(_const_spec((1, D_MODEL)))
        args.append(final_g.reshape(1, D_MODEL))
    return pl.pallas_call(
        functools.partial(_ffn_body, final=final_g is not None),
        grid=(n // tm,),
        in_specs=in_specs,
        out_specs=row,
        out_shape=jax.ShapeDtypeStruct((n, D_MODEL), F32),
        compiler_params=_params(("parallel",)),
        name="ffn",
    )(*args)


def _s5_body(x_ref, h0_ref, g_ref, bbd_ref, cbd_ref, lre_ref, lim_ref, d_ref, wglu_ref,
             xo_ref, hl_ref, bu_scr, s_scr, *, nseq, seg):
    @pl.when(pl.program_id(1) == 0)
    def _():
        s_scr[...] = h0_ref[0]

    x = x_ref[0]
    u = _rms(x, g_ref[...])
    ub = u.astype(BF16)
    for k in range(SSM_BLOCKS):
        bu_scr[:, k * SSM_BLOCK_STATE:(k + 1) * SSM_BLOCK_STATE] = _dot(
            ub[:, k * SSM_BLOCK_IN:(k + 1) * SSM_BLOCK_IN], bbd_ref[k])

    for k in range(SSM_BLOCKS):
        re = slice(k * SSM_BLOCK_STATE, k * SSM_BLOCK_STATE + SSM_BLOCK_HALF)
        im = slice(k * SSM_BLOCK_STATE + SSM_BLOCK_HALF, (k + 1) * SSM_BLOCK_STATE)
        lr = lre_ref[:, k * SSM_BLOCK_HALF:(k + 1) * SSM_BLOCK_HALF]
        li = lim_ref[:, k * SSM_BLOCK_HALF:(k + 1) * SSM_BLOCK_HALF]

        def seq_body(sq, carry, re=re, im=im, lr=lr, li=li):
            def row_body(t, state):
                hr, hi = state
                r = sq * seg + t
                nr = lr * hr - li * hi + bu_scr[pl.ds(r, 1), re]
                ni = lr * hi + li * hr + bu_scr[pl.ds(r, 1), im]
                bu_scr[pl.ds(r, 1), re] = nr
                bu_scr[pl.ds(r, 1), im] = ni
                return nr, ni

            hr, hi = lax.fori_loop(0, seg, row_body,
                                   (s_scr[pl.ds(sq, 1), re], s_scr[pl.ds(sq, 1), im]), unroll=8)
            s_scr[pl.ds(sq, 1), re] = hr
            s_scr[pl.ds(sq, 1), im] = hi
            return carry

        lax.fori_loop(0, nseq, seq_body, 0)

    hl_ref[0] = s_scr[...]

    ys = [_dot(bu_scr[:, k * SSM_BLOCK_STATE:(k + 1) * SSM_BLOCK_STATE].astype(BF16), cbd_ref[k])
          for k in range(SSM_BLOCKS)]
    y = jnp.concatenate(ys, axis=-1) + d_ref[...] * u
    z = jax.nn.gelu(y, approximate=True).astype(BF16)
    zz = _dot(z, wglu_ref[...])
    xo_ref[0] = x + zz[:, :D_MODEL] * jax.nn.sigmoid(zz[:, D_MODEL:])


def _s5_weights(a_re, a_im, log_dt, b_re, b_im, c_re, c_im):
    lam = lax.complex(a_re.astype(F32), a_im.astype(F32))
    dt = jnp.exp(log_dt.astype(F32))[:, None]
    lam_bar = jnp.exp(lam * dt)
    b_bar = ((lam_bar - 1.0) / lam)[:, :, None] * lax.complex(b_re.astype(F32), b_im.astype(F32))
    eye = jnp.eye(SSM_GROUPS_PER_BLOCK, dtype=F32)
    bb = jnp.stack([jnp.real(b_bar), jnp.imag(b_bar)])
    bb = bb.reshape(2, SSM_BLOCKS, SSM_GROUPS_PER_BLOCK, SSM_STATE, SSM_GROUP)
    bbd = jnp.einsum("rkgph,gq->kghrqp", bb, eye).reshape(SSM_BLOCKS, SSM_BLOCK_IN, SSM_BLOCK_STATE)
    cc = jnp.stack([c_re.astype(F32), -c_im.astype(F32)])
    cc = cc.reshape(2, SSM_BLOCKS, SSM_GROUPS_PER_BLOCK, SSM_GROUP, SSM_STATE)
    cbd = jnp.einsum("rkghp,gq->krgpqh", cc, eye).reshape(SSM_BLOCKS, SSM_BLOCK_STATE, SSM_BLOCK_IN)
    lre = jnp.real(lam_bar).reshape(1, N_SSM_GROUPS * SSM_STATE)
    lim = jnp.imag(lam_bar).reshape(1, N_SSM_GROUPS * SSM_STATE)
    return bbd.astype(BF16), cbd.astype(BF16), lre, lim


def _state_to_lanes(h):
    b = h.shape[0]
    h = h.astype(F32).reshape(b, SSM_BLOCKS, SSM_GROUPS_PER_BLOCK, SSM_STATE, 2)
    return h.transpose(0, 1, 4, 2, 3).reshape(b, SSM_STATE_LANES)


def _lanes_to_state(h):
    b = h.shape[0]
    h = h.reshape(b, SSM_BLOCKS, 2, SSM_GROUPS_PER_BLOCK, SSM_STATE)
    return h.transpose(0, 1, 3, 4, 2).reshape(b, N_SSM_GROUPS, SSM_STATE, 2)


def _s5(x, h0, g, ssm_w, d_skip, w_glu, *, tm, nseq, seg):
    nb, rows, _ = x.shape
    bbd, cbd, lre, lim = ssm_w
    row = pl.BlockSpec((1, tm, D_MODEL), lambda b, i: (b, i, 0))
    st = pl.BlockSpec((1, nseq, SSM_STATE_LANES), lambda b, i: (b, 0, 0))
    return pl.pallas_call(
        functools.partial(_s5_body, nseq=nseq, seg=seg),
        grid=(nb, rows // tm),
        in_specs=[row, st, _const_spec((1, D_MODEL)), _const_spec(bbd.shape), _const_spec(cbd.shape),
                  _const_spec(lre.shape), _const_spec(lim.shape), _const_spec((1, D_MODEL)),
                  _const_spec(w_glu.shape)],
        out_specs=[row, st],
        out_shape=[jax.ShapeDtypeStruct(x.shape, F32), jax.ShapeDtypeStruct(h0.shape, F32)],
        scratch_shapes=[pltpu.VMEM((tm, SSM_STATE_LANES), F32), pltpu.VMEM((nseq, SSM_STATE_LANES), F32)],
        compiler_params=_params(("parallel", "arbitrary")),
        name="s5_mixer",
    )(x, h0, g.reshape(1, D_MODEL), bbd, cbd, lre, lim, d_skip.reshape(1, D_MODEL), w_glu)


def _rope_tables(pos):
    half = ROT_DIM // 2
    inv = ROPE_THETA ** (-jnp.arange(0, ROT_DIM, 2, dtype=F32) / ROT_DIM)
    ang = pos.astype(F32)[:, None] * inv[None, :]
    cos, sin = jnp.cos(ang), jnp.sin(ang)
    n = pos.shape[0]
    pad = HEAD_DIM - ROT_DIM
    c = jnp.concatenate([cos, cos, jnp.ones((n, pad), F32)], axis=-1)
    s1 = jnp.concatenate([-sin, jnp.zeros((n, half + pad), F32)], axis=-1)
    s2 = jnp.concatenate([jnp.zeros((n, half), F32), sin, jnp.zeros((n, pad), F32)], axis=-1)
    rep = LANES // HEAD_DIM
    return tuple(jnp.tile(t, (1, rep)) for t in (c, s1, s2))


def _proj_body(x_ref, g_ref, w_ref, c_ref, s1_ref, s2_ref, o_ref, *, rope_tiles):
    h = _rms(x_ref[...], g_ref[...]).astype(BF16)
    y = _dot(h, w_ref[...])
    c, s1, s2 = c_ref[...], s1_ref[...], s2_ref[...]
    half = ROT_DIM // 2
    for j in range(y.shape[1] // LANES):
        yj = y[:, j * LANES:(j + 1) * LANES]
        if j < rope_tiles:
            yj = yj * c + pltpu.roll(yj, LANES - half, 1) * s1 + pltpu.roll(yj, half, 1) * s2
        o_ref[:, j * LANES:(j + 1) * LANES] = yj


def _proj(x, g, w, tables, rope_width, tm=512):
    n = x.shape[0]
    tm = min(tm, n)
    width = w.shape[1]
    row = pl.BlockSpec((tm, D_MODEL), lambda i: (i, 0))
    tab = pl.BlockSpec((tm, LANES), lambda i: (i, 0))
    return pl.pallas_call(
        functools.partial(_proj_body, rope_tiles=rope_width // LANES),
        grid=(n // tm,),
        in_specs=[row, _const_spec((1, D_MODEL)), _const_spec(w.shape), tab, tab, tab],
        out_specs=pl.BlockSpec((tm, width), lambda i: (i, 0)),
        out_shape=jax.ShapeDtypeStruct((n, width), F32),
        compiler_params=_params(("parallel",)),
        name="proj_rope",
    )(x, g.reshape(1, D_MODEL), w, *tables)


def _swa_body(q_ref, kp_ref, kc_ref, vp_ref, vc_ref, o_ref, lse_ref, *, blocks_per_seq):
    n = pl.program_id(1)
    q = q_ref[0]
    k = jnp.concatenate([kp_ref[0], kc_ref[0]], axis=0)
    v = jnp.concatenate([vp_ref[0], vc_ref[0]], axis=0)
    qi = lax.broadcasted_iota(jnp.int32, (SWA_BLOCK, 2 * SWA_BLOCK), 0)
    kj = lax.broadcasted_iota(jnp.int32, (SWA_BLOCK, 2 * SWA_BLOCK), 1)
    has_prev = (n % blocks_per_seq) != 0
    valid = (kj >= qi) & (kj <= qi + SWA_BLOCK) & ((kj >= SWA_BLOCK) | has_prev)
    for h in range(HEADS_PER_BRANCH):
        hs = slice(h * HEAD_DIM, (h + 1) * HEAD_DIM)
        s = lax.dot_general(q[:, hs], k[:, hs], (((1,), (1,)), ((), ())),
                            preferred_element_type=F32) * (HEAD_DIM ** -0.5)
        s = jnp.where(valid, s, NEG_INF)
        m = jnp.max(s, axis=-1, keepdims=True)
        p = jnp.exp(s - m)
        l = jnp.sum(p, axis=-1, keepdims=True)
        o_ref[0, :, hs] = _dot(p.astype(BF16), v[:, hs]) / l
        lse_ref[0, :, hs] = jnp.broadcast_to(m + jnp.log(l), (SWA_BLOCK, HEAD_DIM))


def _swa(q, k, v, dilation):
    b, l, w = q.shape
    nblk = l // SWA_BLOCK
    cur = pl.BlockSpec((1, SWA_BLOCK, w), lambda bi, n: (bi, n, 0))
    prev = pl.BlockSpec((1, SWA_BLOCK, w), lambda bi, n: (bi, jnp.maximum(n - 1, 0), 0))
    out = jax.ShapeDtypeStruct((b, l, w), F32)
    return pl.pallas_call(
        functools.partial(_swa_body, blocks_per_seq=nblk // dilation),
        grid=(b, nblk),
        in_specs=[cur, prev, cur, prev, cur],
        out_specs=[cur, cur],
        out_shape=[out, out],
        compiler_params=_params(("parallel", "arbitrary")),
        name="swa",
    )(q, k, k, v, v)


def _deinterleave(x, d):
    b, l, w = x.shape
    return x.reshape(b, l // d, d, w).transpose(0, 2, 1, 3).reshape(b, l, w)


def _interleave(x, d):
    b, l, w = x.shape
    return x.reshape(b, d, l // d, w).transpose(0, 2, 1, 3).reshape(b, l, w)


def _merge_body(x_ref, o0_ref, o1_ref, o2_ref, l0_ref, l1_ref, l2_ref, wo_ref, out_ref):
    l0, l1, l2 = l0_ref[...], l1_ref[...], l2_ref[...]
    m = jnp.maximum(jnp.maximum(l0, l1), l2)
    e0, e1, e2 = jnp.exp(l0 - m), jnp.exp(l1 - m), jnp.exp(l2 - m)
    o = (e0 * o0_ref[...] + e1 * o1_ref[...] + e2 * o2_ref[...]) / (e0 + e1 + e2)
    out_ref[...] = x_ref[...] + _dot(o.astype(BF16), wo_ref[...])


def _merge(x, os, ls, w_o, tm=512):
    n = x.shape[0]
    row = pl.BlockSpec((tm, D_MODEL), lambda i: (i, 0))
    att = pl.BlockSpec((tm, MERGED_WIDTH), lambda i: (i, 0))
    return pl.pallas_call(
        _merge_body,
        grid=(n // tm,),
        in_specs=[row] + [att] * 6 + [_const_spec(w_o.shape)],
        out_specs=row,
        out_shape=jax.ShapeDtypeStruct((n, D_MODEL), F32),
        compiler_params=_params(("parallel",)),
        name="merge_wo",
    )(x, *os, *ls, w_o)


def _decode_attn_body(q_ref, kn_ref, vn_ref, k0_ref, k1_ref, k2_ref, v0_ref, v1_ref, v2_ref, o_ref,
                      *, past, steps):
    hb = HEADS_PER_BRANCH
    scale = HEAD_DIM ** -0.5
    row = lax.broadcasted_iota(jnp.int32, (WINDOWS[0], hb, 1), 0)
    new = lax.broadcasted_iota(jnp.int32, (steps, hb, 1), 0)

    def query_body(i, carry):
        svals, vvals = [], []
        for br, d in enumerate(DILATIONS):
            heads = slice(br * hb, (br + 1) * hb)
            q = q_ref[0, i, heads, :]
            if br == 0:
                kc, vc, ok = k0_ref[0], v0_ref[0], row >= i
            elif br == 1:
                kc, vc, ok = k1_ref[0, :, i % d], v1_ref[0, :, i % d], row >= i // d
            else:
                kc, vc, ok = k2_ref[0, :, i], v2_ref[0, :, i], row >= 0
            s = jnp.sum(kc * q[None], axis=-1, keepdims=True) * scale
            svals.append(jnp.where(ok, s, NEG_INF))
            vvals.append(vc)
            kn, vn = kn_ref[0, :, heads, :], vn_ref[0, :, heads, :]
            s = jnp.sum(kn * q[None], axis=-1, keepdims=True) * scale
            ok = (new <= i) & ((i - new) % d == 0)
            svals.append(jnp.where(ok, s, NEG_INF))
            vvals.append(vn)
        m = functools.reduce(jnp.maximum, [jnp.max(s, axis=0, keepdims=True) for s in svals])
        num = jnp.zeros((1, hb, HEAD_DIM), F32)
        den = jnp.zeros((1, hb, 1), F32)
        for s, v in zip(svals, vvals):
            p = jnp.exp(s - m)
            den = den + jnp.sum(p, axis=0, keepdims=True)
            num = num + jnp.sum(p * v, axis=0, keepdims=True)
        o_ref[0, pl.ds(i, 1)] = num / den
        return carry

    lax.fori_loop(0, steps, query_body, 0)


def _decode_attn(q, kn, vn, cache_k, cache_v):
    b, steps = q.shape[0], q.shape[1]
    past = cache_k.shape[1]
    assert past == MAX_WINDOW and steps <= DILATIONS[2] // 2
    hb = HEADS_PER_BRANCH
    d1, d2 = DILATIONS[1], DILATIONS[2]
    new = pl.BlockSpec((1, steps, N_HEADS, HEAD_DIM), lambda i: (i, 0, 0, 0))
    c0 = pl.BlockSpec((1, WINDOWS[0], hb, HEAD_DIM), lambda i: (i, past // WINDOWS[0] - 1, 0, 0))
    c1 = pl.BlockSpec((1, WINDOWS[1] // d1, d1, hb, HEAD_DIM),
                      lambda i: (i, past // WINDOWS[1] - 1, 0, 1, 0))
    c2 = pl.BlockSpec((1, past // d2, d2 // 2, hb, HEAD_DIM), lambda i: (i, 0, 0, 2, 0))
    v1 = lambda c: c.reshape(b, past // d1, d1, N_HEADS, HEAD_DIM)
    v2 = lambda c: c.reshape(b, past // d2, d2, N_HEADS, HEAD_DIM)
    return pl.pallas_call(
        functools.partial(_decode_attn_body, past=past, steps=steps),
        grid=(b,),
        in_specs=[new, new, new, c0, c1, c2, c0, c1, c2],
        out_specs=pl.BlockSpec((1, steps, hb, HEAD_DIM), lambda i: (i, 0, 0, 0)),
        out_shape=jax.ShapeDtypeStruct((b, steps, hb, HEAD_DIM), F32),
        compiler_params=_params(("parallel",)),
        name="decode_attn",
    )(q, kn, vn, cache_k, v1(cache_k), v2(cache_k), cache_v, v1(cache_v), v2(cache_v))


def _addproj_body(x_ref, o_ref, wo_ref, out_ref):
    out_ref[...] = x_ref[...] + _dot(o_ref[...].astype(BF16), wo_ref[...])


def _addproj(x, o, w_o):
    n = x.shape[0]
    return pl.pallas_call(
        _addproj_body,
        grid=(1,),
        in_specs=[_const_spec(x.shape), _const_spec(o.shape), _const_spec(w_o.shape)],
        out_specs=pl.BlockSpec(x.shape, lambda i: (0, 0)),
        out_shape=jax.ShapeDtypeStruct((n, D_MODEL), F32),
        compiler_params=_params(("arbitrary",)),
        name="add_wo",
    )(x, o, w_o)


def _cache_roll_body(cache_ref, new_ref, out_ref, sem, *, batch, keep, steps):
    def copies(b):
        return (pltpu.make_async_copy(cache_ref.at[b, pl.ds(steps, keep - steps)],
                                      out_ref.at[b, pl.ds(0, keep - steps)], sem.at[0]),
                pltpu.make_async_copy(new_ref.at[b], out_ref.at[b, pl.ds(keep - steps, steps)], sem.at[1]))

    for b in range(batch):
        for c in copies(b):
            c.start()
    for b in range(batch):
        for c in copies(b):
            c.wait()


def _cache_roll(cache, new):
    batch, keep = cache.shape[0], cache.shape[1]
    steps = new.shape[1]
    any_spec = pl.BlockSpec(memory_space=pl.ANY)
    return pl.pallas_call(
        functools.partial(_cache_roll_body, batch=batch, keep=keep, steps=steps),
        in_specs=[any_spec, any_spec],
        out_specs=any_spec,
        out_shape=jax.ShapeDtypeStruct(cache.shape, cache.dtype),
        scratch_shapes=[pltpu.SemaphoreType.DMA((2,))],
        name="cache_roll",
    )(cache, new)


def kernel(x_prompt, x_sample, state_ssm, cache_k, cache_v, ffn_norm1, ffn_w1_in, ffn_w1_out, ffn_norm2,
           ffn_w2_in, ffn_w2_out, mix_norm, ssm_a_re, ssm_a_im, ssm_log_dt, ssm_b_re, ssm_b_im, ssm_c_re,
           ssm_c_im, ssm_d, ssm_w_glu, kv_norm, w_kv, w_q, w_o, final_norm):
    bp, sp, _ = x_prompt.shape
    bs, ss, _ = x_sample.shape
    past = cache_k.shape[1]

    w1_in, w1_out = ffn_w1_in.astype(BF16), ffn_w1_out.astype(BF16)
    w2_in, w2_out = ffn_w2_in.astype(BF16), ffn_w2_out.astype(BF16)
    w_glu, w_kv_b, w_q_b, w_o_b = (ssm_w_glu[0].astype(BF16), w_kv.astype(BF16), w_q[0].astype(BF16),
                                   w_o[0].astype(BF16))
    ssm_w = _s5_weights(ssm_a_re[0], ssm_a_im[0], ssm_log_dt[0], ssm_b_re[0], ssm_b_im[0], ssm_c_re[0],
                        ssm_c_im[0])

    def layer_a(x, h0, tm, nseq, seg):
        nb, rows, _ = x.shape
        x = _ffn(x.reshape(nb * rows, D_MODEL), ffn_norm1[0], w1_in[0], w1_out[0])
        x, h_last = _s5(x.reshape(nb, rows, D_MODEL), h0, mix_norm[0], ssm_w, ssm_d[0], w_glu,
                        tm=tm, nseq=nseq, seg=seg)
        x = _ffn(x.reshape(nb * rows, D_MODEL), ffn_norm2[0], w2_in[0], w2_out[0])
        return x, h_last

    pos_p = jnp.tile(jnp.arange(sp, dtype=jnp.int32), bp)
    tab_p = _rope_tables(pos_p)
    xp, hl_p = layer_a(x_prompt, jnp.zeros((bp, 1, SSM_STATE_LANES), F32), 256, 1, 256)
    kv_p = _proj(xp, kv_norm, w_kv_b, tab_p, ATTN_WIDTH)
    k_p = kv_p[:, :ATTN_WIDTH].reshape(bp, sp, ATTN_WIDTH)
    v_p = kv_p[:, ATTN_WIDTH:].reshape(bp, sp, ATTN_WIDTH)
    xp = _ffn(xp, ffn_norm1[1], w1_in[1], w1_out[1])
    q_p = _proj(xp, mix_norm[1], w_q_b, tab_p, ATTN_WIDTH).reshape(bp, sp, ATTN_WIDTH)
    outs, lses = [], []
    for br, d in enumerate(DILATIONS):
        cols = slice(br * MERGED_WIDTH, (br + 1) * MERGED_WIDTH)
        qd, kd, vd = (_deinterleave(t[:, :, cols].astype(BF16), d) for t in (q_p, k_p, v_p))
        o, lse = _swa(qd, kd, vd, d)
        outs.append(_interleave(o, d).reshape(bp * sp, MERGED_WIDTH))
        lses.append(_interleave(lse, d).reshape(bp * sp, MERGED_WIDTH))
    xp = _merge(xp, outs, lses, w_o_b)
    y_prompt = _ffn(xp, ffn_norm2[1], w2_in[1], w2_out[1], final_g=final_norm).reshape(bp, sp, D_MODEL)
    keep_p = min(MAX_WINDOW, sp)
    cache_k_prompt = k_p[:, sp - keep_p:].reshape(bp, keep_p, N_HEADS, HEAD_DIM)
    cache_v_prompt = v_p[:, sp - keep_p:].reshape(bp, keep_p, N_HEADS, HEAD_DIM)
    state_prompt = _lanes_to_state(hl_p.reshape(bp, SSM_STATE_LANES))[None]

    pos_s = jnp.tile(PAST_LEN + jnp.arange(ss, dtype=jnp.int32), bs)
    tab_s = _rope_tables(pos_s)
    h0_s = _state_to_lanes(state_ssm[0])[None]
    xs, hl_s = layer_a(x_sample.reshape(1, bs * ss, D_MODEL), h0_s, bs * ss, bs, ss)
    kv_s = _proj(xs, kv_norm, w_kv_b, tab_s, ATTN_WIDTH)
    k_s = kv_s[:, :ATTN_WIDTH].reshape(bs, ss, N_HEADS, HEAD_DIM)
    v_s = kv_s[:, ATTN_WIDTH:].reshape(bs, ss, N_HEADS, HEAD_DIM)
    xs = _ffn(xs, ffn_norm1[1], w1_in[1], w1_out[1])
    q_s = _proj(xs, mix_norm[1], w_q_b, tab_s, ATTN_WIDTH).reshape(bs, ss, N_HEADS, HEAD_DIM)
    o_s = _decode_attn(q_s, k_s, v_s, cache_k, cache_v)
    xs = _addproj(xs, o_s.reshape(bs * ss, MERGED_WIDTH), w_o_b)
    y_sample = _ffn(xs, ffn_norm2[1], w2_in[1], w2_out[1], final_g=final_norm).reshape(bs, ss, D_MODEL)
    cache_k_sample = _cache_roll(cache_k, k_s)
    cache_v_sample = _cache_roll(cache_v, v_s)
    state_sample = _lanes_to_state(hl_s.reshape(bs, SSM_STATE_LANES))[None]

    return (y_prompt, y_sample, state_prompt, cache_k_prompt, cache_v_prompt,
            state_sample, cache_k_sample, cache_v_sample)
```

```python
import functools

import jax
import jax.numpy as jnp
from jax import lax
from jax.experimental import pallas as pl
from jax.experimental.pallas import tpu as pltpu

F32 = jnp.float32
BF16 = jnp.bfloat16

D_MODEL = 1024
D_FF = 2816
SSM_GROUP = 16
N_SSM_GROUPS = D_MODEL // SSM_GROUP
SSM_STATE = 64
HEAD_DIM = 64
WINDOWS = (128, 512, 2048)
DILATIONS = (1, 4, 16)
N_BRANCHES = 3
HEADS_PER_BRANCH = 8
N_HEADS = N_BRANCHES * HEADS_PER_BRANCH
ATTN_WIDTH = N_HEADS * HEAD_DIM
MERGED_WIDTH = HEADS_PER_BRANCH * HEAD_DIM
MAX_WINDOW = max(WINDOWS)
ROT_DIM = HEAD_DIM // 4
ROPE_THETA = 500000.0
PAST_LEN = 16384
RMS_EPS = 1e-6
NEG_INF = -1e30

LANES = 128
SSM_BLOCKS = 4
SSM_GROUPS_PER_BLOCK = N_SSM_GROUPS // SSM_BLOCKS
SSM_BLOCK_IN = SSM_GROUPS_PER_BLOCK * SSM_GROUP
SSM_BLOCK_HALF = SSM_GROUPS_PER_BLOCK * SSM_STATE
SSM_BLOCK_STATE = 2 * SSM_BLOCK_HALF
SSM_STATE_LANES = SSM_BLOCKS * SSM_BLOCK_STATE

FF_CHUNK = 1408
SWA_BLOCK = 128

VMEM_LIMIT = 56 * 1024 * 1024


def _params(semantics):
    return pltpu.CompilerParams(dimension_semantics=semantics, vmem_limit_bytes=VMEM_LIMIT)


def _const_spec(shape):
    nd = len(shape)
    return pl.BlockSpec(shape, lambda *_: (0,) * nd, pipeline_mode=pl.Buffered(1))


def _rms(x, g):
    return x * lax.rsqrt(jnp.mean(x * x, axis=-1, keepdims=True) + RMS_EPS) * g


def _dot(a, b):
    return jnp.dot(a, b, preferred_element_type=F32)


def _dot_nt(a, b):
    return lax.dot_general(a, b, (((1,), (1,)), ((), ())), preferred_element_type=F32)


def _ffn_body(x_ref, g_ref, win_ref, wout_ref, *rest, final):
    if final:
        fg_ref, o_ref = rest
    else:
        (o_ref,) = rest
    x = x_ref[...]
    h = _rms(x, g_ref[...]).astype(BF16)
    acc = x
    for j in range(D_FF // FF_CHUNK):
        lo = j * FF_CHUNK
        gate = _dot(h, win_ref[:, lo:lo + FF_CHUNK])
        up = _dot(h, win_ref[:, D_FF + lo:D_FF + lo + FF_CHUNK])
        act = gate * jax.nn.sigmoid(gate) * up
        acc = acc + _dot((0.5 * act).astype(BF16), wout_ref[lo:lo + FF_CHUNK, :])
    if final:
        acc = _rms(acc, fg_ref[...])
    o_ref[...] = acc


def _ffn(x, g, w_in, w_out, final_g=None, tm=512):
    n = x.shape[0]
    tm = min(tm, n)
    row = pl.BlockSpec((tm, D_MODEL), lambda i: (i, 0))
    in_specs = [row, _const_spec((1, D_MODEL)), _const_spec(w_in.shape), _const_spec(w_out.shape)]
    args = [x, g.reshape(1, D_MODEL), w_in, w_out]
    if final_g is not None:
        in_specs.append(_const_spec((1, D_MODEL)))
        args.append(final_g.reshape(1, D_MODEL))
    return pl.pallas_call(
        functools.partial(_ffn_body, final=final_g is not None),
        grid=(n // tm,),
        in_specs=in_specs,
        out_specs=row,
        out_shape=jax.ShapeDtypeStruct((n, D_MODEL), F32),
        compiler_params=_params(("parallel",)),
        name="ffn",
    )(*args)


def _s5_body(x_ref, h0_ref, g_ref, bbd_ref, cbd_ref, lre_ref, lim_ref, d_ref, wglu_ref,
             xo_ref, hl_ref, bu_scr, s_scr, *, nseq, seg):
    @pl.when(pl.program_id(1) == 0)
    def _():
        s_scr[...] = h0_ref[0]

    x = x_ref[0]
    u = _rms(x, g_ref[...])
    ub = u.astype(BF16)
    for k in range(SSM_BLOCKS):
        bu_scr[:, k * SSM_BLOCK_STATE:(k + 1) * SSM_BLOCK_STATE] = _dot(
            ub[:, k * SSM_BLOCK_IN:(k + 1) * SSM_BLOCK_IN], bbd_ref[k])

    for k in range(SSM_BLOCKS):
        re = slice(k * SSM_BLOCK_STATE, k * SSM_BLOCK_STATE + SSM_BLOCK_HALF)
        im = slice(k * SSM_BLOCK_STATE + SSM_BLOCK_HALF, (k + 1) * SSM_BLOCK_STATE)
        lr = lre_ref[:, k * SSM_BLOCK_HALF:(k + 1) * SSM_BLOCK_HALF]
        li = lim_ref[:, k * SSM_BLOCK_HALF:(k + 1) * SSM_BLOCK_HALF]

        def seq_body(sq, carry, re=re, im=im, lr=lr, li=li):
            def row_body(t, state):
                hr, hi = state
                r = sq * seg + t
                nr = lr * hr - li * hi + bu_scr[pl.ds(r, 1), re]
                ni = lr * hi + li * hr + bu_scr[pl.ds(r, 1), im]
                bu_scr[pl.ds(r, 1), re] = nr
                bu_scr[pl.ds(r, 1), im] = ni
                return nr, ni

            hr, hi = lax.fori_loop(0, seg, row_body,
                                   (s_scr[pl.ds(sq, 1), re], s_scr[pl.ds(sq, 1), im]), unroll=8)
            s_scr[pl.ds(sq, 1), re] = hr
            s_scr[pl.ds(sq, 1), im] = hi
            return carry

        lax.fori_loop(0, nseq, seq_body, 0)

    hl_ref[0] = s_scr[...]

    ys = [_dot(bu_scr[:, k * SSM_BLOCK_STATE:(k + 1) * SSM_BLOCK_STATE].astype(BF16), cbd_ref[k])
          for k in range(SSM_BLOCKS)]
    y = jnp.concatenate(ys, axis=-1) + d_ref[...] * u
    z = jax.nn.gelu(y, approximate=True).astype(BF16)
    zz = _dot(z, wglu_ref[...])
    xo_ref[0] = x + zz[:, :D_MODEL] * jax.nn.sigmoid(zz[:, D_MODEL:])


def _s5_weights(a_re, a_im, log_dt, b_re, b_im, c_re, c_im):
    a_re, a_im = a_re.astype(F32), a_im.astype(F32)
    dt = jnp.exp(log_dt.astype(F32))[:, None]
    mag = jnp.exp(a_re * dt)
    lre, lim = mag * jnp.cos(a_im * dt), mag * jnp.sin(a_im * dt)
    den = a_re * a_re + a_im * a_im
    fre = ((lre - 1.0) * a_re + lim * a_im) / den
    fim = (lim * a_re - (lre - 1.0) * a_im) / den
    b_re, b_im = b_re.astype(F32), b_im.astype(F32)
    bb = jnp.stack([fre[:, :, None] * b_re - fim[:, :, None] * b_im,
                    fre[:, :, None] * b_im + fim[:, :, None] * b_re])
    eye = jnp.eye(SSM_GROUPS_PER_BLOCK, dtype=F32)
    bb = bb.reshape(2, SSM_BLOCKS, SSM_GROUPS_PER_BLOCK, SSM_STATE, SSM_GROUP)
    bbd = jnp.einsum("rkgph,gq->kghrqp", bb, eye).reshape(SSM_BLOCKS, SSM_BLOCK_IN, SSM_BLOCK_STATE)
    cc = jnp.stack([c_re.astype(F32), -c_im.astype(F32)])
    cc = cc.reshape(2, SSM_BLOCKS, SSM_GROUPS_PER_BLOCK, SSM_GROUP, SSM_STATE)
    cbd = jnp.einsum("rkghp,gq->krgpqh", cc, eye).reshape(SSM_BLOCKS, SSM_BLOCK_STATE, SSM_BLOCK_IN)
    lanes = N_SSM_GROUPS * SSM_STATE
    return bbd.astype(BF16), cbd.astype(BF16), lre.reshape(1, lanes), lim.reshape(1, lanes)


def _state_to_lanes(h):
    b = h.shape[0]
    h = h.astype(F32).reshape(b, SSM_BLOCKS, SSM_GROUPS_PER_BLOCK, SSM_STATE, 2)
    return h.transpose(0, 1, 4, 2, 3).reshape(b, SSM_STATE_LANES)


def _lanes_to_state(h):
    b = h.shape[0]
    h = h.reshape(b, SSM_BLOCKS, 2, SSM_GROUPS_PER_BLOCK, SSM_STATE)
    return h.transpose(0, 1, 3, 4, 2).reshape(b, N_SSM_GROUPS, SSM_STATE, 2)


def _s5(x, h0, g, ssm_w, d_skip, w_glu, *, tm, nseq, seg):
    nb, rows, _ = x.shape
    bbd, cbd, lre, lim = ssm_w
    row = pl.BlockSpec((1, tm, D_MODEL), lambda b, i: (b, i, 0))
    st = pl.BlockSpec((1, nseq, SSM_STATE_LANES), lambda b, i: (b, 0, 0))
    return pl.pallas_call(
        functools.partial(_s5_body, nseq=nseq, seg=seg),
        grid=(nb, rows // tm),
        in_specs=[row, st, _const_spec((1, D_MODEL)), _const_spec(bbd.shape), _const_spec(cbd.shape),
                  _const_spec(lre.shape), _const_spec(lim.shape), _const_spec((1, D_MODEL)),
                  _const_spec(w_glu.shape)],
        out_specs=[row, st],
        out_shape=[jax.ShapeDtypeStruct(x.shape, F32), jax.ShapeDtypeStruct(h0.shape, F32)],
        scratch_shapes=[pltpu.VMEM((tm, SSM_STATE_LANES), F32), pltpu.VMEM((nseq, SSM_STATE_LANES), F32)],
        compiler_params=_params(("parallel", "arbitrary")),
        name="s5_mixer",
    )(x, h0, g.reshape(1, D_MODEL), bbd, cbd, lre, lim, d_skip.reshape(1, D_MODEL), w_glu)


def _rope_tables(pos):
    half = ROT_DIM // 2
    inv = ROPE_THETA ** (-jnp.arange(0, ROT_DIM, 2, dtype=F32) / ROT_DIM)
    ang = pos.astype(F32)[:, None] * inv[None, :]
    cos, sin = jnp.cos(ang), jnp.sin(ang)
    n = pos.shape[0]
    pad = HEAD_DIM - ROT_DIM
    c = jnp.concatenate([cos, cos, jnp.ones((n, pad), F32)], axis=-1)
    s1 = jnp.concatenate([-sin, jnp.zeros((n, half + pad), F32)], axis=-1)
    s2 = jnp.concatenate([jnp.zeros((n, half), F32), sin, jnp.zeros((n, pad), F32)], axis=-1)
    rep = LANES // HEAD_DIM
    return tuple(jnp.tile(t, (1, rep)) for t in (c, s1, s2))


def _proj_body(x_ref, g_ref, w_ref, c_ref, s1_ref, s2_ref, *o_refs, rope_tiles):
    h = _rms(x_ref[...], g_ref[...]).astype(BF16)
    y = _dot(h, w_ref[...])
    c, s1, s2 = c_ref[...], s1_ref[...], s2_ref[...]
    half = ROT_DIM // 2
    tiles_per_out = y.shape[1] // LANES // len(o_refs)
    for j in range(y.shape[1] // LANES):
        yj = y[:, j * LANES:(j + 1) * LANES]
        if j < rope_tiles:
            yj = yj * c + pltpu.roll(yj, LANES - half, 1) * s1 + pltpu.roll(yj, half, 1) * s2
        jo = j % tiles_per_out
        o_refs[j // tiles_per_out][:, jo * LANES:(jo + 1) * LANES] = yj


def _proj(x, g, w, tables, rope_width, n_out, tm=512):
    n = x.shape[0]
    tm = min(tm, n)
    width = w.shape[1] // n_out
    row = pl.BlockSpec((tm, D_MODEL), lambda i: (i, 0))
    tab = pl.BlockSpec((tm, LANES), lambda i: (i, 0))
    return pl.pallas_call(
        functools.partial(_proj_body, rope_tiles=rope_width // LANES),
        grid=(n // tm,),
        in_specs=[row, _const_spec((1, D_MODEL)), _const_spec(w.shape), tab, tab, tab],
        out_specs=[pl.BlockSpec((tm, width), lambda i: (i, 0))] * n_out,
        out_shape=[jax.ShapeDtypeStruct((n, width), F32)] * n_out,
        compiler_params=_params(("parallel",)),
        name="proj_rope",
    )(x, g.reshape(1, D_MODEL), w, *tables)


def _swa_body(q_ref, kp_ref, kc_ref, vp_ref, vc_ref, o_ref, lse_ref, *, blocks_per_seq):
    n = pl.program_id(1)
    q = q_ref[0]
    k = jnp.concatenate([kp_ref[0], kc_ref[0]], axis=0)
    v = jnp.concatenate([vp_ref[0], vc_ref[0]], axis=0)
    qi = lax.broadcasted_iota(jnp.int32, (SWA_BLOCK, 2 * SWA_BLOCK), 0)
    kj = lax.broadcasted_iota(jnp.int32, (SWA_BLOCK, 2 * SWA_BLOCK), 1)
    has_prev = (n % blocks_per_seq) != 0
    valid = (kj >= qi) & (kj <= qi + SWA_BLOCK) & ((kj >= SWA_BLOCK) | has_prev)
    for h in range(HEADS_PER_BRANCH):
        hs = slice(h * HEAD_DIM, (h + 1) * HEAD_DIM)
        s = _dot_nt(q[:, hs], k[:, hs]) * (HEAD_DIM ** -0.5)
        s = jnp.where(valid, s, NEG_INF)
        m = jnp.max(s, axis=-1, keepdims=True)
        p = jnp.exp(s - m)
        l = jnp.sum(p, axis=-1, keepdims=True)
        o_ref[0, :, hs] = _dot(p.astype(BF16), v[:, hs]) / l
        lse_ref[0, :, hs] = jnp.broadcast_to(m + jnp.log(l), (SWA_BLOCK, HEAD_DIM))


def _swa(q, k, v, dilation):
    b, l, w = q.shape
    nblk = l // SWA_BLOCK
    cur = pl.BlockSpec((1, SWA_BLOCK, w), lambda bi, n: (bi, n, 0))
    prev = pl.BlockSpec((1, SWA_BLOCK, w), lambda bi, n: (bi, jnp.maximum(n - 1, 0), 0))
    out = jax.ShapeDtypeStruct((b, l, w), F32)
    return pl.pallas_call(
        functools.partial(_swa_body, blocks_per_seq=nblk // dilation),
        grid=(b, nblk),
        in_specs=[cur, prev, cur, prev, cur],
        out_specs=[cur, cur],
        out_shape=[out, out],
        compiler_params=_params(("parallel", "arbitrary")),
        name="swa",
    )(q, k, k, v, v)


def _deinterleave(x, d):
    b, l, w = x.shape
    return x.reshape(b, l // d, d, w).transpose(0, 2, 1, 3).reshape(b, l, w)


def _interleave(x, d):
    b, l, w = x.shape
    return x.reshape(b, d, l // d, w).transpose(0, 2, 1, 3).reshape(b, l, w)


def _merge_body(x_ref, o0_ref, o1_ref, o2_ref, l0_ref, l1_ref, l2_ref, wo_ref, out_ref):
    l0, l1, l2 = l0_ref[...], l1_ref[...], l2_ref[...]
    m = jnp.maximum(jnp.maximum(l0, l1), l2)
    e0, e1, e2 = jnp.exp(l0 - m), jnp.exp(l1 - m), jnp.exp(l2 - m)
    o = (e0 * o0_ref[...] + e1 * o1_ref[...] + e2 * o2_ref[...]) / (e0 + e1 + e2)
    out_ref[...] = x_ref[...] + _dot(o.astype(BF16), wo_ref[...])


def _merge(x, os, ls, w_o, tm=512):
    n = x.shape[0]
    row = pl.BlockSpec((tm, D_MODEL), lambda i: (i, 0))
    att = pl.BlockSpec((tm, MERGED_WIDTH), lambda i: (i, 0))
    return pl.pallas_call(
        _merge_body,
        grid=(n // tm,),
        in_specs=[row] + [att] * 6 + [_const_spec(w_o.shape)],
        out_specs=row,
        out_shape=jax.ShapeDtypeStruct((n, D_MODEL), F32),
        compiler_params=_params(("parallel",)),
        name="merge_wo",
    )(x, *os, *ls, w_o)


def _decode_body(q_ref, kn_ref, vn_ref, knt_ref, vnt_ref, ck_ref, cv_ref, o_ref, cko_ref, cvo_ref,
                 *, past, steps):
    scale = HEAD_DIM ** -0.5
    qi = lax.broadcasted_iota(jnp.int32, (steps, steps), 0)
    kj = lax.broadcasted_iota(jnp.int32, (steps, steps), 1)
    scores, values = [], []
    for br, d in enumerate(DILATIONS):
        w = WINDOWS[br]
        q = q_ref[0, 0, br].astype(BF16)
        kt = ck_ref[0, br, 0, :, past - w:].astype(BF16)
        back = (w + lax.broadcasted_iota(jnp.int32, (steps, w), 0)
                - lax.broadcasted_iota(jnp.int32, (steps, w), 1))
        ok = ((back & (d - 1)) == 0) & (back <= w)
        scores.append(jnp.where(ok, _dot(q, kt) * scale, NEG_INF))
        values.append((cv_ref[0, br, 0, :, past - w:].astype(BF16), True))
        ok = (kj <= qi) & (((qi - kj) & (d - 1)) == 0)
        scores.append(jnp.where(ok, _dot_nt(q, kn_ref[0, 0, br].astype(BF16)) * scale, NEG_INF))
        values.append((vn_ref[0, 0, br].astype(BF16), False))
    m = functools.reduce(jnp.maximum, [jnp.max(s, axis=-1, keepdims=True) for s in scores])
    num = jnp.zeros((steps, HEAD_DIM), F32)
    den = jnp.zeros((steps, 1), F32)
    for s, (v, transposed) in zip(scores, values):
        p = jnp.exp(s - m)
        den = den + jnp.sum(p, axis=-1, keepdims=True)
        num = num + (_dot_nt(p.astype(BF16), v) if transposed else _dot(p.astype(BF16), v))
    o_ref[0, 0] = num / den

    lane = lax.broadcasted_iota(jnp.int32, (HEAD_DIM, LANES), 1)
    tiles = past // LANES
    for src, new, dst in ((ck_ref, knt_ref, cko_ref), (cv_ref, vnt_ref, cvo_ref)):
        for br in range(N_BRANCHES):
            cur = pltpu.roll(src[0, br, 0, :, 0:LANES], LANES - steps, 1)
            for t in range(tiles):
                if t + 1 < tiles:
                    nxt = pltpu.roll(src[0, br, 0, :, (t + 1) * LANES:(t + 2) * LANES], LANES - steps, 1)
                else:
                    nxt = jnp.concatenate([jnp.zeros((HEAD_DIM, LANES - steps), F32), new[0, br, 0]], axis=1)
                dst[0, br, 0, :, t * LANES:(t + 1) * LANES] = jnp.where(lane < LANES - steps, cur, nxt)
                cur = nxt


def _decode(q, kn, vn, cache_k, cache_v):
    b, steps = q.shape[0], q.shape[1]
    past = cache_k.shape[1]
    assert past == MAX_WINDOW and steps <= LANES
    hb = HEADS_PER_BRANCH

    def slots(t):
        return t.reshape(b, steps, N_BRANCHES, hb, HEAD_DIM).transpose(0, 3, 2, 1, 4)

    def slots_t(t):
        return t.reshape(b, steps, N_BRANCHES, hb, HEAD_DIM).transpose(0, 2, 3, 4, 1)

    def cache_t(c):
        return c.transpose(0, 2, 3, 1).reshape(b, N_BRANCHES, hb, HEAD_DIM, past)

    new = pl.BlockSpec((1, 1, N_BRANCHES, steps, HEAD_DIM), lambda i, h: (i, h, 0, 0, 0))
    new_t = pl.BlockSpec((1, N_BRANCHES, 1, HEAD_DIM, steps), lambda i, h: (i, 0, h, 0, 0))
    cache = pl.BlockSpec((1, N_BRANCHES, 1, HEAD_DIM, past), lambda i, h: (i, 0, h, 0, 0))
    cache_shape = jax.ShapeDtypeStruct((b, N_BRANCHES, hb, HEAD_DIM, past), F32)
    o, ck, cv = pl.pallas_call(
        functools.partial(_decode_body, past=past, steps=steps),
        grid=(b, hb),
        in_specs=[new, new, new, new_t, new_t, cache, cache],
        out_specs=[pl.BlockSpec((1, 1, steps, HEAD_DIM), lambda i, h: (i, h, 0, 0)), cache, cache],
        out_shape=[jax.ShapeDtypeStruct((b, hb, steps, HEAD_DIM), F32), cache_shape, cache_shape],
        compiler_params=_params(("parallel", "parallel")),
        name="decode_attn_roll",
    )(slots(q), slots(kn), slots(vn), slots_t(kn), slots_t(vn), cache_t(cache_k), cache_t(cache_v))
    o = o.transpose(0, 2, 1, 3).reshape(b * steps, MERGED_WIDTH)
    ck, cv = (c.reshape(b, N_HEADS, HEAD_DIM, past).transpose(0, 3, 1, 2) for c in (ck, cv))
    return o, ck, cv


def _addproj_body(x_ref, o_ref, wo_ref, out_ref):
    out_ref[...] = x_ref[...] + _dot(o_ref[...].astype(BF16), wo_ref[...])


def _addproj(x, o, w_o):
    n = x.shape[0]
    return pl.pallas_call(
        _addproj_body,
        grid=(1,),
        in_specs=[_const_spec(x.shape), _const_spec(o.shape), _const_spec(w_o.shape)],
        out_specs=pl.BlockSpec(x.shape, lambda i: (0, 0)),
        out_shape=jax.ShapeDtypeStruct((n, D_MODEL), F32),
        compiler_params=_params(("arbitrary",)),
        name="add_wo",
    )(x, o, w_o)


def kernel(x_prompt, x_sample, state_ssm, cache_k, cache_v, ffn_norm1, ffn_w1_in, ffn_w1_out, ffn_norm2,
           ffn_w2_in, ffn_w2_out, mix_norm, ssm_a_re, ssm_a_im, ssm_log_dt, ssm_b_re, ssm_b_im, ssm_c_re,
           ssm_c_im, ssm_d, ssm_w_glu, kv_norm, w_kv, w_q, w_o, final_norm):
    bp, sp, _ = x_prompt.shape
    bs, ss, _ = x_sample.shape

    w1_in, w1_out = ffn_w1_in.astype(BF16), ffn_w1_out.astype(BF16)
    w2_in, w2_out = ffn_w2_in.astype(BF16), ffn_w2_out.astype(BF16)
    w_glu, w_kv_b, w_q_b, w_o_b = (ssm_w_glu[0].astype(BF16), w_kv.astype(BF16), w_q[0].astype(BF16),
                                   w_o[0].astype(BF16))
    ssm_w = _s5_weights(ssm_a_re[0], ssm_a_im[0], ssm_log_dt[0], ssm_b_re[0], ssm_b_im[0], ssm_c_re[0],
                        ssm_c_im[0])

    def layer_a(x, h0, tm, nseq, seg):
        nb, rows, _ = x.shape
        x = _ffn(x.reshape(nb * rows, D_MODEL), ffn_norm1[0], w1_in[0], w1_out[0])
        x, h_last = _s5(x.reshape(nb, rows, D_MODEL), h0, mix_norm[0], ssm_w, ssm_d[0], w_glu,
                        tm=tm, nseq=nseq, seg=seg)
        x = _ffn(x.reshape(nb * rows, D_MODEL), ffn_norm2[0], w2_in[0], w2_out[0])
        return x, h_last

    pos_p = jnp.tile(jnp.arange(sp, dtype=jnp.int32), bp)
    tab_p = _rope_tables(pos_p)
    xp, hl_p = layer_a(x_prompt, jnp.zeros((bp, 1, SSM_STATE_LANES), F32), 256, 1, 256)
    k_p, v_p = (t.reshape(bp, sp, ATTN_WIDTH) for t in _proj(xp, kv_norm, w_kv_b, tab_p, ATTN_WIDTH, 2))
    xp = _ffn(xp, ffn_norm1[1], w1_in[1], w1_out[1])
    q_p = _proj(xp, mix_norm[1], w_q_b, tab_p, ATTN_WIDTH, 1)[0].reshape(bp, sp, ATTN_WIDTH)
    outs, lses = [], []
    for br, d in enumerate(DILATIONS):
        cols = slice(br * MERGED_WIDTH, (br + 1) * MERGED_WIDTH)
        qd, kd, vd = (_deinterleave(t[:, :, cols].astype(BF16), d) for t in (q_p, k_p, v_p))
        o, lse = _swa(qd, kd, vd, d)
        outs.append(_interleave(o, d).reshape(bp * sp, MERGED_WIDTH))
        lses.append(_interleave(lse, d).reshape(bp * sp, MERGED_WIDTH))
    xp = _merge(xp, outs, lses, w_o_b)
    y_prompt = _ffn(xp, ffn_norm2[1], w2_in[1], w2_out[1], final_g=final_norm).reshape(bp, sp, D_MODEL)
    keep_p = min(MAX_WINDOW, sp)
    cache_k_prompt = k_p[:, sp - keep_p:].reshape(bp, keep_p, N_HEADS, HEAD_DIM)
    cache_v_prompt = v_p[:, sp - keep_p:].reshape(bp, keep_p, N_HEADS, HEAD_DIM)
    state_prompt = _lanes_to_state(hl_p.reshape(bp, SSM_STATE_LANES))[None]

    pos_s = jnp.tile(PAST_LEN + jnp.arange(ss, dtype=jnp.int32), bs)
    tab_s = _rope_tables(pos_s)
    h0_s = _state_to_lanes(state_ssm[0])[None]
    xs, hl_s = layer_a(x_sample.reshape(1, bs * ss, D_MODEL), h0_s, bs * ss, bs, ss)
    k_s, v_s = (t.reshape(bs, ss, N_HEADS, HEAD_DIM) for t in _proj(xs, kv_norm, w_kv_b, tab_s, ATTN_WIDTH, 2))
    xs = _ffn(xs, ffn_norm1[1], w1_in[1], w1_out[1])
    q_s = _proj(xs, mix_norm[1], w_q_b, tab_s, ATTN_WIDTH, 1)[0].reshape(bs, ss, N_HEADS, HEAD_DIM)
    o_s, cache_k_sample, cache_v_sample = _decode(q_s, k_s, v_s, cache_k, cache_v)
    xs = _addproj(xs, o_s, w_o_b)
    y_sample = _ffn(xs, ffn_norm2[1], w2_in[1], w2_out[1], final_g=final_norm).reshape(bs, ss, D_MODEL)
    state_sample = _lanes_to_state(hl_s.reshape(bs, SSM_STATE_LANES))[None]

    return (y_prompt, y_sample, state_prompt, cache_k_prompt, cache_v_prompt,
            state_sample, cache_k_sample, cache_v_sample)
```

```python
import functools

import jax
import jax.numpy as jnp
from jax import lax
from jax.experimental import pallas as pl
from jax.experimental.pallas import tpu as pltpu

F32 = jnp.float32
BF16 = jnp.bfloat16

D_MODEL = 1024
D_FF = 2816
SSM_GROUP = 16
N_SSM_GROUPS = D_MODEL // SSM_GROUP
SSM_STATE = 64
HEAD_DIM = 64
WINDOWS = (128, 512, 2048)
DILATIONS = (1, 4, 16)
N_BRANCHES = 3
HEADS_PER_BRANCH = 8
N_HEADS = N_BRANCHES * HEADS_PER_BRANCH
ATTN_WIDTH = N_HEADS * HEAD_DIM
MERGED_WIDTH = HEADS_PER_BRANCH * HEAD_DIM
MAX_WINDOW = max(WINDOWS)
ROT_DIM = HEAD_DIM // 4
ROPE_THETA = 500000.0
PAST_LEN = 16384
RMS_EPS = 1e-6
NEG_INF = -1e30

LANES = 128
SUBLANES = 8
BRANCH_TILES = MERGED_WIDTH // LANES

SSM_BLOCKS = 4
SSM_GROUPS_PER_BLOCK = N_SSM_GROUPS // SSM_BLOCKS
SSM_BLOCK_IN = SSM_GROUPS_PER_BLOCK * SSM_GROUP
SSM_BLOCK_HALF = SSM_GROUPS_PER_BLOCK * SSM_STATE
SSM_BLOCK_STATE = 2 * SSM_BLOCK_HALF
SSM_SLABS = N_SSM_GROUPS * SSM_STATE // LANES
SSM_SLABS_PER_BLOCK = SSM_SLABS // SSM_BLOCKS
SCAN_RADIX = 4
SCAN_BASE = SUBLANES
SCAN_UNROLL = 4

FF_CHUNK = 1408
SWA_BLOCK = 128

VMEM_LIMIT = 56 * 1024 * 1024


def _params(semantics):
    return pltpu.CompilerParams(dimension_semantics=semantics, vmem_limit_bytes=VMEM_LIMIT)


def _const_spec(shape):
    nd = len(shape)
    return pl.BlockSpec(shape, lambda *_: (0,) * nd, pipeline_mode=pl.Buffered(1))


def _rms(x, g):
    return x * lax.rsqrt(jnp.mean(x * x, axis=-1, keepdims=True) + RMS_EPS) * g


def _dot(a, b):
    return jnp.dot(a, b, preferred_element_type=F32)


def _dot_nt(a, b):
    return lax.dot_general(a, b, (((1,), (1,)), ((), ())), preferred_element_type=F32)


def _ffn_body(x_ref, g_ref, win_ref, wout_ref, *rest, final):
    if final:
        fg_ref, o_ref = rest
    else:
        (o_ref,) = rest
    x = x_ref[...]
    h = _rms(x, g_ref[...]).astype(BF16)
    acc = x
    for j in range(D_FF // FF_CHUNK):
        lo = j * FF_CHUNK
        gate = _dot(h, win_ref[:, lo:lo + FF_CHUNK])
        up = _dot(h, win_ref[:, D_FF + lo:D_FF + lo + FF_CHUNK])
        act = gate * jax.nn.sigmoid(gate) * up
        acc = acc + _dot((0.5 * act).astype(BF16), wout_ref[lo:lo + FF_CHUNK, :])
    if final:
        acc = _rms(acc, fg_ref[...])
    o_ref[...] = acc


def _ffn(x, g, w_in, w_out, final_g=None, tm=512):
    n = x.shape[0]
    tm = min(tm, n)
    row = pl.BlockSpec((tm, D_MODEL), lambda i: (i, 0))
    in_specs = [row, _const_spec((1, D_MODEL)), _const_spec(w_in.shape), _const_spec(w_out.shape)]
    args = [x, g.reshape(1, D_MODEL), w_in, w_out]
    if final_g is not None:
        in_specs.append(_const_spec((1, D_MODEL)))
        args.append(final_g.reshape(1, D_MODEL))
    return pl.pallas_call(
        functools.partial(_ffn_body, final=final_g is not None),
        grid=(n // tm,),
        in_specs=in_specs,
        out_specs=row,
        out_shape=jax.ShapeDtypeStruct((n, D_MODEL), F32),
        compiler_params=_params(("parallel",)),
        name="ffn",
    )(*args)


def _scan_levels(tm):
    levels, n = 0, tm
    while n > 1:
        assert n % SCAN_RADIX == 0
        n //= SCAN_RADIX
        levels += 1
    return levels


def _scan_powers(tm):
    pows = [SCAN_RADIX ** lvl * m for lvl in range(_scan_levels(tm)) for m in range(1, SCAN_RADIX)]
    return pows + [tm]


def _cmul(ar, ai, br, bi):
    return ar * br - ai * bi, ar * bi + ai * br


def _blocked_scan(hre, him, pw_ref, lv_re, lv_im, slab, slot, tm):
    levels = _scan_levels(tm)
    top = SCAN_BASE - 1

    def views(lvl):
        if lvl == 0:
            return (lambda sl: hre[slab, sl, :], lambda sl: him[slab, sl, :],
                    lambda sl, v: hre.__setitem__((slab, sl, slice(None)), v),
                    lambda sl, v: him.__setitem__((slab, sl, slice(None)), v))
        a, b = lv_re[lvl - 1], lv_im[lvl - 1]
        return (lambda sl: a[slot, sl, :], lambda sl: b[slot, sl, :],
                lambda sl, v: a.__setitem__((slot, sl, slice(None)), v),
                lambda sl, v: b.__setitem__((slot, sl, slice(None)), v))

    def power(k):
        return pw_ref[slab, pl.ds(2 * k, 1), :], pw_ref[slab, pl.ds(2 * k + 1, 1), :]

    def elems(n, tau):
        return pl.ds(SCAN_BASE + tau, n, stride=SCAN_RADIX)

    s_re, s_im = hre[slab, pl.ds(top, 1), :], him[slab, pl.ds(top, 1), :]
    for lvl in range(levels):
        get_re, get_im, put_re, put_im = views(lvl)
        n = tm // SCAN_RADIX ** (lvl + 1)
        lr, li = power((SCAN_RADIX - 1) * lvl)
        pr, pi = get_re(elems(n, 0)), get_im(elems(n, 0))
        for tau in range(1, SCAN_RADIX):
            mr, mi = _cmul(lr, li, pr, pi)
            pr, pi = mr + get_re(elems(n, tau)), mi + get_im(elems(n, tau))
            put_re(elems(n, tau), pr)
            put_im(elems(n, tau), pi)
        if lvl + 1 < levels:
            _, _, nput_re, nput_im = views(lvl + 1)
            nput_re(pl.ds(SCAN_BASE, n), pr)
            nput_im(pl.ds(SCAN_BASE, n), pi)
            nput_re(pl.ds(top, 1), s_re)
            nput_im(pl.ds(top, 1), s_im)
    for lvl in reversed(range(levels)):
        get_re, get_im, put_re, put_im = views(lvl)
        n = tm // SCAN_RADIX ** (lvl + 1)
        last = lvl + 1 == levels
        if not last:
            nget_re, nget_im, _, _ = views(lvl + 1)
            put_re(elems(n, SCAN_RADIX - 1), nget_re(pl.ds(SCAN_BASE, n)))
            put_im(elems(n, SCAN_RADIX - 1), nget_im(pl.ds(SCAN_BASE, n)))
        before = pl.ds(top, n, stride=SCAN_RADIX)
        tr, ti = get_re(before), get_im(before)
        for tau in range(SCAN_RADIX if last else SCAN_RADIX - 1):
            k = (SCAN_RADIX - 1) * lvl + tau if tau < SCAN_RADIX - 1 else (SCAN_RADIX - 1) * levels
            mr, mi = _cmul(*power(k), tr, ti)
            put_re(elems(n, tau), get_re(elems(n, tau)) + mr)
            put_im(elems(n, tau), get_im(elems(n, tau)) + mi)
    hre[slab, pl.ds(top, 1), :] = hre[slab, pl.ds(top + tm, 1), :]
    him[slab, pl.ds(top, 1), :] = him[slab, pl.ds(top + tm, 1), :]


def _s5_body(x_ref, h0re_ref, h0im_ref, g_ref, bbd_ref, cbd_ref, pw_ref, d_ref, wglu_ref,
             xo_ref, hlre_ref, hlim_ref, hre, him, *levels_scr, tm, seg):
    nseq = tm // seg
    rows = pl.ds(SCAN_BASE, tm)
    top = SCAN_BASE - 1
    n_lv = len(levels_scr) // 2
    lv_re, lv_im = levels_scr[:n_lv], levels_scr[n_lv:]

    if nseq == 1:
        @pl.when(pl.program_id(1) == 0)
        def _():
            hre[:, pl.ds(top, 1), :] = h0re_ref[0]
            him[:, pl.ds(top, 1), :] = h0im_ref[0]

    x = x_ref[0]
    u = _rms(x, g_ref[...])
    ub = u.astype(BF16)
    for k in range(SSM_BLOCKS):
        bu = _dot(ub[:, k * SSM_BLOCK_IN:(k + 1) * SSM_BLOCK_IN], bbd_ref[k])
        for j in range(SSM_SLABS_PER_BLOCK):
            hre[k * SSM_SLABS_PER_BLOCK + j, rows, :] = bu[:, j * LANES:(j + 1) * LANES]
            him[k * SSM_SLABS_PER_BLOCK + j, rows, :] = bu[:, SSM_BLOCK_HALF + j * LANES:
                                                          SSM_BLOCK_HALF + (j + 1) * LANES]

    if nseq == 1:
        def slabs_body(it, carry):
            for slot in range(SCAN_UNROLL):
                _blocked_scan(hre, him, pw_ref, lv_re, lv_im, it * SCAN_UNROLL + slot, slot, tm)
            return carry

        lax.fori_loop(0, SSM_SLABS // SCAN_UNROLL, slabs_body, 0)
        hlre_ref[0] = hre[:, pl.ds(top, 1), :]
        hlim_ref[0] = him[:, pl.ds(top, 1), :]
    else:
        def slab_body(p, carry):
            lr, li = pw_ref[p, pl.ds(0, 1), :], pw_ref[p, pl.ds(1, 1), :]
            hr, hi = h0re_ref[0, p], h0im_ref[0, p]
            for t in range(seg):
                step = pl.ds(SCAN_BASE + t, nseq, stride=seg)
                mr, mi = _cmul(lr, li, hr, hi)
                hr, hi = mr + hre[p, step, :], mi + him[p, step, :]
                hre[p, step, :] = hr
                him[p, step, :] = hi
            hlre_ref[0, p] = hr
            hlim_ref[0, p] = hi
            return carry

        lax.fori_loop(0, SSM_SLABS, slab_body, 0)

    ys = []
    for k in range(SSM_BLOCKS):
        sl = range(k * SSM_SLABS_PER_BLOCK, (k + 1) * SSM_SLABS_PER_BLOCK)
        hk = jnp.concatenate([hre[p, rows, :] for p in sl] + [him[p, rows, :] for p in sl], axis=-1)
        ys.append(_dot(hk.astype(BF16), cbd_ref[k]))
    y = jnp.concatenate(ys, axis=-1) + d_ref[...] * u
    z = jax.nn.gelu(y, approximate=True).astype(BF16)
    zz = _dot(z, wglu_ref[...])
    xo_ref[0] = x + zz[:, :D_MODEL] * jax.nn.sigmoid(zz[:, D_MODEL:])


def _s5_weights(a_re, a_im, log_dt, b_re, b_im, c_re, c_im, powers):
    a_re, a_im = a_re.astype(F32), a_im.astype(F32)
    dt = jnp.exp(log_dt.astype(F32))[:, None]
    mag = jnp.exp(a_re * dt)
    lre, lim = mag * jnp.cos(a_im * dt), mag * jnp.sin(a_im * dt)
    den = a_re * a_re + a_im * a_im
    fre = ((lre - 1.0) * a_re + lim * a_im) / den
    fim = (lim * a_re - (lre - 1.0) * a_im) / den
    b_re, b_im = b_re.astype(F32), b_im.astype(F32)
    bb = jnp.stack([fre[:, :, None] * b_re - fim[:, :, None] * b_im,
                    fre[:, :, None] * b_im + fim[:, :, None] * b_re])
    eye = jnp.eye(SSM_GROUPS_PER_BLOCK, dtype=F32)
    bb = bb.reshape(2, SSM_BLOCKS, SSM_GROUPS_PER_BLOCK, SSM_STATE, SSM_GROUP)
    bbd = jnp.einsum("rkgph,gq->kghrqp", bb, eye).reshape(SSM_BLOCKS, SSM_BLOCK_IN, SSM_BLOCK_STATE)
    cc = jnp.stack([c_re.astype(F32), -c_im.astype(F32)])
    cc = cc.reshape(2, SSM_BLOCKS, SSM_GROUPS_PER_BLOCK, SSM_GROUP, SSM_STATE)
    cbd = jnp.einsum("rkghp,gq->krgpqh", cc, eye).reshape(SSM_BLOCKS, SSM_BLOCK_STATE, SSM_BLOCK_IN)
    tables = []
    for pows in powers:
        k = jnp.asarray(pows, F32)[:, None, None]
        mag_k = jnp.exp(k * (a_re * dt))
        pw = jnp.stack([mag_k * jnp.cos(k * (a_im * dt)), mag_k * jnp.sin(k * (a_im * dt))], axis=1)
        tables.append(pw.reshape(2 * len(pows), SSM_SLABS, LANES).transpose(1, 0, 2))
    return bbd.astype(BF16), cbd.astype(BF16), tables


def _state_to_slabs(h):
    b = h.shape[0]
    h = h.astype(F32).reshape(b, SSM_SLABS, LANES, 2).transpose(3, 1, 0, 2)
    return h[0], h[1]


def _slabs_to_state(re, im):
    b = re.shape[1]
    h = jnp.stack([re, im], axis=-1).transpose(1, 0, 2, 3)
    return h.reshape(b, N_SSM_GROUPS, SSM_STATE, 2)


def _s5(x, h0re, h0im, g, bbd, cbd, pw, d_skip, w_glu, *, tm, seg):
    nb, rows, _ = x.shape
    nseq = tm // seg
    assert nseq == 1 or rows == tm
    row = pl.BlockSpec((1, tm, D_MODEL), lambda b, i: (b, i, 0))
    st = pl.BlockSpec((1, SSM_SLABS, nseq, LANES), lambda b, i: (b, 0, 0, 0))
    slab = pltpu.VMEM((SSM_SLABS, SCAN_BASE + tm, LANES), F32)
    levels = []
    if nseq == 1:
        levels = [pltpu.VMEM((SCAN_UNROLL, SCAN_BASE + max(tm // SCAN_RADIX ** lvl, SUBLANES), LANES), F32)
                  for lvl in range(1, _scan_levels(tm))] * 2
    st_shape = jax.ShapeDtypeStruct(h0re.shape, F32)
    return pl.pallas_call(
        functools.partial(_s5_body, tm=tm, seg=seg),
        grid=(nb, rows // tm),
        in_specs=[row, st, st, _const_spec((1, D_MODEL)), _const_spec(bbd.shape), _const_spec(cbd.shape),
                  _const_spec(pw.shape), _const_spec((1, D_MODEL)), _const_spec(w_glu.shape)],
        out_specs=[row, st, st],
        out_shape=[jax.ShapeDtypeStruct(x.shape, F32), st_shape, st_shape],
        scratch_shapes=[slab, slab] + levels,
        compiler_params=_params(("parallel", "arbitrary")),
        name="s5_mixer",
    )(x, h0re, h0im, g.reshape(1, D_MODEL), bbd, cbd, pw, d_skip.reshape(1, D_MODEL), w_glu)


def _rope_tables(pos):
    half = ROT_DIM // 2
    inv = ROPE_THETA ** (-jnp.arange(0, ROT_DIM, 2, dtype=F32) / ROT_DIM)
    ang = pos.astype(F32)[:, None] * inv[None, :]
    cos, sin = jnp.cos(ang), jnp.sin(ang)
    n = pos.shape[0]
    pad = HEAD_DIM - ROT_DIM
    c = jnp.concatenate([cos, cos, jnp.ones((n, pad), F32)], axis=-1)
    s1 = jnp.concatenate([-sin, jnp.zeros((n, half + pad), F32)], axis=-1)
    s2 = jnp.concatenate([jnp.zeros((n, half), F32), sin, jnp.zeros((n, pad), F32)], axis=-1)
    rep = LANES // HEAD_DIM
    return tuple(jnp.tile(t, (1, rep)) for t in (c, s1, s2))


def _rope_tile(y, c, s1, s2):
    half = ROT_DIM // 2
    return y * c + pltpu.roll(y, LANES - half, 1) * s1 + pltpu.roll(y, half, 1) * s2


def _proj_body(x_ref, g_ref, w_ref, c_ref, s1_ref, s2_ref, *o_refs, rope_tiles):
    h = _rms(x_ref[...], g_ref[...]).astype(BF16)
    y = _dot(h, w_ref[...])
    c, s1, s2 = c_ref[...], s1_ref[...], s2_ref[...]
    tiles_per_out = y.shape[1] // LANES // len(o_refs)
    for j in range(y.shape[1] // LANES):
        yj = y[:, j * LANES:(j + 1) * LANES]
        if j < rope_tiles:
            yj = _rope_tile(yj, c, s1, s2)
        jo = j % tiles_per_out
        o_refs[j // tiles_per_out][:, jo * LANES:(jo + 1) * LANES] = yj


def _proj(x, g, w, tables, rope_width, n_out, tm=512):
    n = x.shape[0]
    tm = min(tm, n)
    width = w.shape[1] // n_out
    row = pl.BlockSpec((tm, D_MODEL), lambda i: (i, 0))
    tab = pl.BlockSpec((tm, LANES), lambda i: (i, 0))
    return pl.pallas_call(
        functools.partial(_proj_body, rope_tiles=rope_width // LANES),
        grid=(n // tm,),
        in_specs=[row, _const_spec((1, D_MODEL)), _const_spec(w.shape), tab, tab, tab],
        out_specs=[pl.BlockSpec((tm, width), lambda i: (i, 0))] * n_out,
        out_shape=[jax.ShapeDtypeStruct((n, width), F32)] * n_out,
        compiler_params=_params(("parallel",)),
        name="proj_rope",
    )(x, g.reshape(1, D_MODEL), w, *tables)


def _proj_heads_body(x_ref, g_ref, w_ref, c_ref, s1_ref, s2_ref, *refs, n_mat, tm, cache_from):
    n_t = n_mat if cache_from is not None else 0
    outs = refs[:N_BRANCHES * n_mat]
    t_refs = refs[N_BRANCHES * n_mat:N_BRANCHES * n_mat + n_t]
    slab = refs[-1]
    h = _rms(x_ref[0], g_ref[...]).astype(BF16)
    y = _dot(h, w_ref[...])
    c, s1, s2 = c_ref[...], s1_ref[...], s2_ref[...]
    for m in range(n_mat):
        tiles = [y[:, m * ATTN_WIDTH + j * LANES:m * ATTN_WIDTH + (j + 1) * LANES]
                 for j in range(ATTN_WIDTH // LANES)]
        if m == 0:
            tiles = [_rope_tile(t, c, s1, s2) for t in tiles]
        for j in range(BRANCH_TILES):
            outs[N_BRANCHES * m][0, 0, :, j * LANES:(j + 1) * LANES] = tiles[j].astype(BF16)
        for j in range(BRANCH_TILES, N_BRANCHES * BRANCH_TILES):
            slab[j - BRANCH_TILES] = tiles[j]
        for br in range(1, N_BRANCHES):
            d = DILATIONS[br]
            for r in range(d):
                for j in range(BRANCH_TILES):
                    part = slab[(br - 1) * BRANCH_TILES + j, pl.ds(r, tm // d, stride=d), :]
                    outs[N_BRANCHES * m + br][0, r, :, j * LANES:(j + 1) * LANES] = part.astype(BF16)
        if cache_from is not None:
            @pl.when(pl.program_id(1) >= cache_from)
            def _(m=m, tiles=tiles):
                t_refs[m][0] = jnp.concatenate(tiles, axis=-1).T


def _proj_heads(x, g, w, tables, n_mat, keep=None, tm=512):
    b, l, _ = x.shape
    assert w.shape[1] == n_mat * ATTN_WIDTH and tm % (DILATIONS[-1] * 2 * SUBLANES) == 0
    cache_from = None if keep is None else (l - keep) // tm
    row = pl.BlockSpec((1, tm, D_MODEL), lambda bi, i: (bi, i, 0))
    tab = pl.BlockSpec((tm, LANES), lambda bi, i: (i, 0))
    out_specs, out_shape = [], []
    for _ in range(n_mat):
        for d in DILATIONS:
            out_specs.append(pl.BlockSpec((1, d, tm // d, MERGED_WIDTH), lambda bi, i: (bi, 0, i, 0)))
            out_shape.append(jax.ShapeDtypeStruct((b, d, l // d, MERGED_WIDTH), BF16))
    if keep is not None:
        for _ in range(n_mat):
            out_specs.append(pl.BlockSpec((1, ATTN_WIDTH, tm),
                                          lambda bi, i: (bi, 0, jnp.maximum(i - cache_from, 0))))
            out_shape.append(jax.ShapeDtypeStruct((b, ATTN_WIDTH, keep), F32))
    return pl.pallas_call(
        functools.partial(_proj_heads_body, n_mat=n_mat, tm=tm, cache_from=cache_from),
        grid=(b, l // tm),
        in_specs=[row, _const_spec((1, D_MODEL)), _const_spec(w.shape), tab, tab, tab],
        out_specs=out_specs,
        out_shape=out_shape,
        scratch_shapes=[pltpu.VMEM(((N_BRANCHES - 1) * BRANCH_TILES, tm, LANES), F32)],
        compiler_params=_params(("parallel", "arbitrary")),
        name="proj_heads",
    )(x, g.reshape(1, D_MODEL), w, *tables)


def _swa_body(q_ref, kp_ref, kc_ref, vp_ref, vc_ref, o_ref, lse_ref, *, blocks_per_seq):
    n = pl.program_id(1)
    q = q_ref[0]
    k = jnp.concatenate([kp_ref[0], kc_ref[0]], axis=0)
    v = jnp.concatenate([vp_ref[0], vc_ref[0]], axis=0)
    qi = lax.broadcasted_iota(jnp.int32, (SWA_BLOCK, 2 * SWA_BLOCK), 0)
    kj = lax.broadcasted_iota(jnp.int32, (SWA_BLOCK, 2 * SWA_BLOCK), 1)
    has_prev = (n % blocks_per_seq) != 0
    valid = (kj >= qi) & (kj <= qi + SWA_BLOCK) & ((kj >= SWA_BLOCK) | has_prev)
    for h in range(HEADS_PER_BRANCH):
        hs = slice(h * HEAD_DIM, (h + 1) * HEAD_DIM)
        s = _dot_nt(q[:, hs], k[:, hs]) * (HEAD_DIM ** -0.5)
        s = jnp.where(valid, s, NEG_INF)
        m = jnp.max(s, axis=-1, keepdims=True)
        p = jnp.exp(s - m)
        l = jnp.sum(p, axis=-1, keepdims=True)
        o_ref[0, :, hs] = _dot(p.astype(BF16), v[:, hs]) / l
        lse_ref[0, :, hs] = jnp.broadcast_to(m + jnp.log(l), (SWA_BLOCK, HEAD_DIM))


def _swa(q, k, v, dilation):
    b, l, w = q.shape
    nblk = l // SWA_BLOCK
    cur = pl.BlockSpec((1, SWA_BLOCK, w), lambda bi, n: (bi, n, 0))
    prev = pl.BlockSpec((1, SWA_BLOCK, w), lambda bi, n: (bi, jnp.maximum(n - 1, 0), 0))
    out = jax.ShapeDtypeStruct((b, l, w), F32)
    return pl.pallas_call(
        functools.partial(_swa_body, blocks_per_seq=nblk // dilation),
        grid=(b, nblk),
        in_specs=[cur, prev, cur, prev, cur],
        out_specs=[cur, cur],
        out_shape=[out, out],
        compiler_params=_params(("parallel", "arbitrary")),
        name="swa",
    )(q, k, k, v, v)


def _merge_body(x_ref, *refs, tm):
    pairs, wo_ref, out_ref, slab = refs[:2 * N_BRANCHES], refs[2 * N_BRANCHES], refs[-2], refs[-1]
    for a, ref in enumerate(pairs[2:]):
        d = DILATIONS[1 + a // 2]
        for r in range(d):
            for j in range(BRANCH_TILES):
                slab[a * BRANCH_TILES + j, pl.ds(r, tm // d, stride=d), :] = ref[0, r, :, j * LANES:(j + 1) * LANES]
    tiles = []
    for j in range(BRANCH_TILES):
        cols = slice(j * LANES, (j + 1) * LANES)
        o0, l0 = pairs[0][0, 0, :, cols], pairs[1][0, 0, :, cols]
        o1, l1, o2, l2 = (slab[a * BRANCH_TILES + j] for a in range(4))
        m = jnp.maximum(jnp.maximum(l0, l1), l2)
        e0, e1, e2 = jnp.exp(l0 - m), jnp.exp(l1 - m), jnp.exp(l2 - m)
        tiles.append(((e0 * o0 + e1 * o1 + e2 * o2) / (e0 + e1 + e2)).astype(BF16))
    out_ref[0] = x_ref[0] + _dot(jnp.concatenate(tiles, axis=-1), wo_ref[...])


def _merge(x, os, ls, w_o, tm=512):
    b, l, _ = x.shape
    row = pl.BlockSpec((1, tm, D_MODEL), lambda bi, i: (bi, i, 0))
    specs, args = [], []
    for d, o, lse in zip(DILATIONS, os, ls):
        spec = pl.BlockSpec((1, d, tm // d, MERGED_WIDTH), lambda bi, i: (bi, 0, i, 0))
        specs += [spec, spec]
        args += [o, lse]
    return pl.pallas_call(
        functools.partial(_merge_body, tm=tm),
        grid=(b, l // tm),
        in_specs=[row] + specs + [_const_spec(w_o.shape)],
        out_specs=row,
        out_shape=jax.ShapeDtypeStruct(x.shape, F32),
        scratch_shapes=[pltpu.VMEM((2 * (N_BRANCHES - 1) * BRANCH_TILES, tm, LANES), F32)],
        compiler_params=_params(("parallel", "parallel")),
        name="merge_wo",
    )(x, *args, w_o)


def _decode_body(q_ref, kn_ref, vn_ref, knt_ref, vnt_ref, ck_ref, cv_ref, o_ref, cko_ref, cvo_ref,
                 *, past, steps):
    scale = HEAD_DIM ** -0.5
    qi = lax.broadcasted_iota(jnp.int32, (steps, steps), 0)
    kj = lax.broadcasted_iota(jnp.int32, (steps, steps), 1)
    scores, values = [], []
    for br, d in enumerate(DILATIONS):
        w = WINDOWS[br]
        q = q_ref[0, 0, br].astype(BF16)
        kt = ck_ref[0, br, 0, :, past - w:].astype(BF16)
        back = (w + lax.broadcasted_iota(jnp.int32, (steps, w), 0)
                - lax.broadcasted_iota(jnp.int32, (steps, w), 1))
        ok = ((back & (d - 1)) == 0) & (back <= w)
        scores.append(jnp.where(ok, _dot(q, kt) * scale, NEG_INF))
        values.append((cv_ref[0, br, 0, :, past - w:].astype(BF16), True))
        ok = (kj <= qi) & (((qi - kj) & (d - 1)) == 0)
        scores.append(jnp.where(ok, _dot_nt(q, kn_ref[0, 0, br].astype(BF16)) * scale, NEG_INF))
        values.append((vn_ref[0, 0, br].astype(BF16), False))
    m = functools.reduce(jnp.maximum, [jnp.max(s, axis=-1, keepdims=True) for s in scores])
    num = jnp.zeros((steps, HEAD_DIM), F32)
    den = jnp.zeros((steps, 1), F32)
    for s, (v, transposed) in zip(scores, values):
        p = jnp.exp(s - m)
        den = den + jnp.sum(p, axis=-1, keepdims=True)
        num = num + (_dot_nt(p.astype(BF16), v) if transposed else _dot(p.astype(BF16), v))
    o_ref[0, 0] = num / den

    lane = lax.broadcasted_iota(jnp.int32, (HEAD_DIM, LANES), 1)
    tiles = past // LANES
    for src, new, dst in ((ck_ref, knt_ref, cko_ref), (cv_ref, vnt_ref, cvo_ref)):
        for br in range(N_BRANCHES):
            cur = pltpu.roll(src[0, br, 0, :, 0:LANES], LANES - steps, 1)
            for t in range(tiles):
                if t + 1 < tiles:
                    nxt = pltpu.roll(src[0, br, 0, :, (t + 1) * LANES:(t + 2) * LANES], LANES - steps, 1)
                else:
                    nxt = jnp.concatenate([jnp.zeros((HEAD_DIM, LANES - steps), F32), new[0, br, 0]], axis=1)
                dst[0, br, 0, :, t * LANES:(t + 1) * LANES] = jnp.where(lane < LANES - steps, cur, nxt)
                cur = nxt


def _decode(q, kn, vn, cache_k, cache_v):
    b, steps = q.shape[0], q.shape[1]
    past = cache_k.shape[1]
    assert past == MAX_WINDOW and steps <= LANES
    hb = HEADS_PER_BRANCH

    def slots(t):
        return t.reshape(b, steps, N_BRANCHES, hb, HEAD_DIM).transpose(0, 3, 2, 1, 4)

    def slots_t(t):
        return t.reshape(b, steps, N_BRANCHES, hb, HEAD_DIM).transpose(0, 2, 3, 4, 1)

    def cache_t(c):
        return c.transpose(0, 2, 3, 1).reshape(b, N_BRANCHES, hb, HEAD_DIM, past)

    new = pl.BlockSpec((1, 1, N_BRANCHES, steps, HEAD_DIM), lambda i, h: (i, h, 0, 0, 0))
    new_t = pl.BlockSpec((1, N_BRANCHES, 1, HEAD_DIM, steps), lambda i, h: (i, 0, h, 0, 0))
    cache = pl.BlockSpec((1, N_BRANCHES, 1, HEAD_DIM, past), lambda i, h: (i, 0, h, 0, 0))
    cache_shape = jax.ShapeDtypeStruct((b, N_BRANCHES, hb, HEAD_DIM, past), F32)
    o, ck, cv = pl.pallas_call(
        functools.partial(_decode_body, past=past, steps=steps),
        grid=(b, hb),
        in_specs=[new, new, new, new_t, new_t, cache, cache],
        out_specs=[pl.BlockSpec((1, 1, steps, HEAD_DIM), lambda i, h: (i, h, 0, 0)), cache, cache],
        out_shape=[jax.ShapeDtypeStruct((b, hb, steps, HEAD_DIM), F32), cache_shape, cache_shape],
        compiler_params=_params(("parallel", "parallel")),
        name="decode_attn_roll",
    )(slots(q), slots(kn), slots(vn), slots_t(kn), slots_t(vn), cache_t(cache_k), cache_t(cache_v))
    o = o.transpose(0, 2, 1, 3).reshape(b * steps, MERGED_WIDTH)
    ck, cv = (c.reshape(b, N_HEADS, HEAD_DIM, past).transpose(0, 3, 1, 2) for c in (ck, cv))
    return o, ck, cv


def _addproj_body(x_ref, o_ref, wo_ref, out_ref):
    out_ref[...] = x_ref[...] + _dot(o_ref[...].astype(BF16), wo_ref[...])


def _addproj(x, o, w_o):
    n = x.shape[0]
    return pl.pallas_call(
        _addproj_body,
        grid=(1,),
        in_specs=[_const_spec(x.shape), _const_spec(o.shape), _const_spec(w_o.shape)],
        out_specs=pl.BlockSpec(x.shape, lambda i: (0, 0)),
        out_shape=jax.ShapeDtypeStruct((n, D_MODEL), F32),
        compiler_params=_params(("arbitrary",)),
        name="add_wo",
    )(x, o, w_o)


def kernel(x_prompt, x_sample, state_ssm, cache_k, cache_v, ffn_norm1, ffn_w1_in, ffn_w1_out, ffn_norm2,
           ffn_w2_in, ffn_w2_out, mix_norm, ssm_a_re, ssm_a_im, ssm_log_dt, ssm_b_re, ssm_b_im, ssm_c_re,
           ssm_c_im, ssm_d, ssm_w_glu, kv_norm, w_kv, w_q, w_o, final_norm):
    bp, sp, _ = x_prompt.shape
    bs, ss, _ = x_sample.shape
    tm_scan = 256

    w1_in, w1_out = ffn_w1_in.astype(BF16), ffn_w1_out.astype(BF16)
    w2_in, w2_out = ffn_w2_in.astype(BF16), ffn_w2_out.astype(BF16)
    w_glu, w_kv_b, w_q_b, w_o_b = (ssm_w_glu[0].astype(BF16), w_kv.astype(BF16), w_q[0].astype(BF16),
                                   w_o[0].astype(BF16))
    bbd, cbd, (pw_p, pw_s) = _s5_weights(ssm_a_re[0], ssm_a_im[0], ssm_log_dt[0], ssm_b_re[0], ssm_b_im[0],
                                         ssm_c_re[0], ssm_c_im[0], (_scan_powers(tm_scan), [1]))

    def layer_a(x, h0, pw, tm, seg):
        nb, rows, _ = x.shape
        x = _ffn(x.reshape(nb * rows, D_MODEL), ffn_norm1[0], w1_in[0], w1_out[0])
        x, hl_re, hl_im = _s5(x.reshape(nb, rows, D_MODEL), *h0, mix_norm[0], bbd, cbd, pw, ssm_d[0], w_glu,
                              tm=tm, seg=seg)
        x = _ffn(x.reshape(nb * rows, D_MODEL), ffn_norm2[0], w2_in[0], w2_out[0])
        return x, hl_re, hl_im

    tab_p = _rope_tables(jnp.arange(sp, dtype=jnp.int32))
    zeros = jnp.zeros((bp, SSM_SLABS, 1, LANES), F32)
    xp, hl_re, hl_im = layer_a(x_prompt, (zeros, zeros), pw_p, tm_scan, tm_scan)
    state_prompt = _slabs_to_state(hl_re[:, :, 0].transpose(1, 0, 2), hl_im[:, :, 0].transpose(1, 0, 2))[None]
    keep_p = min(MAX_WINDOW, sp)
    *kv_d, kt_p, vt_p = _proj_heads(xp.reshape(bp, sp, D_MODEL), kv_norm, w_kv_b, tab_p, 2, keep=keep_p)
    cache_k_prompt = kt_p.reshape(bp, N_HEADS, HEAD_DIM, keep_p).transpose(0, 3, 1, 2)
    cache_v_prompt = vt_p.reshape(bp, N_HEADS, HEAD_DIM, keep_p).transpose(0, 3, 1, 2)
    xp = _ffn(xp, ffn_norm1[1], w1_in[1], w1_out[1]).reshape(bp, sp, D_MODEL)
    q_d = _proj_heads(xp, mix_norm[1], w_q_b, tab_p, 1)
    outs, lses = [], []
    for br, d in enumerate(DILATIONS):
        flat = lambda t: t.reshape(bp, sp, MERGED_WIDTH)
        o, lse = _swa(flat(q_d[br]), flat(kv_d[br]), flat(kv_d[N_BRANCHES + br]), d)
        outs.append(o.reshape(bp, d, sp // d, MERGED_WIDTH))
        lses.append(lse.reshape(bp, d, sp // d, MERGED_WIDTH))
    xp = _merge(xp, outs, lses, w_o_b).reshape(bp * sp, D_MODEL)
    y_prompt = _ffn(xp, ffn_norm2[1], w2_in[1], w2_out[1], final_g=final_norm).reshape(bp, sp, D_MODEL)

    pos_s = jnp.tile(PAST_LEN + jnp.arange(ss, dtype=jnp.int32), bs)
    tab_s = _rope_tables(pos_s)
    h0_s = tuple(t[None] for t in _state_to_slabs(state_ssm[0]))
    xs, hl_re, hl_im = layer_a(x_sample.reshape(1, bs * ss, D_MODEL), h0_s, pw_s, bs * ss, ss)
    state_sample = _slabs_to_state(hl_re[0], hl_im[0])[None]
    k_s, v_s = (t.reshape(bs, ss, N_HEADS, HEAD_DIM) for t in _proj(xs, kv_norm, w_kv_b, tab_s, ATTN_WIDTH, 2))
    xs = _ffn(xs, ffn_norm1[1], w1_in[1], w1_out[1])
    q_s = _proj(xs, mix_norm[1], w_q_b, tab_s, ATTN_WIDTH, 1)[0].reshape(bs, ss, N_HEADS, HEAD_DIM)
    o_s, cache_k_sample, cache_v_sample = _decode(q_s, k_s, v_s, cache_k, cache_v)
    xs = _addproj(xs, o_s, w_o_b)
    y_sample = _ffn(xs, ffn_norm2[1], w2_in[1], w2_out[1], final_g=final_norm).reshape(bs, ss, D_MODEL)

    return (y_prompt, y_sample, state_prompt, cache_k_prompt, cache_v_prompt,
            state_sample, cache_k_sample, cache_v_sample)
```

```python
import functools

import jax
import jax.numpy as jnp
from jax import lax
from jax.experimental import pallas as pl
from jax.experimental.pallas import tpu as pltpu

F32 = jnp.float32
BF16 = jnp.bfloat16

D_MODEL = 1024
D_FF = 2816
SSM_GROUP = 16
N_SSM_GROUPS = D_MODEL // SSM_GROUP
SSM_STATE = 64
HEAD_DIM = 64
WINDOWS = (128, 512, 2048)
DILATIONS = (1, 4, 16)
N_BRANCHES = 3
HEADS_PER_BRANCH = 8
N_HEADS = N_BRANCHES * HEADS_PER_BRANCH
ATTN_WIDTH = N_HEADS * HEAD_DIM
MERGED_WIDTH = HEADS_PER_BRANCH * HEAD_DIM
MAX_WINDOW = max(WINDOWS)
ROT_DIM = HEAD_DIM // 4
ROPE_THETA = 500000.0
PAST_LEN = 16384
RMS_EPS = 1e-6
NEG_INF = -1e30

LANES = 128
SUBLANES = 8
BRANCH_TILES = MERGED_WIDTH // LANES

SSM_BLOCKS = 4
SSM_GROUPS_PER_BLOCK = N_SSM_GROUPS // SSM_BLOCKS
SSM_BLOCK_IN = SSM_GROUPS_PER_BLOCK * SSM_GROUP
SSM_BLOCK_HALF = SSM_GROUPS_PER_BLOCK * SSM_STATE
SSM_BLOCK_STATE = 2 * SSM_BLOCK_HALF
SSM_SLABS = N_SSM_GROUPS * SSM_STATE // LANES
SSM_SLABS_PER_BLOCK = SSM_SLABS // SSM_BLOCKS
SCAN_RADIX = 4
SCAN_BASE = SUBLANES
SCAN_UNROLL = 4

FF_CHUNK = 256
SWA_BLOCK = 128
SWA_SUB = 4

VMEM_LIMIT = 56 * 1024 * 1024


def _params(semantics):
    return pltpu.CompilerParams(dimension_semantics=semantics, vmem_limit_bytes=VMEM_LIMIT)


def _const_spec(shape):
    nd = len(shape)
    return pl.BlockSpec(shape, lambda *_: (0,) * nd, pipeline_mode=pl.Buffered(1))


def _rms(x, g):
    return x * lax.rsqrt(jnp.mean(x * x, axis=-1, keepdims=True) + RMS_EPS) * g


def _dot(a, b):
    return jnp.dot(a, b, preferred_element_type=F32)


def _dot_nt(a, b):
    return lax.dot_general(a, b, (((1,), (1,)), ((), ())), preferred_element_type=F32)


def _ffn_body(x_ref, g_ref, win_ref, wout_ref, *rest, final):
    if final:
        fg_ref, o_ref = rest
    else:
        (o_ref,) = rest
    x = x_ref[...]
    h = _rms(x, g_ref[...]).astype(BF16)
    acc = x
    for j in range(D_FF // FF_CHUNK):
        lo = j * FF_CHUNK
        gate = _dot(h, win_ref[:, lo:lo + FF_CHUNK])
        up = _dot(h, win_ref[:, D_FF + lo:D_FF + lo + FF_CHUNK])
        act = gate * jax.nn.sigmoid(gate) * up
        acc = acc + _dot((0.5 * act).astype(BF16), wout_ref[lo:lo + FF_CHUNK, :])
    if final:
        acc = _rms(acc, fg_ref[...])
    o_ref[...] = acc


def _ffn(x, g, w_in, w_out, final_g=None, tm=1024):
    n = x.shape[0]
    tm = min(tm, n)
    row = pl.BlockSpec((tm, D_MODEL), lambda i: (i, 0))
    in_specs = [row, _const_spec((1, D_MODEL)), _const_spec(w_in.shape), _const_spec(w_out.shape)]
    args = [x, g.reshape(1, D_MODEL), w_in, w_out]
    if final_g is not None:
        in_specs.append(_const_spec((1, D_MODEL)))
        args.append(final_g.reshape(1, D_MODEL))
    return pl.pallas_call(
        functools.partial(_ffn_body, final=final_g is not None),
        grid=(n // tm,),
        in_specs=in_specs,
        out_specs=row,
        out_shape=jax.ShapeDtypeStruct((n, D_MODEL), F32),
        compiler_params=_params(("parallel",)),
        name="ffn",
    )(*args)


def _scan_levels(tm):
    levels, n = 0, tm
    while n > 1:
        assert n % SCAN_RADIX == 0
        n //= SCAN_RADIX
        levels += 1
    return levels


def _scan_powers(tm):
    pows = [SCAN_RADIX ** lvl * m for lvl in range(_scan_levels(tm)) for m in range(1, SCAN_RADIX)]
    return pows + [tm]


def _cmul(ar, ai, br, bi):
    return ar * br - ai * bi, ar * bi + ai * br


def _blocked_scan(hre, him, pw_ref, lv_re, lv_im, slab, slot, tm):
    levels = _scan_levels(tm)
    top = SCAN_BASE - 1

    def views(lvl):
        if lvl == 0:
            return (lambda sl: hre[slab, sl, :], lambda sl: him[slab, sl, :],
                    lambda sl, v: hre.__setitem__((slab, sl, slice(None)), v),
                    lambda sl, v: him.__setitem__((slab, sl, slice(None)), v))
        a, b = lv_re[lvl - 1], lv_im[lvl - 1]
        return (lambda sl: a[slot, sl, :], lambda sl: b[slot, sl, :],
                lambda sl, v: a.__setitem__((slot, sl, slice(None)), v),
                lambda sl, v: b.__setitem__((slot, sl, slice(None)), v))

    def power(k):
        return pw_ref[slab, pl.ds(2 * k, 1), :], pw_ref[slab, pl.ds(2 * k + 1, 1), :]

    def elems(n, tau):
        return pl.ds(SCAN_BASE + tau, n, stride=SCAN_RADIX)

    s_re, s_im = hre[slab, pl.ds(top, 1), :], him[slab, pl.ds(top, 1), :]
    for lvl in range(levels):
        get_re, get_im, put_re, put_im = views(lvl)
        n = tm // SCAN_RADIX ** (lvl + 1)
        lr, li = power((SCAN_RADIX - 1) * lvl)
        pr, pi = get_re(elems(n, 0)), get_im(elems(n, 0))
        for tau in range(1, SCAN_RADIX):
            mr, mi = _cmul(lr, li, pr, pi)
            pr, pi = mr + get_re(elems(n, tau)), mi + get_im(elems(n, tau))
            put_re(elems(n, tau), pr)
            put_im(elems(n, tau), pi)
        if lvl + 1 < levels:
            _, _, nput_re, nput_im = views(lvl + 1)
            nput_re(pl.ds(SCAN_BASE, n), pr)
            nput_im(pl.ds(SCAN_BASE, n), pi)
            nput_re(pl.ds(top, 1), s_re)
            nput_im(pl.ds(top, 1), s_im)
    for lvl in reversed(range(levels)):
        get_re, get_im, put_re, put_im = views(lvl)
        n = tm // SCAN_RADIX ** (lvl + 1)
        last = lvl + 1 == levels
        if not last:
            nget_re, nget_im, _, _ = views(lvl + 1)
            put_re(elems(n, SCAN_RADIX - 1), nget_re(pl.ds(SCAN_BASE, n)))
            put_im(elems(n, SCAN_RADIX - 1), nget_im(pl.ds(SCAN_BASE, n)))
        before = pl.ds(top, n, stride=SCAN_RADIX)
        tr, ti = get_re(before), get_im(before)
        for tau in range(SCAN_RADIX if last else SCAN_RADIX - 1):
            k = (SCAN_RADIX - 1) * lvl + tau if tau < SCAN_RADIX - 1 else (SCAN_RADIX - 1) * levels
            mr, mi = _cmul(*power(k), tr, ti)
            put_re(elems(n, tau), get_re(elems(n, tau)) + mr)
            put_im(elems(n, tau), get_im(elems(n, tau)) + mi)
    hre[slab, pl.ds(top, 1), :] = hre[slab, pl.ds(top + tm, 1), :]
    him[slab, pl.ds(top, 1), :] = him[slab, pl.ds(top + tm, 1), :]


def _s5_body(x_ref, h0re_ref, h0im_ref, g_ref, bbd_ref, cbd_ref, pw_ref, d_ref, wglu_ref,
             xo_ref, hlre_ref, hlim_ref, hre, him, *levels_scr, tm, seg):
    nseq = tm // seg
    rows = pl.ds(SCAN_BASE, tm)
    top = SCAN_BASE - 1
    n_lv = len(levels_scr) // 2
    lv_re, lv_im = levels_scr[:n_lv], levels_scr[n_lv:]

    if nseq == 1:
        @pl.when(pl.program_id(1) == 0)
        def _():
            hre[:, pl.ds(top, 1), :] = h0re_ref[0]
            him[:, pl.ds(top, 1), :] = h0im_ref[0]

    x = x_ref[0]
    u = _rms(x, g_ref[...])
    ub = u.astype(BF16)

    def project_in(k):
        bu = _dot(ub[:, k * SSM_BLOCK_IN:(k + 1) * SSM_BLOCK_IN], bbd_ref[k])
        for j in range(SSM_SLABS_PER_BLOCK):
            hre[k * SSM_SLABS_PER_BLOCK + j, rows, :] = bu[:, j * LANES:(j + 1) * LANES]
            him[k * SSM_SLABS_PER_BLOCK + j, rows, :] = bu[:, SSM_BLOCK_HALF + j * LANES:
                                                          SSM_BLOCK_HALF + (j + 1) * LANES]

    def project_out(k):
        sl = range(k * SSM_SLABS_PER_BLOCK, (k + 1) * SSM_SLABS_PER_BLOCK)
        hk = jnp.concatenate([hre[p, rows, :] for p in sl] + [him[p, rows, :] for p in sl], axis=-1)
        return _dot(hk.astype(BF16), cbd_ref[k])

    if nseq == 1:
        ys = []
        project_in(0)
        for k in range(SSM_BLOCKS):
            if k + 1 < SSM_BLOCKS:
                project_in(k + 1)
            for j in range(SSM_SLABS_PER_BLOCK):
                _blocked_scan(hre, him, pw_ref, lv_re, lv_im, k * SSM_SLABS_PER_BLOCK + j, j % SCAN_UNROLL, tm)
            ys.append(project_out(k))
        hlre_ref[0] = hre[:, pl.ds(top, 1), :]
        hlim_ref[0] = him[:, pl.ds(top, 1), :]
    else:
        for k in range(SSM_BLOCKS):
            project_in(k)

        def slab_body(p, carry):
            lr, li = pw_ref[p, pl.ds(0, 1), :], pw_ref[p, pl.ds(1, 1), :]
            hr, hi = h0re_ref[0, p], h0im_ref[0, p]
            for t in range(seg):
                step = pl.ds(SCAN_BASE + t, nseq, stride=seg)
                mr, mi = _cmul(lr, li, hr, hi)
                hr, hi = mr + hre[p, step, :], mi + him[p, step, :]
                hre[p, step, :] = hr
                him[p, step, :] = hi
            hlre_ref[0, p] = hr
            hlim_ref[0, p] = hi
            return carry

        lax.fori_loop(0, SSM_SLABS, slab_body, 0)
        ys = [project_out(k) for k in range(SSM_BLOCKS)]

    y = jnp.concatenate(ys, axis=-1) + d_ref[...] * u
    z = jax.nn.gelu(y, approximate=True).astype(BF16)
    zz = _dot(z, wglu_ref[...])
    xo_ref[0] = x + zz[:, :D_MODEL] * jax.nn.sigmoid(zz[:, D_MODEL:])


def _s5_weights(a_re, a_im, log_dt, b_re, b_im, c_re, c_im, powers):
    a_re, a_im = a_re.astype(F32), a_im.astype(F32)
    dt = jnp.exp(log_dt.astype(F32))[:, None]
    mag = jnp.exp(a_re * dt)
    lre, lim = mag * jnp.cos(a_im * dt), mag * jnp.sin(a_im * dt)
    den = a_re * a_re + a_im * a_im
    fre = ((lre - 1.0) * a_re + lim * a_im) / den
    fim = (lim * a_re - (lre - 1.0) * a_im) / den
    b_re, b_im = b_re.astype(F32), b_im.astype(F32)
    bb = jnp.stack([fre[:, :, None] * b_re - fim[:, :, None] * b_im,
                    fre[:, :, None] * b_im + fim[:, :, None] * b_re])
    eye = jnp.eye(SSM_GROUPS_PER_BLOCK, dtype=F32)
    bb = bb.reshape(2, SSM_BLOCKS, SSM_GROUPS_PER_BLOCK, SSM_STATE, SSM_GROUP)
    bbd = jnp.einsum("rkgph,gq->kghrqp", bb, eye).reshape(SSM_BLOCKS, SSM_BLOCK_IN, SSM_BLOCK_STATE)
    cc = jnp.stack([c_re.astype(F32), -c_im.astype(F32)])
    cc = cc.reshape(2, SSM_BLOCKS, SSM_GROUPS_PER_BLOCK, SSM_GROUP, SSM_STATE)
    cbd = jnp.einsum("rkghp,gq->krgpqh", cc, eye).reshape(SSM_BLOCKS, SSM_BLOCK_STATE, SSM_BLOCK_IN)
    tables = []
    for pows in powers:
        k = jnp.asarray(pows, F32)[:, None, None]
        mag_k = jnp.exp(k * (a_re * dt))
        pw = jnp.stack([mag_k * jnp.cos(k * (a_im * dt)), mag_k * jnp.sin(k * (a_im * dt))], axis=1)
        tables.append(pw.reshape(2 * len(pows), SSM_SLABS, LANES).transpose(1, 0, 2))
    return bbd.astype(BF16), cbd.astype(BF16), tables


def _state_to_slabs(h):
    b = h.shape[0]
    h = h.astype(F32).reshape(b, SSM_SLABS, LANES, 2).transpose(3, 1, 0, 2)
    return h[0], h[1]


def _slabs_to_state(re, im):
    b = re.shape[1]
    h = jnp.stack([re, im], axis=-1).transpose(1, 0, 2, 3)
    return h.reshape(b, N_SSM_GROUPS, SSM_STATE, 2)


def _s5(x, h0re, h0im, g, bbd, cbd, pw, d_skip, w_glu, *, tm, seg):
    nb, rows, _ = x.shape
    nseq = tm // seg
    assert nseq == 1 or rows == tm
    row = pl.BlockSpec((1, tm, D_MODEL), lambda b, i: (b, i, 0))
    st = pl.BlockSpec((1, SSM_SLABS, nseq, LANES), lambda b, i: (b, 0, 0, 0))
    slab = pltpu.VMEM((SSM_SLABS, SCAN_BASE + tm, LANES), F32)
    levels = []
    if nseq == 1:
        levels = [pltpu.VMEM((SCAN_UNROLL, SCAN_BASE + max(tm // SCAN_RADIX ** lvl, SUBLANES), LANES), F32)
                  for lvl in range(1, _scan_levels(tm))] * 2
    st_shape = jax.ShapeDtypeStruct(h0re.shape, F32)
    return pl.pallas_call(
        functools.partial(_s5_body, tm=tm, seg=seg),
        grid=(nb, rows // tm),
        in_specs=[row, st, st, _const_spec((1, D_MODEL)), _const_spec(bbd.shape), _const_spec(cbd.shape),
                  _const_spec(pw.shape), _const_spec((1, D_MODEL)), _const_spec(w_glu.shape)],
        out_specs=[row, st, st],
        out_shape=[jax.ShapeDtypeStruct(x.shape, F32), st_shape, st_shape],
        scratch_shapes=[slab, slab] + levels,
        compiler_params=_params(("parallel", "arbitrary")),
        name="s5_mixer",
    )(x, h0re, h0im, g.reshape(1, D_MODEL), bbd, cbd, pw, d_skip.reshape(1, D_MODEL), w_glu)


def _rope_tables(pos):
    half = ROT_DIM // 2
    inv = ROPE_THETA ** (-jnp.arange(0, ROT_DIM, 2, dtype=F32) / ROT_DIM)
    ang = pos.astype(F32)[:, None] * inv[None, :]
    cos, sin = jnp.cos(ang), jnp.sin(ang)
    n = pos.shape[0]
    pad = HEAD_DIM - ROT_DIM
    c = jnp.concatenate([cos, cos, jnp.ones((n, pad), F32)], axis=-1)
    s1 = jnp.concatenate([-sin, jnp.zeros((n, half + pad), F32)], axis=-1)
    s2 = jnp.concatenate([jnp.zeros((n, half), F32), sin, jnp.zeros((n, pad), F32)], axis=-1)
    rep = LANES // HEAD_DIM
    return tuple(jnp.tile(t, (1, rep)) for t in (c, s1, s2))


def _rope_tile(y, c, s1, s2):
    half = ROT_DIM // 2
    return y * c + pltpu.roll(y, LANES - half, 1) * s1 + pltpu.roll(y, half, 1) * s2


def _proj_body(x_ref, g_ref, w_ref, c_ref, s1_ref, s2_ref, *o_refs, rope_tiles):
    h = _rms(x_ref[...], g_ref[...]).astype(BF16)
    y = _dot(h, w_ref[...])
    c, s1, s2 = c_ref[...], s1_ref[...], s2_ref[...]
    tiles_per_out = y.shape[1] // LANES // len(o_refs)
    for j in range(y.shape[1] // LANES):
        yj = y[:, j * LANES:(j + 1) * LANES]
        if j < rope_tiles:
            yj = _rope_tile(yj, c, s1, s2)
        jo = j % tiles_per_out
        o_refs[j // tiles_per_out][:, jo * LANES:(jo + 1) * LANES] = yj


def _proj(x, g, w, tables, rope_width, n_out, tm=512):
    n = x.shape[0]
    tm = min(tm, n)
    width = w.shape[1] // n_out
    row = pl.BlockSpec((tm, D_MODEL), lambda i: (i, 0))
    tab = pl.BlockSpec((tm, LANES), lambda i: (i, 0))
    return pl.pallas_call(
        functools.partial(_proj_body, rope_tiles=rope_width // LANES),
        grid=(n // tm,),
        in_specs=[row, _const_spec((1, D_MODEL)), _const_spec(w.shape), tab, tab, tab],
        out_specs=[pl.BlockSpec((tm, width), lambda i: (i, 0))] * n_out,
        out_shape=[jax.ShapeDtypeStruct((n, width), F32)] * n_out,
        compiler_params=_params(("parallel",)),
        name="proj_rope",
    )(x, g.reshape(1, D_MODEL), w, *tables)


def _proj_heads_body(x_ref, g_ref, w_ref, c_ref, s1_ref, s2_ref, *refs, n_mat, tm, cache_from):
    n_t = n_mat if cache_from is not None else 0
    outs = refs[:N_BRANCHES * n_mat]
    t_refs = refs[N_BRANCHES * n_mat:N_BRANCHES * n_mat + n_t]
    slab = refs[-1]
    h = _rms(x_ref[0], g_ref[...]).astype(BF16)
    c, s1, s2 = c_ref[...], s1_ref[...], s2_ref[...]
    for m in range(n_mat):
        tiles = []
        for br, d in enumerate(DILATIONS):
            lo = m * ATTN_WIDTH + br * MERGED_WIDTH
            y = _dot(h, w_ref[:, lo:lo + MERGED_WIDTH])
            part = [y[:, j * LANES:(j + 1) * LANES] for j in range(BRANCH_TILES)]
            if m == 0:
                part = [_rope_tile(t, c, s1, s2) for t in part]
            tiles += part
            out = outs[N_BRANCHES * m + br]
            if d == 1:
                for j in range(BRANCH_TILES):
                    out[0, 0, :, j * LANES:(j + 1) * LANES] = part[j].astype(BF16)
                continue
            for j in range(BRANCH_TILES):
                slab[(br - 1) * BRANCH_TILES + j] = part[j]
            for r in range(d):
                for j in range(BRANCH_TILES):
                    rows = slab[(br - 1) * BRANCH_TILES + j, pl.ds(r, tm // d, stride=d), :]
                    out[0, r, :, j * LANES:(j + 1) * LANES] = rows.astype(BF16)
        if cache_from is not None:
            @pl.when(pl.program_id(1) >= cache_from)
            def _(m=m, tiles=tiles):
                t_refs[m][0] = jnp.concatenate(tiles, axis=-1).T


def _proj_heads(x, g, w, tables, n_mat, keep=None, tm=512):
    b, l, _ = x.shape
    assert w.shape[1] == n_mat * ATTN_WIDTH and tm % (DILATIONS[-1] * 2 * SUBLANES) == 0
    cache_from = None if keep is None else (l - keep) // tm
    row = pl.BlockSpec((1, tm, D_MODEL), lambda bi, i: (bi, i, 0))
    tab = pl.BlockSpec((tm, LANES), lambda bi, i: (i, 0))
    out_specs, out_shape = [], []
    for _ in range(n_mat):
        for d in DILATIONS:
            out_specs.append(pl.BlockSpec((1, d, tm // d, MERGED_WIDTH), lambda bi, i: (bi, 0, i, 0)))
            out_shape.append(jax.ShapeDtypeStruct((b, d, l // d, MERGED_WIDTH), BF16))
    if keep is not None:
        for _ in range(n_mat):
            out_specs.append(pl.BlockSpec((1, ATTN_WIDTH, tm),
                                          lambda bi, i: (bi, 0, jnp.maximum(i - cache_from, 0))))
            out_shape.append(jax.ShapeDtypeStruct((b, ATTN_WIDTH, keep), F32))
    return pl.pallas_call(
        functools.partial(_proj_heads_body, n_mat=n_mat, tm=tm, cache_from=cache_from),
        grid=(b, l // tm),
        in_specs=[row, _const_spec((1, D_MODEL)), _const_spec(w.shape), tab, tab, tab],
        out_specs=out_specs,
        out_shape=out_shape,
        scratch_shapes=[pltpu.VMEM(((N_BRANCHES - 1) * BRANCH_TILES, tm, LANES), F32)],
        compiler_params=_params(("parallel", "arbitrary")),
        name="proj_heads",
    )(x, g.reshape(1, D_MODEL), w, *tables)


def _swa_body(q_ref, kp_ref, kc_ref, vp_ref, vc_ref, o_ref, lse_ref, *, steps_per_seq):
    n = pl.program_id(1)
    k = jnp.concatenate([kp_ref[0], kc_ref[0]], axis=0)
    v = jnp.concatenate([vp_ref[0], vc_ref[0]], axis=0)
    qi = lax.broadcasted_iota(jnp.int32, (SWA_BLOCK, 2 * SWA_BLOCK), 0)
    kj = lax.broadcasted_iota(jnp.int32, (SWA_BLOCK, 2 * SWA_BLOCK), 1)
    band = (kj >= qi) & (kj <= qi + SWA_BLOCK)
    has_prev = (n % steps_per_seq) != 0
    first = band & ((kj >= SWA_BLOCK) | has_prev)
    low = lax.broadcasted_iota(jnp.int32, (SWA_BLOCK, LANES), 1) < HEAD_DIM
    zero = jnp.zeros((SWA_BLOCK, LANES), BF16)
    for j in range(SWA_SUB):
        rows = slice(j * SWA_BLOCK, (j + 1) * SWA_BLOCK)
        keys = slice(j * SWA_BLOCK, (j + 2) * SWA_BLOCK)
        valid = first if j == 0 else band
        for t in range(BRANCH_TILES):
            cols = slice(t * LANES, (t + 1) * LANES)
            q = q_ref[0, rows, cols] * (HEAD_DIM ** -0.5)
            kt, vt = k[keys, cols], v[keys, cols]
            outs, lses = [], []
            for qh in (jnp.where(low, q, zero), jnp.where(low, zero, q)):
                s = jnp.where(valid, _dot_nt(qh, kt), NEG_INF)
                m = jnp.max(s, axis=-1, keepdims=True)
                p = jnp.exp(s - m)
                l = jnp.sum(p, axis=-1, keepdims=True)
                outs.append(_dot(p.astype(BF16), vt) / l)
                lses.append(jnp.broadcast_to(m + jnp.log(l), (SWA_BLOCK, LANES)))
            o_ref[0, rows, cols] = jnp.where(low, outs[0], outs[1])
            lse_ref[0, rows, cols] = jnp.where(low, lses[0], lses[1])


def _swa(q, k, v, dilation):
    b, l, w = q.shape
    step = SWA_SUB * SWA_BLOCK
    assert (l // dilation) % step == 0
    cur = pl.BlockSpec((1, step, w), lambda bi, n: (bi, n, 0))
    prev = pl.BlockSpec((1, SWA_BLOCK, w), lambda bi, n: (bi, jnp.maximum(SWA_SUB * n - 1, 0), 0))
    out = jax.ShapeDtypeStruct((b, l, w), F32)
    return pl.pallas_call(
        functools.partial(_swa_body, steps_per_seq=l // dilation // step),
        grid=(b, l // step),
        in_specs=[cur, prev, cur, prev, cur],
        out_specs=[cur, cur],
        out_shape=[out, out],
        compiler_params=_params(("parallel", "arbitrary")),
        name="swa",
    )(q, k, k, v, v)


def _merge_body(x_ref, *refs, tm):
    pairs, wo_ref, out_ref, slab = refs[:2 * N_BRANCHES], refs[2 * N_BRANCHES], refs[-2], refs[-1]
    for a, ref in enumerate(pairs[2:]):
        d = DILATIONS[1 + a // 2]
        for r in range(d):
            for j in range(BRANCH_TILES):
                slab[a * BRANCH_TILES + j, pl.ds(r, tm // d, stride=d), :] = ref[0, r, :, j * LANES:(j + 1) * LANES]
    tiles = []
    for j in range(BRANCH_TILES):
        cols = slice(j * LANES, (j + 1) * LANES)
        o0, l0 = pairs[0][0, 0, :, cols], pairs[1][0, 0, :, cols]
        o1, l1, o2, l2 = (slab[a * BRANCH_TILES + j] for a in range(4))
        m = jnp.maximum(jnp.maximum(l0, l1), l2)
        e0, e1, e2 = jnp.exp(l0 - m), jnp.exp(l1 - m), jnp.exp(l2 - m)
        tiles.append(((e0 * o0 + e1 * o1 + e2 * o2) / (e0 + e1 + e2)).astype(BF16))
    out_ref[0] = x_ref[0] + _dot(jnp.concatenate(tiles, axis=-1), wo_ref[...])


def _merge(x, os, ls, w_o, tm=512):
    b, l, _ = x.shape
    row = pl.BlockSpec((1, tm, D_MODEL), lambda bi, i: (bi, i, 0))
    specs, args = [], []
    for d, o, lse in zip(DILATIONS, os, ls):
        spec = pl.BlockSpec((1, d, tm // d, MERGED_WIDTH), lambda bi, i: (bi, 0, i, 0))
        specs += [spec, spec]
        args += [o, lse]
    return pl.pallas_call(
        functools.partial(_merge_body, tm=tm),
        grid=(b, l // tm),
        in_specs=[row] + specs + [_const_spec(w_o.shape)],
        out_specs=row,
        out_shape=jax.ShapeDtypeStruct(x.shape, F32),
        scratch_shapes=[pltpu.VMEM((2 * (N_BRANCHES - 1) * BRANCH_TILES, tm, LANES), F32)],
        compiler_params=_params(("parallel", "parallel")),
        name="merge_wo",
    )(x, *args, w_o)


def _decode_body(q_ref, kn_ref, vn_ref, knt_ref, vnt_ref, ck_ref, cv_ref, o_ref, cko_ref, cvo_ref,
                 *, past, steps):
    scale = HEAD_DIM ** -0.5
    qi = lax.broadcasted_iota(jnp.int32, (steps, steps), 0)
    kj = lax.broadcasted_iota(jnp.int32, (steps, steps), 1)
    scores, values = [], []
    for br, d in enumerate(DILATIONS):
        w = WINDOWS[br]
        q = q_ref[0, 0, br].astype(BF16)
        kt = ck_ref[0, br, 0, :, past - w:].astype(BF16)
        back = (w + lax.broadcasted_iota(jnp.int32, (steps, w), 0)
                - lax.broadcasted_iota(jnp.int32, (steps, w), 1))
        ok = ((back & (d - 1)) == 0) & (back <= w)
        scores.append(jnp.where(ok, _dot(q, kt) * scale, NEG_INF))
        values.append((cv_ref[0, br, 0, :, past - w:].astype(BF16), True))
        ok = (kj <= qi) & (((qi - kj) & (d - 1)) == 0)
        scores.append(jnp.where(ok, _dot_nt(q, kn_ref[0, 0, br].astype(BF16)) * scale, NEG_INF))
        values.append((vn_ref[0, 0, br].astype(BF16), False))
    m = functools.reduce(jnp.maximum, [jnp.max(s, axis=-1, keepdims=True) for s in scores])
    num = jnp.zeros((steps, HEAD_DIM), F32)
    den = jnp.zeros((steps, 1), F32)
    for s, (v, transposed) in zip(scores, values):
        p = jnp.exp(s - m)
        den = den + jnp.sum(p, axis=-1, keepdims=True)
        num = num + (_dot_nt(p.astype(BF16), v) if transposed else _dot(p.astype(BF16), v))
    o_ref[0, 0] = num / den

    lane = lax.broadcasted_iota(jnp.int32, (HEAD_DIM, LANES), 1)
    tiles = past // LANES
    for src, new, dst in ((ck_ref, knt_ref, cko_ref), (cv_ref, vnt_ref, cvo_ref)):
        for br in range(N_BRANCHES):
            cur = pltpu.roll(src[0, br, 0, :, 0:LANES], LANES - steps, 1)
            for t in range(tiles):
                if t + 1 < tiles:
                    nxt = pltpu.roll(src[0, br, 0, :, (t + 1) * LANES:(t + 2) * LANES], LANES - steps, 1)
                else:
                    nxt = jnp.concatenate([jnp.zeros((HEAD_DIM, LANES - steps), F32), new[0, br, 0]], axis=1)
                dst[0, br, 0, :, t * LANES:(t + 1) * LANES] = jnp.where(lane < LANES - steps, cur, nxt)
                cur = nxt


def _decode(q, kn, vn, cache_k, cache_v):
    b, steps = q.shape[0], q.shape[1]
    past = cache_k.shape[1]
    assert past == MAX_WINDOW and steps <= LANES
    hb = HEADS_PER_BRANCH

    def slots(t):
        return t.reshape(b, steps, N_BRANCHES, hb, HEAD_DIM).transpose(0, 3, 2, 1, 4)

    def slots_t(t):
        return t.reshape(b, steps, N_BRANCHES, hb, HEAD_DIM).transpose(0, 2, 3, 4, 1)

    def cache_t(c):
        return c.transpose(0, 2, 3, 1).reshape(b, N_BRANCHES, hb, HEAD_DIM, past)

    new = pl.BlockSpec((1, 1, N_BRANCHES, steps, HEAD_DIM), lambda i, h: (i, h, 0, 0, 0))
    new_t = pl.BlockSpec((1, N_BRANCHES, 1, HEAD_DIM, steps), lambda i, h: (i, 0, h, 0, 0))
    cache = pl.BlockSpec((1, N_BRANCHES, 1, HEAD_DIM, past), lambda i, h: (i, 0, h, 0, 0))
    cache_shape = jax.ShapeDtypeStruct((b, N_BRANCHES, hb, HEAD_DIM, past), F32)
    o, ck, cv = pl.pallas_call(
        functools.partial(_decode_body, past=past, steps=steps),
        grid=(b, hb),
        in_specs=[new, new, new, new_t, new_t, cache, cache],
        out_specs=[pl.BlockSpec((1, 1, steps, HEAD_DIM), lambda i, h: (i, h, 0, 0)), cache, cache],
        out_shape=[jax.ShapeDtypeStruct((b, hb, steps, HEAD_DIM), F32), cache_shape, cache_shape],
        compiler_params=_params(("parallel", "parallel")),
        name="decode_attn_roll",
    )(slots(q), slots(kn), slots(vn), slots_t(kn), slots_t(vn), cache_t(cache_k), cache_t(cache_v))
    o = o.transpose(0, 2, 1, 3).reshape(b * steps, MERGED_WIDTH)
    ck, cv = (c.reshape(b, N_HEADS, HEAD_DIM, past).transpose(0, 3, 1, 2) for c in (ck, cv))
    return o, ck, cv


def _addproj_body(x_ref, o_ref, wo_ref, out_ref):
    out_ref[...] = x_ref[...] + _dot(o_ref[...].astype(BF16), wo_ref[...])


def _addproj(x, o, w_o):
    n = x.shape[0]
    return pl.pallas_call(
        _addproj_body,
        grid=(1,),
        in_specs=[_const_spec(x.shape), _const_spec(o.shape), _const_spec(w_o.shape)],
        out_specs=pl.BlockSpec(x.shape, lambda i: (0, 0)),
        out_shape=jax.ShapeDtypeStruct((n, D_MODEL), F32),
        compiler_params=_params(("arbitrary",)),
        name="add_wo",
    )(x, o, w_o)


def kernel(x_prompt, x_sample, state_ssm, cache_k, cache_v, ffn_norm1, ffn_w1_in, ffn_w1_out, ffn_norm2,
           ffn_w2_in, ffn_w2_out, mix_norm, ssm_a_re, ssm_a_im, ssm_log_dt, ssm_b_re, ssm_b_im, ssm_c_re,
           ssm_c_im, ssm_d, ssm_w_glu, kv_norm, w_kv, w_q, w_o, final_norm):
    bp, sp, _ = x_prompt.shape
    bs, ss, _ = x_sample.shape
    tm_scan = 256

    per_layer = lambda w: [w[layer].astype(BF16) for layer in range(w.shape[0])]
    w1_in, w1_out, w2_in, w2_out = (per_layer(w) for w in (ffn_w1_in, ffn_w1_out, ffn_w2_in, ffn_w2_out))
    w_glu, w_kv_b, w_q_b, w_o_b = (ssm_w_glu[0].astype(BF16), w_kv.astype(BF16), w_q[0].astype(BF16),
                                   w_o[0].astype(BF16))
    bbd, cbd, (pw_p, pw_s) = _s5_weights(ssm_a_re[0], ssm_a_im[0], ssm_log_dt[0], ssm_b_re[0], ssm_b_im[0],
                                         ssm_c_re[0], ssm_c_im[0], (_scan_powers(tm_scan), [1]))

    def layer_a(x, h0, pw, tm, seg):
        nb, rows, _ = x.shape
        x = _ffn(x.reshape(nb * rows, D_MODEL), ffn_norm1[0], w1_in[0], w1_out[0])
        x, hl_re, hl_im = _s5(x.reshape(nb, rows, D_MODEL), *h0, mix_norm[0], bbd, cbd, pw, ssm_d[0], w_glu,
                              tm=tm, seg=seg)
        x = _ffn(x.reshape(nb * rows, D_MODEL), ffn_norm2[0], w2_in[0], w2_out[0])
        return x, hl_re, hl_im

    tab_p = _rope_tables(jnp.arange(sp, dtype=jnp.int32))
    zeros = jnp.zeros((bp, SSM_SLABS, 1, LANES), F32)
    xp, hl_re, hl_im = layer_a(x_prompt, (zeros, zeros), pw_p, tm_scan, tm_scan)
    state_prompt = _slabs_to_state(hl_re[:, :, 0].transpose(1, 0, 2), hl_im[:, :, 0].transpose(1, 0, 2))[None]
    keep_p = min(MAX_WINDOW, sp)
    *kv_d, kt_p, vt_p = _proj_heads(xp.reshape(bp, sp, D_MODEL), kv_norm, w_kv_b, tab_p, 2, keep=keep_p)
    cache_k_prompt = kt_p.reshape(bp, N_HEADS, HEAD_DIM, keep_p).transpose(0, 3, 1, 2)
    cache_v_prompt = vt_p.reshape(bp, N_HEADS, HEAD_DIM, keep_p).transpose(0, 3, 1, 2)
    xp = _ffn(xp, ffn_norm1[1], w1_in[1], w1_out[1]).reshape(bp, sp, D_MODEL)
    q_d = _proj_heads(xp, mix_norm[1], w_q_b, tab_p, 1)
    outs, lses = [], []
    for br, d in enumerate(DILATIONS):
        flat = lambda t: t.reshape(bp, sp, MERGED_WIDTH)
        o, lse = _swa(flat(q_d[br]), flat(kv_d[br]), flat(kv_d[N_BRANCHES + br]), d)
        outs.append(o.reshape(bp, d, sp // d, MERGED_WIDTH))
        lses.append(lse.reshape(bp, d, sp // d, MERGED_WIDTH))
    xp = _merge(xp, outs, lses, w_o_b).reshape(bp * sp, D_MODEL)
    y_prompt = _ffn(xp, ffn_norm2[1], w2_in[1], w2_out[1], final_g=final_norm).reshape(bp, sp, D_MODEL)

    pos_s = jnp.tile(PAST_LEN + jnp.arange(ss, dtype=jnp.int32), bs)
    tab_s = _rope_tables(pos_s)
    h0_s = tuple(t[None] for t in _state_to_slabs(state_ssm[0]))
    xs, hl_re, hl_im = layer_a(x_sample.reshape(1, bs * ss, D_MODEL), h0_s, pw_s, bs * ss, ss)
    state_sample = _slabs_to_state(hl_re[0], hl_im[0])[None]
    k_s, v_s = (t.reshape(bs, ss, N_HEADS, HEAD_DIM) for t in _proj(xs, kv_norm, w_kv_b, tab_s, ATTN_WIDTH, 2))
    xs = _ffn(xs, ffn_norm1[1], w1_in[1], w1_out[1])
    q_s = _proj(xs, mix_norm[1], w_q_b, tab_s, ATTN_WIDTH, 1)[0].reshape(bs, ss, N_HEADS, HEAD_DIM)
    o_s, cache_k_sample, cache_v_sample = _decode(q_s, k_s, v_s, cache_k, cache_v)
    xs = _addproj(xs, o_s, w_o_b)
    y_sample = _ffn(xs, ffn_norm2[1], w2_in[1], w2_out[1], final_g=final_norm).reshape(bs, ss, D_MODEL)

    return (y_prompt, y_sample, state_prompt, cache_k_prompt, cache_v_prompt,
            state_sample, cache_k_sample, cache_v_sample)
```

```python
import functools

import jax
import jax.numpy as jnp
from jax import lax
from jax.experimental import pallas as pl
from jax.experimental.pallas import tpu as pltpu

F32 = jnp.float32
BF16 = jnp.bfloat16

D_MODEL = 1024
D_FF = 2816
SSM_GROUP = 16
N_SSM_GROUPS = D_MODEL // SSM_GROUP
SSM_STATE = 64
HEAD_DIM = 64
WINDOWS = (128, 512, 2048)
DILATIONS = (1, 4, 16)
N_BRANCHES = 3
HEADS_PER_BRANCH = 8
N_HEADS = N_BRANCHES * HEADS_PER_BRANCH
ATTN_WIDTH = N_HEADS * HEAD_DIM
MERGED_WIDTH = HEADS_PER_BRANCH * HEAD_DIM
MAX_WINDOW = max(WINDOWS)
ROT_DIM = HEAD_DIM // 4
ROPE_THETA = 500000.0
PAST_LEN = 16384
RMS_EPS = 1e-6
NEG_INF = -1e30

LANES = 128
SUBLANES = 8
BRANCH_TILES = MERGED_WIDTH // LANES

SSM_BLOCKS = 4
SSM_GROUPS_PER_BLOCK = N_SSM_GROUPS // SSM_BLOCKS
SSM_BLOCK_IN = SSM_GROUPS_PER_BLOCK * SSM_GROUP
SSM_BLOCK_HALF = SSM_GROUPS_PER_BLOCK * SSM_STATE
SSM_BLOCK_STATE = 2 * SSM_BLOCK_HALF
SSM_SLABS = N_SSM_GROUPS * SSM_STATE // LANES
SSM_SLABS_PER_BLOCK = SSM_SLABS // SSM_BLOCKS
SCAN_RADIX = 4
SCAN_BASE = SUBLANES
SCAN_UNROLL = 4

FF_CHUNK = 256
SWA_BLOCK = 128
SWA_SUB = 4
DECODE_SLOTS = 2

VMEM_LIMIT = 56 * 1024 * 1024


def _params(semantics):
    return pltpu.CompilerParams(dimension_semantics=semantics, vmem_limit_bytes=VMEM_LIMIT)


def _const_spec(shape):
    nd = len(shape)
    return pl.BlockSpec(shape, lambda *_: (0,) * nd, pipeline_mode=pl.Buffered(1))


def _rms(x, g):
    return x * lax.rsqrt(jnp.mean(x * x, axis=-1, keepdims=True) + RMS_EPS) * g


def _dot(a, b):
    return jnp.dot(a, b, preferred_element_type=F32)


def _dot_nt(a, b):
    return lax.dot_general(a, b, (((1,), (1,)), ((), ())), preferred_element_type=F32)


def _ffn_body(x_ref, g_ref, win_ref, wout_ref, *rest, final):
    if final:
        fg_ref, o_ref = rest
    else:
        (o_ref,) = rest
    x = x_ref[...]
    h = _rms(x, g_ref[...]).astype(BF16)
    acc = x
    for j in range(D_FF // FF_CHUNK):
        lo = j * FF_CHUNK
        gate = _dot(h, win_ref[:, lo:lo + FF_CHUNK])
        up = _dot(h, win_ref[:, D_FF + lo:D_FF + lo + FF_CHUNK])
        act = gate * jax.nn.sigmoid(gate) * up
        acc = acc + _dot((0.5 * act).astype(BF16), wout_ref[lo:lo + FF_CHUNK, :])
    if final:
        acc = _rms(acc, fg_ref[...])
    o_ref[...] = acc


def _layer_spec(w, layer):
    return pl.BlockSpec((None,) + w.shape[1:], lambda *_: (layer, 0, 0), pipeline_mode=pl.Buffered(1))


def _ffn(x, g, w_in, w_out, layer, final_g=None, tm=1024):
    n = x.shape[0]
    tm = min(tm, n)
    row = pl.BlockSpec((tm, D_MODEL), lambda i: (i, 0))
    in_specs = [row, _const_spec((1, D_MODEL)), _layer_spec(w_in, layer), _layer_spec(w_out, layer)]
    args = [x, g[layer].reshape(1, D_MODEL), w_in, w_out]
    if final_g is not None:
        in_specs.append(_const_spec((1, D_MODEL)))
        args.append(final_g.reshape(1, D_MODEL))
    return pl.pallas_call(
        functools.partial(_ffn_body, final=final_g is not None),
        grid=(n // tm,),
        in_specs=in_specs,
        out_specs=row,
        out_shape=jax.ShapeDtypeStruct((n, D_MODEL), F32),
        compiler_params=_params(("parallel",)),
        name="ffn",
    )(*args)


def _scan_levels(tm):
    levels, n = 0, tm
    while n > 1:
        assert n % SCAN_RADIX == 0
        n //= SCAN_RADIX
        levels += 1
    return levels


def _scan_powers(tm):
    pows = [SCAN_RADIX ** lvl * m for lvl in range(_scan_levels(tm)) for m in range(1, SCAN_RADIX)]
    return pows + [tm]


def _cmul(ar, ai, br, bi):
    return ar * br - ai * bi, ar * bi + ai * br


def _blocked_scan(hre, him, pw_ref, lv_re, lv_im, slab, slot, tm):
    levels = _scan_levels(tm)
    top = SCAN_BASE - 1

    def views(lvl):
        if lvl == 0:
            return (lambda sl: hre[slab, sl, :], lambda sl: him[slab, sl, :],
                    lambda sl, v: hre.__setitem__((slab, sl, slice(None)), v),
                    lambda sl, v: him.__setitem__((slab, sl, slice(None)), v))
        a, b = lv_re[lvl - 1], lv_im[lvl - 1]
        return (lambda sl: a[slot, sl, :], lambda sl: b[slot, sl, :],
                lambda sl, v: a.__setitem__((slot, sl, slice(None)), v),
                lambda sl, v: b.__setitem__((slot, sl, slice(None)), v))

    def power(k):
        return pw_ref[slab, pl.ds(2 * k, 1), :], pw_ref[slab, pl.ds(2 * k + 1, 1), :]

    def elems(n, tau):
        return pl.ds(SCAN_BASE + tau, n, stride=SCAN_RADIX)

    s_re, s_im = hre[slab, pl.ds(top, 1), :], him[slab, pl.ds(top, 1), :]
    for lvl in range(levels):
        get_re, get_im, put_re, put_im = views(lvl)
        n = tm // SCAN_RADIX ** (lvl + 1)
        lr, li = power((SCAN_RADIX - 1) * lvl)
        pr, pi = get_re(elems(n, 0)), get_im(elems(n, 0))
        for tau in range(1, SCAN_RADIX):
            mr, mi = _cmul(lr, li, pr, pi)
            pr, pi = mr + get_re(elems(n, tau)), mi + get_im(elems(n, tau))
            put_re(elems(n, tau), pr)
            put_im(elems(n, tau), pi)
        if lvl + 1 < levels:
            _, _, nput_re, nput_im = views(lvl + 1)
            nput_re(pl.ds(SCAN_BASE, n), pr)
            nput_im(pl.ds(SCAN_BASE, n), pi)
            nput_re(pl.ds(top, 1), s_re)
            nput_im(pl.ds(top, 1), s_im)
    for lvl in reversed(range(levels)):
        get_re, get_im, put_re, put_im = views(lvl)
        n = tm // SCAN_RADIX ** (lvl + 1)
        last = lvl + 1 == levels
        if not last:
            nget_re, nget_im, _, _ = views(lvl + 1)
            put_re(elems(n, SCAN_RADIX - 1), nget_re(pl.ds(SCAN_BASE, n)))
            put_im(elems(n, SCAN_RADIX - 1), nget_im(pl.ds(SCAN_BASE, n)))
        before = pl.ds(top, n, stride=SCAN_RADIX)
        tr, ti = get_re(before), get_im(before)
        for tau in range(SCAN_RADIX if last else SCAN_RADIX - 1):
            k = (SCAN_RADIX - 1) * lvl + tau if tau < SCAN_RADIX - 1 else (SCAN_RADIX - 1) * levels
            mr, mi = _cmul(*power(k), tr, ti)
            put_re(elems(n, tau), get_re(elems(n, tau)) + mr)
            put_im(elems(n, tau), get_im(elems(n, tau)) + mi)
    hre[slab, pl.ds(top, 1), :] = hre[slab, pl.ds(top + tm, 1), :]
    him[slab, pl.ds(top, 1), :] = him[slab, pl.ds(top + tm, 1), :]


def _s5_body(x_ref, h0re_ref, h0im_ref, g_ref, bbd_ref, cbd_ref, pw_ref, d_ref, wglu_ref,
             xo_ref, hlre_ref, hlim_ref, hre, him, *levels_scr, tm, seg):
    nseq = tm // seg
    rows = pl.ds(SCAN_BASE, tm)
    top = SCAN_BASE - 1
    n_lv = len(levels_scr) // 2
    lv_re, lv_im = levels_scr[:n_lv], levels_scr[n_lv:]

    if nseq == 1:
        @pl.when(pl.program_id(1) == 0)
        def _():
            hre[:, pl.ds(top, 1), :] = h0re_ref[0]
            him[:, pl.ds(top, 1), :] = h0im_ref[0]

    x = x_ref[0]
    u = _rms(x, g_ref[...])
    ub = u.astype(BF16)

    def project_in(k):
        bu = _dot(ub[:, k * SSM_BLOCK_IN:(k + 1) * SSM_BLOCK_IN], bbd_ref[k])
        for j in range(SSM_SLABS_PER_BLOCK):
            hre[k * SSM_SLABS_PER_BLOCK + j, rows, :] = bu[:, j * LANES:(j + 1) * LANES]
            him[k * SSM_SLABS_PER_BLOCK + j, rows, :] = bu[:, SSM_BLOCK_HALF + j * LANES:
                                                          SSM_BLOCK_HALF + (j + 1) * LANES]

    def project_out(k):
        sl = range(k * SSM_SLABS_PER_BLOCK, (k + 1) * SSM_SLABS_PER_BLOCK)
        hk = jnp.concatenate([hre[p, rows, :] for p in sl] + [him[p, rows, :] for p in sl], axis=-1)
        return _dot(hk.astype(BF16), cbd_ref[k])

    if nseq == 1:
        ys = []
        project_in(0)
        for k in range(SSM_BLOCKS):
            if k + 1 < SSM_BLOCKS:
                project_in(k + 1)
            for j in range(SSM_SLABS_PER_BLOCK):
                _blocked_scan(hre, him, pw_ref, lv_re, lv_im, k * SSM_SLABS_PER_BLOCK + j, j % SCAN_UNROLL, tm)
            ys.append(project_out(k))
        hlre_ref[0] = hre[:, pl.ds(top, 1), :]
        hlim_ref[0] = him[:, pl.ds(top, 1), :]
    else:
        for k in range(SSM_BLOCKS):
            project_in(k)

        def slab_body(p, carry):
            lr, li = pw_ref[p, pl.ds(0, 1), :], pw_ref[p, pl.ds(1, 1), :]
            hr, hi = h0re_ref[0, p], h0im_ref[0, p]
            for t in range(seg):
                step = pl.ds(SCAN_BASE + t, nseq, stride=seg)
                mr, mi = _cmul(lr, li, hr, hi)
                hr, hi = mr + hre[p, step, :], mi + him[p, step, :]
                hre[p, step, :] = hr
                him[p, step, :] = hi
            hlre_ref[0, p] = hr
            hlim_ref[0, p] = hi
            return carry

        lax.fori_loop(0, SSM_SLABS, slab_body, 0)
        ys = [project_out(k) for k in range(SSM_BLOCKS)]

    y = jnp.concatenate(ys, axis=-1) + d_ref[...] * u
    z = jax.nn.gelu(y, approximate=True).astype(BF16)
    zz = _dot(z, wglu_ref[...])
    xo_ref[0] = x + zz[:, :D_MODEL] * jax.nn.sigmoid(zz[:, D_MODEL:])


def _s5_weights(a_re, a_im, log_dt, b_re, b_im, c_re, c_im, powers):
    a_re, a_im = a_re.astype(F32), a_im.astype(F32)
    dt = jnp.exp(log_dt.astype(F32))[:, None]
    mag = jnp.exp(a_re * dt)
    lre, lim = mag * jnp.cos(a_im * dt), mag * jnp.sin(a_im * dt)
    den = a_re * a_re + a_im * a_im
    fre = ((lre - 1.0) * a_re + lim * a_im) / den
    fim = (lim * a_re - (lre - 1.0) * a_im) / den
    b_re, b_im = b_re.astype(F32), b_im.astype(F32)
    bb = jnp.stack([fre[:, :, None] * b_re - fim[:, :, None] * b_im,
                    fre[:, :, None] * b_im + fim[:, :, None] * b_re])
    eye = jnp.eye(SSM_GROUPS_PER_BLOCK, dtype=F32)
    bb = bb.reshape(2, SSM_BLOCKS, SSM_GROUPS_PER_BLOCK, SSM_STATE, SSM_GROUP)
    bbd = jnp.einsum("rkgph,gq->kghrqp", bb, eye).reshape(SSM_BLOCKS, SSM_BLOCK_IN, SSM_BLOCK_STATE)
    cc = jnp.stack([c_re.astype(F32), -c_im.astype(F32)])
    cc = cc.reshape(2, SSM_BLOCKS, SSM_GROUPS_PER_BLOCK, SSM_GROUP, SSM_STATE)
    cbd = jnp.einsum("rkghp,gq->krgpqh", cc, eye).reshape(SSM_BLOCKS, SSM_BLOCK_STATE, SSM_BLOCK_IN)
    tables = []
    for pows in powers:
        k = jnp.asarray(pows, F32)[:, None, None]
        mag_k = jnp.exp(k * (a_re * dt))
        pw = jnp.stack([mag_k * jnp.cos(k * (a_im * dt)), mag_k * jnp.sin(k * (a_im * dt))], axis=1)
        tables.append(pw.reshape(2 * len(pows), SSM_SLABS, LANES).transpose(1, 0, 2))
    return bbd.astype(BF16), cbd.astype(BF16), tables


def _state_to_slabs(h):
    b = h.shape[0]
    h = h.astype(F32).reshape(b, SSM_SLABS, LANES, 2).transpose(3, 1, 0, 2)
    return h[0], h[1]


def _slabs_to_state(re, im):
    b = re.shape[1]
    h = jnp.stack([re, im], axis=-1).transpose(1, 0, 2, 3)
    return h.reshape(b, N_SSM_GROUPS, SSM_STATE, 2)


def _s5(x, h0re, h0im, g, bbd, cbd, pw, d_skip, w_glu, *, tm, seg):
    nb, rows, _ = x.shape
    nseq = tm // seg
    assert nseq == 1 or rows == tm
    row = pl.BlockSpec((1, tm, D_MODEL), lambda b, i: (b, i, 0))
    st = pl.BlockSpec((1, SSM_SLABS, nseq, LANES), lambda b, i: (b, 0, 0, 0))
    slab = pltpu.VMEM((SSM_SLABS, SCAN_BASE + tm, LANES), F32)
    levels = []
    if nseq == 1:
        levels = [pltpu.VMEM((SCAN_UNROLL, SCAN_BASE + max(tm // SCAN_RADIX ** lvl, SUBLANES), LANES), F32)
                  for lvl in range(1, _scan_levels(tm))] * 2
    st_shape = jax.ShapeDtypeStruct(h0re.shape, F32)
    return pl.pallas_call(
        functools.partial(_s5_body, tm=tm, seg=seg),
        grid=(nb, rows // tm),
        in_specs=[row, st, st, _const_spec((1, D_MODEL)), _const_spec(bbd.shape), _const_spec(cbd.shape),
                  _const_spec(pw.shape), _const_spec((1, D_MODEL)), _const_spec(w_glu.shape)],
        out_specs=[row, st, st],
        out_shape=[jax.ShapeDtypeStruct(x.shape, F32), st_shape, st_shape],
        scratch_shapes=[slab, slab] + levels,
        compiler_params=_params(("parallel", "arbitrary")),
        name="s5_mixer",
    )(x, h0re, h0im, g.reshape(1, D_MODEL), bbd, cbd, pw, d_skip.reshape(1, D_MODEL), w_glu)


def _rope_tables(pos):
    half = ROT_DIM // 2
    inv = ROPE_THETA ** (-jnp.arange(0, ROT_DIM, 2, dtype=F32) / ROT_DIM)
    ang = pos.astype(F32)[:, None] * inv[None, :]
    cos, sin = jnp.cos(ang), jnp.sin(ang)
    n = pos.shape[0]
    pad = HEAD_DIM - ROT_DIM
    c = jnp.concatenate([cos, cos, jnp.ones((n, pad), F32)], axis=-1)
    s1 = jnp.concatenate([-sin, jnp.zeros((n, half + pad), F32)], axis=-1)
    s2 = jnp.concatenate([jnp.zeros((n, half), F32), sin, jnp.zeros((n, pad), F32)], axis=-1)
    rep = LANES // HEAD_DIM
    return tuple(jnp.tile(t, (1, rep)) for t in (c, s1, s2))


def _rope_tile(y, c, s1, s2):
    half = ROT_DIM // 2
    return y * c + pltpu.roll(y, LANES - half, 1) * s1 + pltpu.roll(y, half, 1) * s2


def _proj_body(x_ref, g_ref, w_ref, c_ref, s1_ref, s2_ref, *o_refs, rope_tiles):
    h = _rms(x_ref[...], g_ref[...]).astype(BF16)
    y = _dot(h, w_ref[...])
    c, s1, s2 = c_ref[...], s1_ref[...], s2_ref[...]
    tiles_per_out = y.shape[1] // LANES // len(o_refs)
    for j in range(y.shape[1] // LANES):
        yj = y[:, j * LANES:(j + 1) * LANES]
        if j < rope_tiles:
            yj = _rope_tile(yj, c, s1, s2)
        jo = j % tiles_per_out
        o_refs[j // tiles_per_out][:, jo * LANES:(jo + 1) * LANES] = yj


def _proj(x, g, w, tables, rope_width, n_out, tm=512):
    n = x.shape[0]
    tm = min(tm, n)
    width = w.shape[1] // n_out
    row = pl.BlockSpec((tm, D_MODEL), lambda i: (i, 0))
    tab = pl.BlockSpec((tm, LANES), lambda i: (i, 0))
    return pl.pallas_call(
        functools.partial(_proj_body, rope_tiles=rope_width // LANES),
        grid=(n // tm,),
        in_specs=[row, _const_spec((1, D_MODEL)), _const_spec(w.shape), tab, tab, tab],
        out_specs=[pl.BlockSpec((tm, width), lambda i: (i, 0))] * n_out,
        out_shape=[jax.ShapeDtypeStruct((n, width), F32)] * n_out,
        compiler_params=_params(("parallel",)),
        name="proj_rope",
    )(x, g.reshape(1, D_MODEL), w, *tables)


def _proj_heads_body(x_ref, g_ref, w_ref, c_ref, s1_ref, s2_ref, *refs, n_mat, tm, cache_from):
    n_t = n_mat if cache_from is not None else 0
    outs = refs[:N_BRANCHES * n_mat]
    t_refs = refs[N_BRANCHES * n_mat:N_BRANCHES * n_mat + n_t]
    slab = refs[-1]
    h = _rms(x_ref[0], g_ref[...]).astype(BF16)
    y = _dot(h, w_ref[...])
    c, s1, s2 = c_ref[...], s1_ref[...], s2_ref[...]
    for m in range(n_mat):
        tiles = []
        for br, d in enumerate(DILATIONS):
            lo = m * ATTN_WIDTH + br * MERGED_WIDTH
            part = [y[:, lo + j * LANES:lo + (j + 1) * LANES] for j in range(BRANCH_TILES)]
            if m == 0:
                part = [_rope_tile(t, c, s1, s2) for t in part]
            tiles += part
            out = outs[N_BRANCHES * m + br]
            if d == 1:
                for j in range(BRANCH_TILES):
                    out[0, 0, :, j * LANES:(j + 1) * LANES] = part[j].astype(BF16)
                continue
            for j in range(BRANCH_TILES):
                slab[(br - 1) * BRANCH_TILES + j] = part[j]
            for r in range(d):
                for j in range(BRANCH_TILES):
                    rows = slab[(br - 1) * BRANCH_TILES + j, pl.ds(r, tm // d, stride=d), :]
                    out[0, r, :, j * LANES:(j + 1) * LANES] = rows.astype(BF16)
        if cache_from is not None:
            @pl.when(pl.program_id(1) >= cache_from)
            def _(m=m, tiles=tiles):
                t_refs[m][0] = jnp.concatenate(tiles, axis=-1).T


def _proj_heads(x, g, w, tables, n_mat, keep=None, tm=512):
    b, l, _ = x.shape
    assert w.shape[1] == n_mat * ATTN_WIDTH and tm % (DILATIONS[-1] * 2 * SUBLANES) == 0
    cache_from = None if keep is None else (l - keep) // tm
    row = pl.BlockSpec((1, tm, D_MODEL), lambda bi, i: (bi, i, 0))
    tab = pl.BlockSpec((tm, LANES), lambda bi, i: (i, 0))
    out_specs, out_shape = [], []
    for _ in range(n_mat):
        for d in DILATIONS:
            out_specs.append(pl.BlockSpec((1, d, tm // d, MERGED_WIDTH), lambda bi, i: (bi, 0, i, 0)))
            out_shape.append(jax.ShapeDtypeStruct((b, d, l // d, MERGED_WIDTH), BF16))
    if keep is not None:
        for _ in range(n_mat):
            out_specs.append(pl.BlockSpec((1, ATTN_WIDTH, tm),
                                          lambda bi, i: (bi, 0, jnp.maximum(i - cache_from, 0))))
            out_shape.append(jax.ShapeDtypeStruct((b, ATTN_WIDTH, keep), F32))
    return pl.pallas_call(
        functools.partial(_proj_heads_body, n_mat=n_mat, tm=tm, cache_from=cache_from),
        grid=(b, l // tm),
        in_specs=[row, _const_spec((1, D_MODEL)), _const_spec(w.shape), tab, tab, tab],
        out_specs=out_specs,
        out_shape=out_shape,
        scratch_shapes=[pltpu.VMEM(((N_BRANCHES - 1) * BRANCH_TILES, tm, LANES), F32)],
        compiler_params=_params(("parallel", "arbitrary")),
        name="proj_heads",
    )(x, g.reshape(1, D_MODEL), w, *tables)


def _swa_body(q_ref, kp_ref, kc_ref, vp_ref, vc_ref, o_ref, lse_ref, *, steps_per_seq):
    n = pl.program_id(1)
    k = jnp.concatenate([kp_ref[0], kc_ref[0]], axis=0)
    v = jnp.concatenate([vp_ref[0], vc_ref[0]], axis=0)
    qi = lax.broadcasted_iota(jnp.int32, (SWA_BLOCK, 2 * SWA_BLOCK), 0)
    kj = lax.broadcasted_iota(jnp.int32, (SWA_BLOCK, 2 * SWA_BLOCK), 1)
    band = (kj >= qi) & (kj <= qi + SWA_BLOCK)
    has_prev = (n % steps_per_seq) != 0
    first = band & ((kj >= SWA_BLOCK) | has_prev)
    low = lax.broadcasted_iota(jnp.int32, (SWA_BLOCK, LANES), 1) < HEAD_DIM
    zero = jnp.zeros((SWA_BLOCK, LANES), BF16)
    for j in range(SWA_SUB):
        rows = slice(j * SWA_BLOCK, (j + 1) * SWA_BLOCK)
        keys = slice(j * SWA_BLOCK, (j + 2) * SWA_BLOCK)
        valid = first if j == 0 else band
        for t in range(BRANCH_TILES):
            cols = slice(t * LANES, (t + 1) * LANES)
            q = q_ref[0, rows, cols] * (HEAD_DIM ** -0.5)
            kt, vt = k[keys, cols], v[keys, cols]
            outs, lses = [], []
            for qh in (jnp.where(low, q, zero), jnp.where(low, zero, q)):
                s = jnp.where(valid, _dot_nt(qh, kt), NEG_INF)
                m = jnp.max(s, axis=-1, keepdims=True)
                p = jnp.exp(s - m)
                l = jnp.sum(p, axis=-1, keepdims=True)
                outs.append(_dot(p.astype(BF16), vt) / l)
                lses.append(jnp.broadcast_to(m + jnp.log(l), (SWA_BLOCK, LANES)))
            o_ref[0, rows, cols] = jnp.where(low, outs[0], outs[1])
            lse_ref[0, rows, cols] = jnp.where(low, lses[0], lses[1])


def _swa(q, k, v, dilation):
    b, l, w = q.shape
    step = SWA_SUB * SWA_BLOCK
    assert (l // dilation) % step == 0
    cur = pl.BlockSpec((1, step, w), lambda bi, n: (bi, n, 0))
    prev = pl.BlockSpec((1, SWA_BLOCK, w), lambda bi, n: (bi, jnp.maximum(SWA_SUB * n - 1, 0), 0))
    out = jax.ShapeDtypeStruct((b, l, w), F32)
    return pl.pallas_call(
        functools.partial(_swa_body, steps_per_seq=l // dilation // step),
        grid=(b, l // step),
        in_specs=[cur, prev, cur, prev, cur],
        out_specs=[cur, cur],
        out_shape=[out, out],
        compiler_params=_params(("parallel", "arbitrary")),
        name="swa",
    )(q, k, k, v, v)


def _merge_body(x_ref, *refs, tm):
    pairs, wo_ref, out_ref, slab = refs[:2 * N_BRANCHES], refs[2 * N_BRANCHES], refs[-2], refs[-1]
    for a, ref in enumerate(pairs[2:]):
        d = DILATIONS[1 + a // 2]
        for r in range(d):
            for j in range(BRANCH_TILES):
                slab[a * BRANCH_TILES + j, pl.ds(r, tm // d, stride=d), :] = ref[0, r, :, j * LANES:(j + 1) * LANES]
    tiles = []
    for j in range(BRANCH_TILES):
        cols = slice(j * LANES, (j + 1) * LANES)
        o0, l0 = pairs[0][0, 0, :, cols], pairs[1][0, 0, :, cols]
        o1, l1, o2, l2 = (slab[a * BRANCH_TILES + j] for a in range(4))
        m = jnp.maximum(jnp.maximum(l0, l1), l2)
        e0, e1, e2 = jnp.exp(l0 - m), jnp.exp(l1 - m), jnp.exp(l2 - m)
        tiles.append(((e0 * o0 + e1 * o1 + e2 * o2) / (e0 + e1 + e2)).astype(BF16))
    out_ref[0] = x_ref[0] + _dot(jnp.concatenate(tiles, axis=-1), wo_ref[...])


def _merge(x, os, ls, w_o, tm=512):
    b, l, _ = x.shape
    row = pl.BlockSpec((1, tm, D_MODEL), lambda bi, i: (bi, i, 0))
    specs, args = [], []
    for d, o, lse in zip(DILATIONS, os, ls):
        spec = pl.BlockSpec((1, d, tm // d, MERGED_WIDTH), lambda bi, i: (bi, 0, i, 0))
        specs += [spec, spec]
        args += [o, lse]
    return pl.pallas_call(
        functools.partial(_merge_body, tm=tm),
        grid=(b, l // tm),
        in_specs=[row] + specs + [_const_spec(w_o.shape)],
        out_specs=row,
        out_shape=jax.ShapeDtypeStruct(x.shape, F32),
        scratch_shapes=[pltpu.VMEM((2 * (N_BRANCHES - 1) * BRANCH_TILES, tm, LANES), F32)],
        compiler_params=_params(("parallel", "parallel")),
        name="merge_wo",
    )(x, *args, w_o)


def _decode_body(q_ref, kn_ref, vn_ref, knt_ref, vnt_ref, ck_ref, cv_ref, o_ref, cko_ref, cvo_ref,
                 *, past, steps):
    scale = HEAD_DIM ** -0.5
    qi = lax.broadcasted_iota(jnp.int32, (steps, steps), 0)
    kj = lax.broadcasted_iota(jnp.int32, (steps, steps), 1)
    lane = lax.broadcasted_iota(jnp.int32, (HEAD_DIM, LANES), 1)
    tiles = past // LANES
    for sl in range(DECODE_SLOTS):
        scores, values = [], []
        for br, d in enumerate(DILATIONS):
            w = WINDOWS[br]
            q = q_ref[0, sl, br].astype(BF16)
            kt = ck_ref[0, br, sl, :, past - w:].astype(BF16)
            back = (w + lax.broadcasted_iota(jnp.int32, (steps, w), 0)
                    - lax.broadcasted_iota(jnp.int32, (steps, w), 1))
            ok = ((back & (d - 1)) == 0) & (back <= w)
            scores.append(jnp.where(ok, _dot(q, kt) * scale, NEG_INF))
            values.append((cv_ref[0, br, sl, :, past - w:].astype(BF16), True))
            ok = (kj <= qi) & (((qi - kj) & (d - 1)) == 0)
            scores.append(jnp.where(ok, _dot_nt(q, kn_ref[0, sl, br].astype(BF16)) * scale, NEG_INF))
            values.append((vn_ref[0, sl, br].astype(BF16), False))
        m = functools.reduce(jnp.maximum, [jnp.max(s, axis=-1, keepdims=True) for s in scores])
        num = jnp.zeros((steps, HEAD_DIM), F32)
        den = jnp.zeros((steps, 1), F32)
        for s, (v, transposed) in zip(scores, values):
            p = jnp.exp(s - m)
            den = den + jnp.sum(p, axis=-1, keepdims=True)
            num = num + (_dot_nt(p.astype(BF16), v) if transposed else _dot(p.astype(BF16), v))
        o_ref[0, sl] = num / den

        for src, new, dst in ((ck_ref, knt_ref, cko_ref), (cv_ref, vnt_ref, cvo_ref)):
            for br in range(N_BRANCHES):
                cur = pltpu.roll(src[0, br, sl, :, 0:LANES], LANES - steps, 1)
                for t in range(tiles):
                    if t + 1 < tiles:
                        nxt = pltpu.roll(src[0, br, sl, :, (t + 1) * LANES:(t + 2) * LANES], LANES - steps, 1)
                    else:
                        nxt = jnp.concatenate([jnp.zeros((HEAD_DIM, LANES - steps), F32), new[0, br, sl]], axis=1)
                    dst[0, br, sl, :, t * LANES:(t + 1) * LANES] = jnp.where(lane < LANES - steps, cur, nxt)
                    cur = nxt


def _decode(q, kn, vn, cache_k, cache_v):
    b, steps = q.shape[0], q.shape[1]
    past = cache_k.shape[1]
    assert past == MAX_WINDOW and steps <= LANES
    hb, ns = HEADS_PER_BRANCH, DECODE_SLOTS

    def slots(t):
        return t.reshape(b, steps, N_BRANCHES, hb, HEAD_DIM).transpose(0, 3, 2, 1, 4)

    def slots_t(t):
        return t.reshape(b, steps, N_BRANCHES, hb, HEAD_DIM).transpose(0, 2, 3, 4, 1)

    def cache_t(c):
        return c.transpose(0, 2, 3, 1).reshape(b, N_BRANCHES, hb, HEAD_DIM, past)

    new = pl.BlockSpec((1, ns, N_BRANCHES, steps, HEAD_DIM), lambda i, h: (i, h, 0, 0, 0))
    new_t = pl.BlockSpec((1, N_BRANCHES, ns, HEAD_DIM, steps), lambda i, h: (i, 0, h, 0, 0))
    cache = pl.BlockSpec((1, N_BRANCHES, ns, HEAD_DIM, past), lambda i, h: (i, 0, h, 0, 0))
    cache_shape = jax.ShapeDtypeStruct((b, N_BRANCHES, hb, HEAD_DIM, past), F32)
    o, ck, cv = pl.pallas_call(
        functools.partial(_decode_body, past=past, steps=steps),
        grid=(b, hb // ns),
        in_specs=[new, new, new, new_t, new_t, cache, cache],
        out_specs=[pl.BlockSpec((1, ns, steps, HEAD_DIM), lambda i, h: (i, h, 0, 0)), cache, cache],
        out_shape=[jax.ShapeDtypeStruct((b, hb, steps, HEAD_DIM), F32), cache_shape, cache_shape],
        compiler_params=_params(("parallel", "parallel")),
        name="decode_attn_roll",
    )(slots(q), slots(kn), slots(vn), slots_t(kn), slots_t(vn), cache_t(cache_k), cache_t(cache_v))
    o = o.transpose(0, 2, 1, 3).reshape(b * steps, MERGED_WIDTH)
    ck, cv = (c.reshape(b, N_HEADS, HEAD_DIM, past).transpose(0, 3, 1, 2) for c in (ck, cv))
    return o, ck, cv


def _addproj_body(x_ref, o_ref, wo_ref, out_ref):
    out_ref[...] = x_ref[...] + _dot(o_ref[...].astype(BF16), wo_ref[...])


def _addproj(x, o, w_o):
    n = x.shape[0]
    return pl.pallas_call(
        _addproj_body,
        grid=(1,),
        in_specs=[_const_spec(x.shape), _const_spec(o.shape), _const_spec(w_o.shape)],
        out_specs=pl.BlockSpec(x.shape, lambda i: (0, 0)),
        out_shape=jax.ShapeDtypeStruct((n, D_MODEL), F32),
        compiler_params=_params(("arbitrary",)),
        name="add_wo",
    )(x, o, w_o)


def kernel(x_prompt, x_sample, state_ssm, cache_k, cache_v, ffn_norm1, ffn_w1_in, ffn_w1_out, ffn_norm2,
           ffn_w2_in, ffn_w2_out, mix_norm, ssm_a_re, ssm_a_im, ssm_log_dt, ssm_b_re, ssm_b_im, ssm_c_re,
           ssm_c_im, ssm_d, ssm_w_glu, kv_norm, w_kv, w_q, w_o, final_norm):
    bp, sp, _ = x_prompt.shape
    bs, ss, _ = x_sample.shape
    tm_scan = 256

    w1_in, w1_out, w2_in, w2_out = (w.astype(BF16) for w in (ffn_w1_in, ffn_w1_out, ffn_w2_in, ffn_w2_out))
    w_glu, w_kv_b, w_q_b, w_o_b = (ssm_w_glu[0].astype(BF16), w_kv.astype(BF16), w_q[0].astype(BF16),
                                   w_o[0].astype(BF16))
    bbd, cbd, (pw_p, pw_s) = _s5_weights(ssm_a_re[0], ssm_a_im[0], ssm_log_dt[0], ssm_b_re[0], ssm_b_im[0],
                                         ssm_c_re[0], ssm_c_im[0], (_scan_powers(tm_scan), [1]))

    def layer_a(x, h0, pw, tm, seg):
        nb, rows, _ = x.shape
        x = _ffn(x.reshape(nb * rows, D_MODEL), ffn_norm1, w1_in, w1_out, 0)
        x, hl_re, hl_im = _s5(x.reshape(nb, rows, D_MODEL), *h0, mix_norm[0], bbd, cbd, pw, ssm_d[0], w_glu,
                              tm=tm, seg=seg)
        x = _ffn(x.reshape(nb * rows, D_MODEL), ffn_norm2, w2_in, w2_out, 0)
        return x, hl_re, hl_im

    tab_p = _rope_tables(jnp.arange(sp, dtype=jnp.int32))
    zeros = jnp.zeros((bp, SSM_SLABS, 1, LANES), F32)
    xp, hl_re, hl_im = layer_a(x_prompt, (zeros, zeros), pw_p, tm_scan, tm_scan)
    state_prompt = _slabs_to_state(hl_re[:, :, 0].transpose(1, 0, 2), hl_im[:, :, 0].transpose(1, 0, 2))[None]
    keep_p = min(MAX_WINDOW, sp)
    *kv_d, kt_p, vt_p = _proj_heads(xp.reshape(bp, sp, D_MODEL), kv_norm, w_kv_b, tab_p, 2, keep=keep_p)
    cache_k_prompt = kt_p.reshape(bp, N_HEADS, HEAD_DIM, keep_p).transpose(0, 3, 1, 2)
    cache_v_prompt = vt_p.reshape(bp, N_HEADS, HEAD_DIM, keep_p).transpose(0, 3, 1, 2)
    xp = _ffn(xp, ffn_norm1, w1_in, w1_out, 1).reshape(bp, sp, D_MODEL)
    q_d = _proj_heads(xp, mix_norm[1], w_q_b, tab_p, 1)
    outs, lses = [], []
    for br, d in enumerate(DILATIONS):
        flat = lambda t: t.reshape(bp, sp, MERGED_WIDTH)
        o, lse = _swa(flat(q_d[br]), flat(kv_d[br]), flat(kv_d[N_BRANCHES + br]), d)
        outs.append(o.reshape(bp, d, sp // d, MERGED_WIDTH))
        lses.append(lse.reshape(bp, d, sp // d, MERGED_WIDTH))
    xp = _merge(xp, outs, lses, w_o_b).reshape(bp * sp, D_MODEL)
    y_prompt = _ffn(xp, ffn_norm2, w2_in, w2_out, 1, final_g=final_norm).reshape(bp, sp, D_MODEL)

    pos_s = jnp.tile(PAST_LEN + jnp.arange(ss, dtype=jnp.int32), bs)
    tab_s = _rope_tables(pos_s)
    h0_s = tuple(t[None] for t in _state_to_slabs(state_ssm[0]))
    xs, hl_re, hl_im = layer_a(x_sample.reshape(1, bs * ss, D_MODEL), h0_s, pw_s, bs * ss, ss)
    state_sample = _slabs_to_state(hl_re[0], hl_im[0])[None]
    k_s, v_s = (t.reshape(bs, ss, N_HEADS, HEAD_DIM) for t in _proj(xs, kv_norm, w_kv_b, tab_s, ATTN_WIDTH, 2))
    xs = _ffn(xs, ffn_norm1, w1_in, w1_out, 1)
    q_s = _proj(xs, mix_norm[1], w_q_b, tab_s, ATTN_WIDTH, 1)[0].reshape(bs, ss, N_HEADS, HEAD_DIM)
    o_s, cache_k_sample, cache_v_sample = _decode(q_s, k_s, v_s, cache_k, cache_v)
    xs = _addproj(xs, o_s, w_o_b)
    y_sample = _ffn(xs, ffn_norm2, w2_in, w2_out, 1, final_g=final_norm).reshape(bs, ss, D_MODEL)

    return (y_prompt, y_sample, state_prompt, cache_k_prompt, cache_v_prompt,
            state_sample, cache_k_sample, cache_v_sample)
```

```python
import functools

import jax
import jax.numpy as jnp
from jax import lax
from jax.experimental import pallas as pl
from jax.experimental.pallas import tpu as pltpu

F32 = jnp.float32
BF16 = jnp.bfloat16

D_MODEL = 1024
D_FF = 2816
SSM_GROUP = 16
N_SSM_GROUPS = D_MODEL // SSM_GROUP
SSM_STATE = 64
HEAD_DIM = 64
WINDOWS = (128, 512, 2048)
DILATIONS = (1, 4, 16)
N_BRANCHES = 3
HEADS_PER_BRANCH = 8
N_HEADS = N_BRANCHES * HEADS_PER_BRANCH
ATTN_WIDTH = N_HEADS * HEAD_DIM
MERGED_WIDTH = HEADS_PER_BRANCH * HEAD_DIM
MAX_WINDOW = max(WINDOWS)
ROT_DIM = HEAD_DIM // 4
ROPE_THETA = 500000.0
PAST_LEN = 16384
RMS_EPS = 1e-6
NEG_INF = -1e30

LANES = 128
SUBLANES = 8
BRANCH_TILES = MERGED_WIDTH // LANES

SSM_BLOCKS = 4
SSM_GROUPS_PER_BLOCK = N_SSM_GROUPS // SSM_BLOCKS
SSM_BLOCK_IN = SSM_GROUPS_PER_BLOCK * SSM_GROUP
SSM_BLOCK_HALF = SSM_GROUPS_PER_BLOCK * SSM_STATE
SSM_BLOCK_STATE = 2 * SSM_BLOCK_HALF
SSM_SLABS = N_SSM_GROUPS * SSM_STATE // LANES
SSM_SLABS_PER_BLOCK = SSM_SLABS // SSM_BLOCKS
SCAN_RADIX = 4
SCAN_BASE = SUBLANES
SCAN_UNROLL = 4

FF_CHUNK = 256
SWA_BLOCK = 128
SWA_SUB = 4
ROLL_GROUP = 4

VMEM_LIMIT = 56 * 1024 * 1024


def _params(semantics):
    return pltpu.CompilerParams(dimension_semantics=semantics, vmem_limit_bytes=VMEM_LIMIT)


def _const_spec(shape):
    nd = len(shape)
    return pl.BlockSpec(shape, lambda *_: (0,) * nd, pipeline_mode=pl.Buffered(1))


def _rms(x, g):
    return x * lax.rsqrt(jnp.mean(x * x, axis=-1, keepdims=True) + RMS_EPS) * g


def _dot(a, b):
    return jnp.dot(a, b, preferred_element_type=F32)


def _dot_nt(a, b):
    return lax.dot_general(a, b, (((1,), (1,)), ((), ())), preferred_element_type=F32)


def _ffn_body(x_ref, g_ref, win_ref, wout_ref, *rest, final):
    if final:
        fg_ref, o_ref = rest
    else:
        (o_ref,) = rest
    x = x_ref[...]
    h = _rms(x, g_ref[...]).astype(BF16)
    acc = x
    for j in range(D_FF // FF_CHUNK):
        lo = j * FF_CHUNK
        gate = _dot(h, win_ref[:, lo:lo + FF_CHUNK])
        up = _dot(h, win_ref[:, D_FF + lo:D_FF + lo + FF_CHUNK])
        act = gate * jax.nn.sigmoid(gate) * up
        acc = acc + _dot((0.5 * act).astype(BF16), wout_ref[lo:lo + FF_CHUNK, :])
    if final:
        acc = _rms(acc, fg_ref[...])
    o_ref[...] = acc


def _layer_spec(w, layer):
    return pl.BlockSpec((None,) + w.shape[1:], lambda *_: (layer, 0, 0), pipeline_mode=pl.Buffered(1))


def _ffn(x, g, w_in, w_out, layer, final_g=None, tm=1024):
    n = x.shape[0]
    tm = min(tm, n)
    row = pl.BlockSpec((tm, D_MODEL), lambda i: (i, 0))
    in_specs = [row, _const_spec((1, D_MODEL)), _layer_spec(w_in, layer), _layer_spec(w_out, layer)]
    args = [x, g[layer].reshape(1, D_MODEL), w_in, w_out]
    if final_g is not None:
        in_specs.append(_const_spec((1, D_MODEL)))
        args.append(final_g.reshape(1, D_MODEL))
    return pl.pallas_call(
        functools.partial(_ffn_body, final=final_g is not None),
        grid=(n // tm,),
        in_specs=in_specs,
        out_specs=row,
        out_shape=jax.ShapeDtypeStruct((n, D_MODEL), F32),
        compiler_params=_params(("parallel",)),
        name="ffn",
    )(*args)


def _ffn_roll_body(x_ref, g_ref, win_ref, wout_ref, *rest, final, aliased, batch0, past, steps):
    rest = list(rest)
    fg_ref = rest.pop(0) if final else None
    new_ref, cache_hbm = rest.pop(0), rest.pop(0)
    if aliased:
        rest.pop(0)
    o_ref, rolled_hbm, in_buf, out_buf, in_sem, out_sem = rest
    b = batch0 + pl.program_id(0)
    n_groups = N_HEADS // ROLL_GROUP
    n_chunks = D_FF // FF_CHUNK
    lane = lax.broadcasted_iota(jnp.int32, (HEAD_DIM, LANES), 1)
    tiles = past // LANES

    def copies(g, inward):
        out = []
        for k in range(ROLL_GROUP):
            br, sl = divmod(g * ROLL_GROUP + k, HEADS_PER_BRANCH)
            buf = (g % 2) * ROLL_GROUP + k
            if inward:
                out.append(pltpu.make_async_copy(cache_hbm.at[b, br, sl], in_buf.at[buf], in_sem.at[g % 2]))
            else:
                out.append(pltpu.make_async_copy(out_buf.at[buf], rolled_hbm.at[b, br, sl], out_sem.at[g % 2]))
        return out

    def arrive(g):
        for c in copies(g, True):
            c.wait()
        if g >= 2:
            for c in copies(g - 2, False):
                c.wait()
        if g + 1 < n_groups:
            for c in copies(g + 1, True):
                c.start()

    def shift(g):
        for k in range(ROLL_GROUP):
            br, sl = divmod(g * ROLL_GROUP + k, HEADS_PER_BRANCH)
            src, dst = in_buf.at[(g % 2) * ROLL_GROUP + k], out_buf.at[(g % 2) * ROLL_GROUP + k]
            cur = pltpu.roll(src[:, 0:LANES], LANES - steps, 1)
            for t in range(tiles):
                if t + 1 < tiles:
                    nxt = pltpu.roll(src[:, (t + 1) * LANES:(t + 2) * LANES], LANES - steps, 1)
                else:
                    nxt = jnp.concatenate([jnp.zeros((HEAD_DIM, LANES - steps), F32), new_ref[0, br, sl]], axis=1)
                dst[:, t * LANES:(t + 1) * LANES] = jnp.where(lane < LANES - steps, cur, nxt)
                cur = nxt
        for c in copies(g, False):
            c.start()

    for c in copies(0, True):
        c.start()
    x = x_ref[...]
    h = _rms(x, g_ref[...]).astype(BF16)
    acc = x
    group_at = {(2 * g + 1) * n_chunks // (2 * n_groups): g for g in range(n_groups)}
    for j in range(n_chunks):
        lo = j * FF_CHUNK
        if j in group_at:
            arrive(group_at[j])
        gate = _dot(h, win_ref[:, lo:lo + FF_CHUNK])
        up = _dot(h, win_ref[:, D_FF + lo:D_FF + lo + FF_CHUNK])
        if j in group_at:
            shift(group_at[j])
        act = gate * jax.nn.sigmoid(gate) * up
        acc = acc + _dot((0.5 * act).astype(BF16), wout_ref[lo:lo + FF_CHUNK, :])
    for g in range(max(n_groups - 2, 0), n_groups):
        for c in copies(g, False):
            c.wait()
    if final:
        acc = _rms(acc, fg_ref[...])
    o_ref[...] = acc


def _ffn_roll(x, g, w_in, w_out, layer, cache, new, prev, batch0, final_g=None, tm=1024):
    n = x.shape[0]
    past, steps = cache.shape[-1], new.shape[-1]
    assert n % tm == 0 and batch0 + n // tm <= cache.shape[0] and N_HEADS // ROLL_GROUP <= D_FF // FF_CHUNK
    row = pl.BlockSpec((tm, D_MODEL), lambda i: (i, 0))
    anywhere = pl.BlockSpec(memory_space=pl.ANY)
    in_specs = [row, _const_spec((1, D_MODEL)), _layer_spec(w_in, layer), _layer_spec(w_out, layer)]
    args = [x, g[layer].reshape(1, D_MODEL), w_in, w_out]
    if final_g is not None:
        in_specs.append(_const_spec((1, D_MODEL)))
        args.append(final_g.reshape(1, D_MODEL))
    in_specs += [pl.BlockSpec((1,) + new.shape[1:], lambda i: (batch0 + i, 0, 0, 0, 0)), anywhere]
    args += [new, cache]
    aliases = {}
    if prev is not None:
        aliases = {len(args): 1}
        in_specs.append(anywhere)
        args.append(prev)
    unit = pltpu.VMEM((2 * ROLL_GROUP, HEAD_DIM, past), F32)
    return pl.pallas_call(
        functools.partial(_ffn_roll_body, final=final_g is not None, aliased=prev is not None, batch0=batch0,
                          past=past, steps=steps),
        grid=(n // tm,),
        in_specs=in_specs,
        out_specs=[row, anywhere],
        out_shape=[jax.ShapeDtypeStruct((n, D_MODEL), F32), jax.ShapeDtypeStruct(cache.shape, F32)],
        scratch_shapes=[unit, unit, pltpu.SemaphoreType.DMA((2,)), pltpu.SemaphoreType.DMA((2,))],
        input_output_aliases=aliases,
        compiler_params=_params(("arbitrary",)),
        name="ffn_roll",
    )(*args)


def _scan_levels(tm):
    levels, n = 0, tm
    while n > 1:
        assert n % SCAN_RADIX == 0
        n //= SCAN_RADIX
        levels += 1
    return levels


def _scan_powers(tm):
    pows = [SCAN_RADIX ** lvl * m for lvl in range(_scan_levels(tm)) for m in range(1, SCAN_RADIX)]
    return pows + [tm]


def _cmul(ar, ai, br, bi):
    return ar * br - ai * bi, ar * bi + ai * br


def _blocked_scan(hre, him, pw_ref, lv_re, lv_im, slab, slot, tm):
    levels = _scan_levels(tm)
    top = SCAN_BASE - 1

    def views(lvl):
        if lvl == 0:
            return (lambda sl: hre[slab, sl, :], lambda sl: him[slab, sl, :],
                    lambda sl, v: hre.__setitem__((slab, sl, slice(None)), v),
                    lambda sl, v: him.__setitem__((slab, sl, slice(None)), v))
        a, b = lv_re[lvl - 1], lv_im[lvl - 1]
        return (lambda sl: a[slot, sl, :], lambda sl: b[slot, sl, :],
                lambda sl, v: a.__setitem__((slot, sl, slice(None)), v),
                lambda sl, v: b.__setitem__((slot, sl, slice(None)), v))

    def power(k):
        return pw_ref[slab, pl.ds(2 * k, 1), :], pw_ref[slab, pl.ds(2 * k + 1, 1), :]

    def elems(n, tau):
        return pl.ds(SCAN_BASE + tau, n, stride=SCAN_RADIX)

    s_re, s_im = hre[slab, pl.ds(top, 1), :], him[slab, pl.ds(top, 1), :]
    for lvl in range(levels):
        get_re, get_im, put_re, put_im = views(lvl)
        n = tm // SCAN_RADIX ** (lvl + 1)
        lr, li = power((SCAN_RADIX - 1) * lvl)
        pr, pi = get_re(elems(n, 0)), get_im(elems(n, 0))
        for tau in range(1, SCAN_RADIX):
            mr, mi = _cmul(lr, li, pr, pi)
            pr, pi = mr + get_re(elems(n, tau)), mi + get_im(elems(n, tau))
            put_re(elems(n, tau), pr)
            put_im(elems(n, tau), pi)
        if lvl + 1 < levels:
            _, _, nput_re, nput_im = views(lvl + 1)
            nput_re(pl.ds(SCAN_BASE, n), pr)
            nput_im(pl.ds(SCAN_BASE, n), pi)
            nput_re(pl.ds(top, 1), s_re)
            nput_im(pl.ds(top, 1), s_im)
    for lvl in reversed(range(levels)):
        get_re, get_im, put_re, put_im = views(lvl)
        n = tm // SCAN_RADIX ** (lvl + 1)
        last = lvl + 1 == levels
        if not last:
            nget_re, nget_im, _, _ = views(lvl + 1)
            put_re(elems(n, SCAN_RADIX - 1), nget_re(pl.ds(SCAN_BASE, n)))
            put_im(elems(n, SCAN_RADIX - 1), nget_im(pl.ds(SCAN_BASE, n)))
        before = pl.ds(top, n, stride=SCAN_RADIX)
        tr, ti = get_re(before), get_im(before)
        for tau in range(SCAN_RADIX if last else SCAN_RADIX - 1):
            k = (SCAN_RADIX - 1) * lvl + tau if tau < SCAN_RADIX - 1 else (SCAN_RADIX - 1) * levels
            mr, mi = _cmul(*power(k), tr, ti)
            put_re(elems(n, tau), get_re(elems(n, tau)) + mr)
            put_im(elems(n, tau), get_im(elems(n, tau)) + mi)
    hre[slab, pl.ds(top, 1), :] = hre[slab, pl.ds(top + tm, 1), :]
    him[slab, pl.ds(top, 1), :] = him[slab, pl.ds(top + tm, 1), :]


def _s5_body(x_ref, h0re_ref, h0im_ref, g_ref, bbd_ref, cbd_ref, pw_ref, d_ref, wglu_ref,
             xo_ref, hlre_ref, hlim_ref, hre, him, *levels_scr, tm, seg):
    nseq = tm // seg
    rows = pl.ds(SCAN_BASE, tm)
    top = SCAN_BASE - 1
    n_lv = len(levels_scr) // 2
    lv_re, lv_im = levels_scr[:n_lv], levels_scr[n_lv:]

    if nseq == 1:
        @pl.when(pl.program_id(1) == 0)
        def _():
            hre[:, pl.ds(top, 1), :] = h0re_ref[0]
            him[:, pl.ds(top, 1), :] = h0im_ref[0]

    x = x_ref[0]
    u = _rms(x, g_ref[...])
    ub = u.astype(BF16)

    def project_in(k):
        bu = _dot(ub[:, k * SSM_BLOCK_IN:(k + 1) * SSM_BLOCK_IN], bbd_ref[k])
        for j in range(SSM_SLABS_PER_BLOCK):
            hre[k * SSM_SLABS_PER_BLOCK + j, rows, :] = bu[:, j * LANES:(j + 1) * LANES]
            him[k * SSM_SLABS_PER_BLOCK + j, rows, :] = bu[:, SSM_BLOCK_HALF + j * LANES:
                                                          SSM_BLOCK_HALF + (j + 1) * LANES]

    def project_out(k):
        sl = range(k * SSM_SLABS_PER_BLOCK, (k + 1) * SSM_SLABS_PER_BLOCK)
        hk = jnp.concatenate([hre[p, rows, :] for p in sl] + [him[p, rows, :] for p in sl], axis=-1)
        return _dot(hk.astype(BF16), cbd_ref[k])

    if nseq == 1:
        ys = []
        project_in(0)
        for k in range(SSM_BLOCKS):
            if k + 1 < SSM_BLOCKS:
                project_in(k + 1)
            for j in range(SSM_SLABS_PER_BLOCK):
                _blocked_scan(hre, him, pw_ref, lv_re, lv_im, k * SSM_SLABS_PER_BLOCK + j, j % SCAN_UNROLL, tm)
            ys.append(project_out(k))
        hlre_ref[0] = hre[:, pl.ds(top, 1), :]
        hlim_ref[0] = him[:, pl.ds(top, 1), :]
    else:
        for k in range(SSM_BLOCKS):
            project_in(k)

        def slab_body(p, carry):
            lr, li = pw_ref[p, pl.ds(0, 1), :], pw_ref[p, pl.ds(1, 1), :]
            hr, hi = h0re_ref[0, p], h0im_ref[0, p]
            for t in range(seg):
                step = pl.ds(SCAN_BASE + t, nseq, stride=seg)
                mr, mi = _cmul(lr, li, hr, hi)
                hr, hi = mr + hre[p, step, :], mi + him[p, step, :]
                hre[p, step, :] = hr
                him[p, step, :] = hi
            hlre_ref[0, p] = hr
            hlim_ref[0, p] = hi
            return carry

        lax.fori_loop(0, SSM_SLABS, slab_body, 0)
        ys = [project_out(k) for k in range(SSM_BLOCKS)]

    y = jnp.concatenate(ys, axis=-1) + d_ref[...] * u
    z = jax.nn.gelu(y, approximate=True).astype(BF16)
    zz = _dot(z, wglu_ref[...])
    xo_ref[0] = x + zz[:, :D_MODEL] * jax.nn.sigmoid(zz[:, D_MODEL:])


def _s5_weights(a_re, a_im, log_dt, b_re, b_im, c_re, c_im, powers):
    a_re, a_im = a_re.astype(F32), a_im.astype(F32)
    dt = jnp.exp(log_dt.astype(F32))[:, None]
    mag = jnp.exp(a_re * dt)
    lre, lim = mag * jnp.cos(a_im * dt), mag * jnp.sin(a_im * dt)
    den = a_re * a_re + a_im * a_im
    fre = ((lre - 1.0) * a_re + lim * a_im) / den
    fim = (lim * a_re - (lre - 1.0) * a_im) / den
    b_re, b_im = b_re.astype(F32), b_im.astype(F32)
    bb = jnp.stack([fre[:, :, None] * b_re - fim[:, :, None] * b_im,
                    fre[:, :, None] * b_im + fim[:, :, None] * b_re])
    eye = jnp.eye(SSM_GROUPS_PER_BLOCK, dtype=F32)
    bb = bb.reshape(2, SSM_BLOCKS, SSM_GROUPS_PER_BLOCK, SSM_STATE, SSM_GROUP)
    bbd = jnp.einsum("rkgph,gq->kghrqp", bb, eye).reshape(SSM_BLOCKS, SSM_BLOCK_IN, SSM_BLOCK_STATE)
    cc = jnp.stack([c_re.astype(F32), -c_im.astype(F32)])
    cc = cc.reshape(2, SSM_BLOCKS, SSM_GROUPS_PER_BLOCK, SSM_GROUP, SSM_STATE)
    cbd = jnp.einsum("rkghp,gq->krgpqh", cc, eye).reshape(SSM_BLOCKS, SSM_BLOCK_STATE, SSM_BLOCK_IN)
    tables = []
    for pows in powers:
        k = jnp.asarray(pows, F32)[:, None, None]
        mag_k = jnp.exp(k * (a_re * dt))
        pw = jnp.stack([mag_k * jnp.cos(k * (a_im * dt)), mag_k * jnp.sin(k * (a_im * dt))], axis=1)
        tables.append(pw.reshape(2 * len(pows), SSM_SLABS, LANES).transpose(1, 0, 2))
    return bbd.astype(BF16), cbd.astype(BF16), tables


def _state_to_slabs(h):
    b = h.shape[0]
    h = h.astype(F32).reshape(b, SSM_SLABS, LANES, 2).transpose(3, 1, 0, 2)
    return h[0], h[1]


def _slabs_to_state(re, im):
    b = re.shape[1]
    h = jnp.stack([re, im], axis=-1).transpose(1, 0, 2, 3)
    return h.reshape(b, N_SSM_GROUPS, SSM_STATE, 2)


def _s5(x, h0re, h0im, g, bbd, cbd, pw, d_skip, w_glu, *, tm, seg):
    nb, rows, _ = x.shape
    nseq = tm // seg
    assert nseq == 1 or rows == tm
    row = pl.BlockSpec((1, tm, D_MODEL), lambda b, i: (b, i, 0))
    st = pl.BlockSpec((1, SSM_SLABS, nseq, LANES), lambda b, i: (b, 0, 0, 0))
    slab = pltpu.VMEM((SSM_SLABS, SCAN_BASE + tm, LANES), F32)
    levels = []
    if nseq == 1:
        levels = [pltpu.VMEM((SCAN_UNROLL, SCAN_BASE + max(tm // SCAN_RADIX ** lvl, SUBLANES), LANES), F32)
                  for lvl in range(1, _scan_levels(tm))] * 2
    st_shape = jax.ShapeDtypeStruct(h0re.shape, F32)
    return pl.pallas_call(
        functools.partial(_s5_body, tm=tm, seg=seg),
        grid=(nb, rows // tm),
        in_specs=[row, st, st, _const_spec((1, D_MODEL)), _const_spec(bbd.shape), _const_spec(cbd.shape),
                  _const_spec(pw.shape), _const_spec((1, D_MODEL)), _const_spec(w_glu.shape)],
        out_specs=[row, st, st],
        out_shape=[jax.ShapeDtypeStruct(x.shape, F32), st_shape, st_shape],
        scratch_shapes=[slab, slab] + levels,
        compiler_params=_params(("parallel", "arbitrary")),
        name="s5_mixer",
    )(x, h0re, h0im, g.reshape(1, D_MODEL), bbd, cbd, pw, d_skip.reshape(1, D_MODEL), w_glu)


def _rope_tables(pos):
    half = ROT_DIM // 2
    inv = ROPE_THETA ** (-jnp.arange(0, ROT_DIM, 2, dtype=F32) / ROT_DIM)
    ang = pos.astype(F32)[:, None] * inv[None, :]
    cos, sin = jnp.cos(ang), jnp.sin(ang)
    n = pos.shape[0]
    pad = HEAD_DIM - ROT_DIM
    c = jnp.concatenate([cos, cos, jnp.ones((n, pad), F32)], axis=-1)
    s1 = jnp.concatenate([-sin, jnp.zeros((n, half + pad), F32)], axis=-1)
    s2 = jnp.concatenate([jnp.zeros((n, half), F32), sin, jnp.zeros((n, pad), F32)], axis=-1)
    rep = LANES // HEAD_DIM
    return tuple(jnp.tile(t, (1, rep)) for t in (c, s1, s2))


def _rope_tile(y, c, s1, s2):
    half = ROT_DIM // 2
    return y * c + pltpu.roll(y, LANES - half, 1) * s1 + pltpu.roll(y, half, 1) * s2


def _proj_body(x_ref, g_ref, w_ref, c_ref, s1_ref, s2_ref, *o_refs, rope_tiles):
    h = _rms(x_ref[...], g_ref[...]).astype(BF16)
    y = _dot(h, w_ref[...])
    c, s1, s2 = c_ref[...], s1_ref[...], s2_ref[...]
    tiles_per_out = y.shape[1] // LANES // len(o_refs)
    for j in range(y.shape[1] // LANES):
        yj = y[:, j * LANES:(j + 1) * LANES]
        if j < rope_tiles:
            yj = _rope_tile(yj, c, s1, s2)
        jo = j % tiles_per_out
        o_refs[j // tiles_per_out][:, jo * LANES:(jo + 1) * LANES] = yj


def _proj(x, g, w, tables, rope_width, n_out, tm=512):
    n = x.shape[0]
    tm = min(tm, n)
    width = w.shape[1] // n_out
    row = pl.BlockSpec((tm, D_MODEL), lambda i: (i, 0))
    tab = pl.BlockSpec((tm, LANES), lambda i: (i, 0))
    return pl.pallas_call(
        functools.partial(_proj_body, rope_tiles=rope_width // LANES),
        grid=(n // tm,),
        in_specs=[row, _const_spec((1, D_MODEL)), _const_spec(w.shape), tab, tab, tab],
        out_specs=[pl.BlockSpec((tm, width), lambda i: (i, 0))] * n_out,
        out_shape=[jax.ShapeDtypeStruct((n, width), F32)] * n_out,
        compiler_params=_params(("parallel",)),
        name="proj_rope",
    )(x, g.reshape(1, D_MODEL), w, *tables)


def _proj_heads_body(x_ref, g_ref, w_ref, c_ref, s1_ref, s2_ref, *refs, n_mat, tm, cache_from):
    n_t = n_mat if cache_from is not None else 0
    outs = refs[:N_BRANCHES * n_mat]
    t_refs = refs[N_BRANCHES * n_mat:N_BRANCHES * n_mat + n_t]
    slab = refs[-1]
    h = _rms(x_ref[0], g_ref[...]).astype(BF16)
    y = _dot(h, w_ref[...])
    c, s1, s2 = c_ref[...], s1_ref[...], s2_ref[...]
    for m in range(n_mat):
        tiles = []
        for br, d in enumerate(DILATIONS):
            lo = m * ATTN_WIDTH + br * MERGED_WIDTH
            part = [y[:, lo + j * LANES:lo + (j + 1) * LANES] for j in range(BRANCH_TILES)]
            if m == 0:
                part = [_rope_tile(t, c, s1, s2) for t in part]
            tiles += part
            out = outs[N_BRANCHES * m + br]
            if d == 1:
                for j in range(BRANCH_TILES):
                    out[0, 0, :, j * LANES:(j + 1) * LANES] = part[j].astype(BF16)
                continue
            for j in range(BRANCH_TILES):
                slab[(br - 1) * BRANCH_TILES + j] = part[j]
            for r in range(d):
                for j in range(BRANCH_TILES):
                    rows = slab[(br - 1) * BRANCH_TILES + j, pl.ds(r, tm // d, stride=d), :]
                    out[0, r, :, j * LANES:(j + 1) * LANES] = rows.astype(BF16)
        if cache_from is not None:
            @pl.when(pl.program_id(1) >= cache_from)
            def _(m=m, tiles=tiles):
                t_refs[m][0] = jnp.concatenate(tiles, axis=-1).T


def _proj_heads(x, g, w, tables, n_mat, keep=None, tm=512):
    b, l, _ = x.shape
    assert w.shape[1] == n_mat * ATTN_WIDTH and tm % (DILATIONS[-1] * 2 * SUBLANES) == 0
    cache_from = None if keep is None else (l - keep) // tm
    row = pl.BlockSpec((1, tm, D_MODEL), lambda bi, i: (bi, i, 0))
    tab = pl.BlockSpec((tm, LANES), lambda bi, i: (i, 0))
    out_specs, out_shape = [], []
    for _ in range(n_mat):
        for d in DILATIONS:
            out_specs.append(pl.BlockSpec((1, d, tm // d, MERGED_WIDTH), lambda bi, i: (bi, 0, i, 0)))
            out_shape.append(jax.ShapeDtypeStruct((b, d, l // d, MERGED_WIDTH), BF16))
    if keep is not None:
        for _ in range(n_mat):
            out_specs.append(pl.BlockSpec((1, ATTN_WIDTH, tm),
                                          lambda bi, i: (bi, 0, jnp.maximum(i - cache_from, 0))))
            out_shape.append(jax.ShapeDtypeStruct((b, ATTN_WIDTH, keep), F32))
    return pl.pallas_call(
        functools.partial(_proj_heads_body, n_mat=n_mat, tm=tm, cache_from=cache_from),
        grid=(b, l // tm),
        in_specs=[row, _const_spec((1, D_MODEL)), _const_spec(w.shape), tab, tab, tab],
        out_specs=out_specs,
        out_shape=out_shape,
        scratch_shapes=[pltpu.VMEM(((N_BRANCHES - 1) * BRANCH_TILES, tm, LANES), F32)],
        compiler_params=_params(("parallel", "arbitrary")),
        name="proj_heads",
    )(x, g.reshape(1, D_MODEL), w, *tables)


def _swa_body(q_ref, kp_ref, kc_ref, vp_ref, vc_ref, o_ref, lse_ref, *, steps_per_seq):
    n = pl.program_id(1)
    k = jnp.concatenate([kp_ref[0], kc_ref[0]], axis=0)
    v = jnp.concatenate([vp_ref[0], vc_ref[0]], axis=0)
    qi = lax.broadcasted_iota(jnp.int32, (SWA_BLOCK, 2 * SWA_BLOCK), 0)
    kj = lax.broadcasted_iota(jnp.int32, (SWA_BLOCK, 2 * SWA_BLOCK), 1)
    band = (kj >= qi) & (kj <= qi + SWA_BLOCK)
    has_prev = (n % steps_per_seq) != 0
    first = band & ((kj >= SWA_BLOCK) | has_prev)
    low = lax.broadcasted_iota(jnp.int32, (SWA_BLOCK, LANES), 1) < HEAD_DIM
    zero = jnp.zeros((SWA_BLOCK, LANES), BF16)
    for j in range(SWA_SUB):
        rows = slice(j * SWA_BLOCK, (j + 1) * SWA_BLOCK)
        keys = slice(j * SWA_BLOCK, (j + 2) * SWA_BLOCK)
        valid = first if j == 0 else band
        for t in range(BRANCH_TILES):
            cols = slice(t * LANES, (t + 1) * LANES)
            q = q_ref[0, rows, cols] * (HEAD_DIM ** -0.5)
            kt, vt = k[keys, cols], v[keys, cols]
            outs, lses = [], []
            for qh in (jnp.where(low, q, zero), jnp.where(low, zero, q)):
                s = jnp.where(valid, _dot_nt(qh, kt), NEG_INF)
                m = jnp.max(s, axis=-1, keepdims=True)
                p = jnp.exp(s - m)
                l = jnp.sum(p, axis=-1, keepdims=True)
                outs.append(_dot(p.astype(BF16), vt) / l)
                lses.append(jnp.broadcast_to(m + jnp.log(l), (SWA_BLOCK, LANES)))
            o_ref[0, rows, cols] = jnp.where(low, outs[0], outs[1])
            lse_ref[0, rows, cols] = jnp.where(low, lses[0], lses[1])


def _swa(q, k, v, dilation):
    b, l, w = q.shape
    step = SWA_SUB * SWA_BLOCK
    assert (l // dilation) % step == 0
    cur = pl.BlockSpec((1, step, w), lambda bi, n: (bi, n, 0))
    prev = pl.BlockSpec((1, SWA_BLOCK, w), lambda bi, n: (bi, jnp.maximum(SWA_SUB * n - 1, 0), 0))
    out = jax.ShapeDtypeStruct((b, l, w), F32)
    return pl.pallas_call(
        functools.partial(_swa_body, steps_per_seq=l // dilation // step),
        grid=(b, l // step),
        in_specs=[cur, prev, cur, prev, cur],
        out_specs=[cur, cur],
        out_shape=[out, out],
        compiler_params=_params(("parallel", "arbitrary")),
        name="swa",
    )(q, k, k, v, v)


def _merge_body(x_ref, *refs, tm):
    pairs, wo_ref, out_ref, slab = refs[:2 * N_BRANCHES], refs[2 * N_BRANCHES], refs[-2], refs[-1]
    for a, ref in enumerate(pairs[2:]):
        d = DILATIONS[1 + a // 2]
        for r in range(d):
            for j in range(BRANCH_TILES):
                slab[a * BRANCH_TILES + j, pl.ds(r, tm // d, stride=d), :] = ref[0, r, :, j * LANES:(j + 1) * LANES]
    tiles = []
    for j in range(BRANCH_TILES):
        cols = slice(j * LANES, (j + 1) * LANES)
        o0, l0 = pairs[0][0, 0, :, cols], pairs[1][0, 0, :, cols]
        o1, l1, o2, l2 = (slab[a * BRANCH_TILES + j] for a in range(4))
        m = jnp.maximum(jnp.maximum(l0, l1), l2)
        e0, e1, e2 = jnp.exp(l0 - m), jnp.exp(l1 - m), jnp.exp(l2 - m)
        tiles.append(((e0 * o0 + e1 * o1 + e2 * o2) / (e0 + e1 + e2)).astype(BF16))
    out_ref[0] = x_ref[0] + _dot(jnp.concatenate(tiles, axis=-1), wo_ref[...])


def _merge(x, os, ls, w_o, tm=512):
    b, l, _ = x.shape
    row = pl.BlockSpec((1, tm, D_MODEL), lambda bi, i: (bi, i, 0))
    specs, args = [], []
    for d, o, lse in zip(DILATIONS, os, ls):
        spec = pl.BlockSpec((1, d, tm // d, MERGED_WIDTH), lambda bi, i: (bi, 0, i, 0))
        specs += [spec, spec]
        args += [o, lse]
    return pl.pallas_call(
        functools.partial(_merge_body, tm=tm),
        grid=(b, l // tm),
        in_specs=[row] + specs + [_const_spec(w_o.shape)],
        out_specs=row,
        out_shape=jax.ShapeDtypeStruct(x.shape, F32),
        scratch_shapes=[pltpu.VMEM((2 * (N_BRANCHES - 1) * BRANCH_TILES, tm, LANES), F32)],
        compiler_params=_params(("parallel", "parallel")),
        name="merge_wo",
    )(x, *args, w_o)


def _decode_body(q_ref, kn_ref, vn_ref, k0_ref, k1_ref, k2_ref, v0_ref, v1_ref, v2_ref, o_ref, *, steps):
    scale = HEAD_DIM ** -0.5
    qi = lax.broadcasted_iota(jnp.int32, (steps, steps), 0)
    kj = lax.broadcasted_iota(jnp.int32, (steps, steps), 1)
    k_refs, v_refs = (k0_ref, k1_ref, k2_ref), (v0_ref, v1_ref, v2_ref)

    def slot_body(sl, carry):
        scores, values = [], []
        for br, d in enumerate(DILATIONS):
            w = WINDOWS[br]
            q = q_ref[0, sl, br].astype(BF16)
            back = (w + lax.broadcasted_iota(jnp.int32, (steps, w), 0)
                    - lax.broadcasted_iota(jnp.int32, (steps, w), 1))
            ok = ((back & (d - 1)) == 0) & (back <= w)
            scores.append(jnp.where(ok, _dot(q, k_refs[br][0, 0, sl].astype(BF16)) * scale, NEG_INF))
            values.append((v_refs[br][0, 0, sl].astype(BF16), True))
            ok = (kj <= qi) & (((qi - kj) & (d - 1)) == 0)
            scores.append(jnp.where(ok, _dot_nt(q, kn_ref[0, sl, br].astype(BF16)) * scale, NEG_INF))
            values.append((vn_ref[0, sl, br].astype(BF16), False))
        m = functools.reduce(jnp.maximum, [jnp.max(s, axis=-1, keepdims=True) for s in scores])
        num = jnp.zeros((steps, HEAD_DIM), F32)
        den = jnp.zeros((steps, 1), F32)
        for s, (v, transposed) in zip(scores, values):
            p = jnp.exp(s - m)
            den = den + jnp.sum(p, axis=-1, keepdims=True)
            num = num + (_dot_nt(p.astype(BF16), v) if transposed else _dot(p.astype(BF16), v))
        o_ref[0, sl] = num / den
        return carry

    lax.fori_loop(0, HEADS_PER_BRANCH, slot_body, 0)


def _slots(t):
    b, steps = t.shape[0], t.shape[1]
    return t.reshape(b, steps, N_BRANCHES, HEADS_PER_BRANCH, HEAD_DIM).transpose(0, 3, 2, 1, 4)


def _slots_t(t):
    b, steps = t.shape[0], t.shape[1]
    return t.reshape(b, steps, N_BRANCHES, HEADS_PER_BRANCH, HEAD_DIM).transpose(0, 2, 3, 4, 1)


def _cache_t(c):
    b, past = c.shape[0], c.shape[1]
    return c.transpose(0, 2, 3, 1).reshape(b, N_BRANCHES, HEADS_PER_BRANCH, HEAD_DIM, past)


def _cache_from_t(c):
    b, past = c.shape[0], c.shape[-1]
    return c.reshape(b, N_HEADS, HEAD_DIM, past).transpose(0, 3, 1, 2)


def _decode(q, kn, vn, cache_k, cache_v):
    b, steps = q.shape[0], q.shape[1]
    past = cache_k.shape[-1]
    assert past == MAX_WINDOW
    hb = HEADS_PER_BRANCH
    new = pl.BlockSpec((1, hb, N_BRANCHES, steps, HEAD_DIM), lambda i: (i, 0, 0, 0, 0))
    windows = [pl.BlockSpec((1, 1, hb, HEAD_DIM, w), functools.partial(lambda br, last, i: (i, br, 0, 0, last),
                                                                      br, past // w - 1))
               for br, w in enumerate(WINDOWS)]
    o = pl.pallas_call(
        functools.partial(_decode_body, steps=steps),
        grid=(b,),
        in_specs=[new, new, new] + windows + windows,
        out_specs=pl.BlockSpec((1, hb, steps, HEAD_DIM), lambda i: (i, 0, 0, 0)),
        out_shape=jax.ShapeDtypeStruct((b, hb, steps, HEAD_DIM), F32),
        compiler_params=_params(("parallel",)),
        name="decode_attn",
    )(_slots(q), _slots(kn), _slots(vn), *([cache_k] * N_BRANCHES), *([cache_v] * N_BRANCHES))
    return o.transpose(0, 2, 1, 3).reshape(b * steps, MERGED_WIDTH)


def _addproj_body(x_ref, o_ref, wo_ref, out_ref):
    out_ref[...] = x_ref[...] + _dot(o_ref[...].astype(BF16), wo_ref[...])


def _addproj(x, o, w_o):
    n = x.shape[0]
    return pl.pallas_call(
        _addproj_body,
        grid=(1,),
        in_specs=[_const_spec(x.shape), _const_spec(o.shape), _const_spec(w_o.shape)],
        out_specs=pl.BlockSpec(x.shape, lambda i: (0, 0)),
        out_shape=jax.ShapeDtypeStruct((n, D_MODEL), F32),
        compiler_params=_params(("arbitrary",)),
        name="add_wo",
    )(x, o, w_o)


def kernel(x_prompt, x_sample, state_ssm, cache_k, cache_v, ffn_norm1, ffn_w1_in, ffn_w1_out, ffn_norm2,
           ffn_w2_in, ffn_w2_out, mix_norm, ssm_a_re, ssm_a_im, ssm_log_dt, ssm_b_re, ssm_b_im, ssm_c_re,
           ssm_c_im, ssm_d, ssm_w_glu, kv_norm, w_kv, w_q, w_o, final_norm):
    bp, sp, _ = x_prompt.shape
    bs, ss, _ = x_sample.shape
    tm_scan = 256

    w1_in, w1_out, w2_in, w2_out = (w.astype(BF16) for w in (ffn_w1_in, ffn_w1_out, ffn_w2_in, ffn_w2_out))
    w_glu, w_kv_b, w_q_b, w_o_b = (ssm_w_glu[0].astype(BF16), w_kv.astype(BF16), w_q[0].astype(BF16),
                                   w_o[0].astype(BF16))
    bbd, cbd, (pw_p, pw_s) = _s5_weights(ssm_a_re[0], ssm_a_im[0], ssm_log_dt[0], ssm_b_re[0], ssm_b_im[0],
                                         ssm_c_re[0], ssm_c_im[0], (_scan_powers(tm_scan), [1]))

    def mixer(x, h0, pw, tm, seg):
        return _s5(x, *h0, mix_norm[0], bbd, cbd, pw, ssm_d[0], w_glu, tm=tm, seg=seg)

    pos_s = jnp.tile(PAST_LEN + jnp.arange(ss, dtype=jnp.int32), bs)
    tab_s = _rope_tables(pos_s)
    h0_s = tuple(t[None] for t in _state_to_slabs(state_ssm[0]))
    xs = _ffn(x_sample.reshape(bs * ss, D_MODEL), ffn_norm1, w1_in, w1_out, 0)
    xs, hl_re, hl_im = mixer(xs.reshape(1, bs * ss, D_MODEL), h0_s, pw_s, bs * ss, ss)
    xs = _ffn(xs.reshape(bs * ss, D_MODEL), ffn_norm2, w2_in, w2_out, 0)
    state_sample = _slabs_to_state(hl_re[0], hl_im[0])[None]
    k_s, v_s = (t.reshape(bs, ss, N_HEADS, HEAD_DIM) for t in _proj(xs, kv_norm, w_kv_b, tab_s, ATTN_WIDTH, 2))
    ck_t, cv_t, kn_t, vn_t = _cache_t(cache_k), _cache_t(cache_v), _slots_t(k_s), _slots_t(v_s)
    half = bs // 2
    tm_roll = bp * sp // half

    tab_p = _rope_tables(jnp.arange(sp, dtype=jnp.int32))
    zeros = jnp.zeros((bp, SSM_SLABS, 1, LANES), F32)
    xp, k_roll = _ffn_roll(x_prompt.reshape(bp * sp, D_MODEL), ffn_norm1, w1_in, w1_out, 0, ck_t, kn_t, None, 0,
                           tm=tm_roll)
    xp, hl_re, hl_im = mixer(xp.reshape(bp, sp, D_MODEL), (zeros, zeros), pw_p, tm_scan, tm_scan)
    xp, k_roll = _ffn_roll(xp.reshape(bp * sp, D_MODEL), ffn_norm2, w2_in, w2_out, 0, ck_t, kn_t, k_roll, half,
                           tm=tm_roll)
    cache_k_sample = _cache_from_t(k_roll)
    state_prompt = _slabs_to_state(hl_re[:, :, 0].transpose(1, 0, 2), hl_im[:, :, 0].transpose(1, 0, 2))[None]
    keep_p = min(MAX_WINDOW, sp)
    *kv_d, kt_p, vt_p = _proj_heads(xp.reshape(bp, sp, D_MODEL), kv_norm, w_kv_b, tab_p, 2, keep=keep_p)
    cache_k_prompt = kt_p.reshape(bp, N_HEADS, HEAD_DIM, keep_p).transpose(0, 3, 1, 2)
    cache_v_prompt = vt_p.reshape(bp, N_HEADS, HEAD_DIM, keep_p).transpose(0, 3, 1, 2)
    xp, v_roll = _ffn_roll(xp, ffn_norm1, w1_in, w1_out, 1, cv_t, vn_t, None, 0, tm=tm_roll)
    xp = xp.reshape(bp, sp, D_MODEL)
    q_d = _proj_heads(xp, mix_norm[1], w_q_b, tab_p, 1)
    outs, lses = [], []
    for br, d in enumerate(DILATIONS):
        flat = lambda t: t.reshape(bp, sp, MERGED_WIDTH)
        o, lse = _swa(flat(q_d[br]), flat(kv_d[br]), flat(kv_d[N_BRANCHES + br]), d)
        outs.append(o.reshape(bp, d, sp // d, MERGED_WIDTH))
        lses.append(lse.reshape(bp, d, sp // d, MERGED_WIDTH))
    xp = _merge(xp, outs, lses, w_o_b).reshape(bp * sp, D_MODEL)
    y_prompt, v_roll = _ffn_roll(xp, ffn_norm2, w2_in, w2_out, 1, cv_t, vn_t, v_roll, half, final_g=final_norm,
                                 tm=tm_roll)
    y_prompt = y_prompt.reshape(bp, sp, D_MODEL)
    cache_v_sample = _cache_from_t(v_roll)

    xs = _ffn(xs, ffn_norm1, w1_in, w1_out, 1)
    q_s = _proj(xs, mix_norm[1], w_q_b, tab_s, ATTN_WIDTH, 1)[0].reshape(bs, ss, N_HEADS, HEAD_DIM)
    xs = _addproj(xs, _decode(q_s, k_s, v_s, ck_t, cv_t), w_o_b)
    y_sample = _ffn(xs, ffn_norm2, w2_in, w2_out, 1, final_g=final_norm).reshape(bs, ss, D_MODEL)

    return (y_prompt, y_sample, state_prompt, cache_k_prompt, cache_v_prompt,
            state_sample, cache_k_sample, cache_v_sample)
```

```python
import functools

import jax
import jax.numpy as jnp
from jax import lax
from jax.experimental import pallas as pl
from jax.experimental.pallas import tpu as pltpu

F32 = jnp.float32
BF16 = jnp.bfloat16

D_MODEL = 1024
D_FF = 2816
SSM_GROUP = 16
N_SSM_GROUPS = D_MODEL // SSM_GROUP
SSM_STATE = 64
HEAD_DIM = 64
WINDOWS = (128, 512, 2048)
DILATIONS = (1, 4, 16)
N_BRANCHES = 3
HEADS_PER_BRANCH = 8
N_HEADS = N_BRANCHES * HEADS_PER_BRANCH
ATTN_WIDTH = N_HEADS * HEAD_DIM
MERGED_WIDTH = HEADS_PER_BRANCH * HEAD_DIM
MAX_WINDOW = max(WINDOWS)
ROT_DIM = HEAD_DIM // 4
ROPE_THETA = 500000.0
PAST_LEN = 16384
RMS_EPS = 1e-6
NEG_INF = -1e30

LANES = 128
SUBLANES = 8
BRANCH_TILES = MERGED_WIDTH // LANES

SSM_BLOCKS = 4
SSM_GROUPS_PER_BLOCK = N_SSM_GROUPS // SSM_BLOCKS
SSM_BLOCK_IN = SSM_GROUPS_PER_BLOCK * SSM_GROUP
SSM_BLOCK_HALF = SSM_GROUPS_PER_BLOCK * SSM_STATE
SSM_BLOCK_STATE = 2 * SSM_BLOCK_HALF
SSM_SLABS = N_SSM_GROUPS * SSM_STATE // LANES
SSM_SLABS_PER_BLOCK = SSM_SLABS // SSM_BLOCKS
SCAN_RADIX = 4
SCAN_BASE = SUBLANES
SCAN_UNROLL = 4

FF_CHUNK = 256
SWA_BLOCK = 128
SWA_SUB = 4
ROLL_GROUP = 4

VMEM_LIMIT = 56 * 1024 * 1024


def _params(semantics):
    return pltpu.CompilerParams(dimension_semantics=semantics, vmem_limit_bytes=VMEM_LIMIT)


def _const_spec(shape):
    nd = len(shape)
    return pl.BlockSpec(shape, lambda *_: (0,) * nd, pipeline_mode=pl.Buffered(1))


def _rms(x, g):
    return x * lax.rsqrt(jnp.mean(x * x, axis=-1, keepdims=True) + RMS_EPS) * g


def _dot(a, b):
    return jnp.dot(a, b, preferred_element_type=F32)


def _dot_nt(a, b):
    return lax.dot_general(a, b, (((1,), (1,)), ((), ())), preferred_element_type=F32)


def _ffn_body(x_ref, g_ref, win_ref, wout_ref, *rest, final):
    if final:
        fg_ref, o_ref = rest
    else:
        (o_ref,) = rest
    x = x_ref[...]
    h = _rms(x, g_ref[...]).astype(BF16)
    acc = x
    for j in range(D_FF // FF_CHUNK):
        lo = j * FF_CHUNK
        gate = _dot(h, win_ref[:, lo:lo + FF_CHUNK])
        up = _dot(h, win_ref[:, D_FF + lo:D_FF + lo + FF_CHUNK])
        act = gate * jax.nn.sigmoid(gate) * up
        acc = acc + _dot((0.5 * act).astype(BF16), wout_ref[lo:lo + FF_CHUNK, :])
    if final:
        acc = _rms(acc, fg_ref[...])
    o_ref[...] = acc


def _layer_spec(w, layer):
    return pl.BlockSpec((None,) + w.shape[1:], lambda *_: (layer, 0, 0), pipeline_mode=pl.Buffered(1))


def _ffn(x, g, w_in, w_out, layer, final_g=None, tm=1024):
    n = x.shape[0]
    tm = min(tm, n)
    row = pl.BlockSpec((tm, D_MODEL), lambda i: (i, 0))
    in_specs = [row, _const_spec((1, D_MODEL)), _layer_spec(w_in, layer), _layer_spec(w_out, layer)]
    args = [x, g[layer].reshape(1, D_MODEL), w_in, w_out]
    if final_g is not None:
        in_specs.append(_const_spec((1, D_MODEL)))
        args.append(final_g.reshape(1, D_MODEL))
    return pl.pallas_call(
        functools.partial(_ffn_body, final=final_g is not None),
        grid=(n // tm,),
        in_specs=in_specs,
        out_specs=row,
        out_shape=jax.ShapeDtypeStruct((n, D_MODEL), F32),
        compiler_params=_params(("parallel",)),
        name="ffn",
    )(*args)


def _ffn_roll_body(x_ref, g_ref, win_ref, wout_ref, *rest, final, aliased, batch0, past, steps):
    rest = list(rest)
    fg_ref = rest.pop(0) if final else None
    new_ref, cache_hbm = rest.pop(0), rest.pop(0)
    if aliased:
        rest.pop(0)
    o_ref, rolled_hbm, in_buf, out_buf, in_sem, out_sem = rest
    b = batch0 + pl.program_id(0)
    n_groups = N_HEADS // ROLL_GROUP
    n_chunks = D_FF // FF_CHUNK
    lane = lax.broadcasted_iota(jnp.int32, (HEAD_DIM, LANES), 1)
    tiles = past // LANES

    def copy(g, inward, seq=b):
        br, sl = divmod(g * ROLL_GROUP, HEADS_PER_BRANCH)
        half = pl.ds((g % 2) * ROLL_GROUP, ROLL_GROUP)
        if inward:
            return pltpu.make_async_copy(cache_hbm.at[seq, br, pl.ds(sl, ROLL_GROUP)], in_buf.at[half],
                                         in_sem.at[g % 2])
        return pltpu.make_async_copy(out_buf.at[half], rolled_hbm.at[seq, br, pl.ds(sl, ROLL_GROUP)],
                                     out_sem.at[g % 2])

    def arrive(g):
        copy(g, True).wait()
        if g >= 2:
            copy(g - 2, False).wait()

    def shift(g):
        for k in range(ROLL_GROUP):
            br, sl = divmod(g * ROLL_GROUP + k, HEADS_PER_BRANCH)
            src, dst = in_buf.at[(g % 2) * ROLL_GROUP + k], out_buf.at[(g % 2) * ROLL_GROUP + k]
            cur = pltpu.roll(src[:, 0:LANES], LANES - steps, 1)
            for t in range(tiles):
                if t + 1 < tiles:
                    nxt = pltpu.roll(src[:, (t + 1) * LANES:(t + 2) * LANES], LANES - steps, 1)
                else:
                    nxt = jnp.concatenate([jnp.zeros((HEAD_DIM, LANES - steps), F32), new_ref[0, br, sl]], axis=1)
                dst[:, t * LANES:(t + 1) * LANES] = jnp.where(lane < LANES - steps, cur, nxt)
                cur = nxt
        copy(g, False).start()
        if g + 2 < n_groups:
            copy(g + 2, True).start()
        else:
            @pl.when(pl.program_id(0) + 1 < pl.num_programs(0))
            def _():
                copy(g + 2 - n_groups, True, b + 1).start()

    @pl.when(pl.program_id(0) == 0)
    def _():
        copy(0, True).start()
        copy(1, True).start()

    x = x_ref[...]
    h = _rms(x, g_ref[...]).astype(BF16)
    acc = x
    group_at = {1 + g * (n_chunks - 1) // n_groups: g for g in range(n_groups)}
    for j in range(n_chunks):
        lo = j * FF_CHUNK
        if j in group_at:
            arrive(group_at[j])
        gate = _dot(h, win_ref[:, lo:lo + FF_CHUNK])
        up = _dot(h, win_ref[:, D_FF + lo:D_FF + lo + FF_CHUNK])
        if j in group_at:
            shift(group_at[j])
        act = gate * jax.nn.sigmoid(gate) * up
        acc = acc + _dot((0.5 * act).astype(BF16), wout_ref[lo:lo + FF_CHUNK, :])
    for g in range(n_groups - 2, n_groups):
        copy(g, False).wait()
    if final:
        acc = _rms(acc, fg_ref[...])
    o_ref[...] = acc


def _ffn_roll(x, g, w_in, w_out, layer, cache, new, prev, batch0, final_g=None, tm=1024):
    n = x.shape[0]
    past, steps = cache.shape[-1], new.shape[-1]
    n_groups = N_HEADS // ROLL_GROUP
    assert n % tm == 0 and batch0 + n // tm <= cache.shape[0]
    assert HEADS_PER_BRANCH % ROLL_GROUP == 0 and n_groups % 2 == 0 and n_groups < D_FF // FF_CHUNK
    row = pl.BlockSpec((tm, D_MODEL), lambda i: (i, 0))
    anywhere = pl.BlockSpec(memory_space=pl.ANY)
    in_specs = [row, _const_spec((1, D_MODEL)), _layer_spec(w_in, layer), _layer_spec(w_out, layer)]
    args = [x, g[layer].reshape(1, D_MODEL), w_in, w_out]
    if final_g is not None:
        in_specs.append(_const_spec((1, D_MODEL)))
        args.append(final_g.reshape(1, D_MODEL))
    in_specs += [pl.BlockSpec((1,) + new.shape[1:], lambda i: (batch0 + i, 0, 0, 0, 0)), anywhere]
    args += [new, cache]
    aliases = {}
    if prev is not None:
        aliases = {len(args): 1}
        in_specs.append(anywhere)
        args.append(prev)
    unit = pltpu.VMEM((2 * ROLL_GROUP, HEAD_DIM, past), F32)
    return pl.pallas_call(
        functools.partial(_ffn_roll_body, final=final_g is not None, aliased=prev is not None, batch0=batch0,
                          past=past, steps=steps),
        grid=(n // tm,),
        in_specs=in_specs,
        out_specs=[row, anywhere],
        out_shape=[jax.ShapeDtypeStruct((n, D_MODEL), F32), jax.ShapeDtypeStruct(cache.shape, F32)],
        scratch_shapes=[unit, unit, pltpu.SemaphoreType.DMA((2,)), pltpu.SemaphoreType.DMA((2,))],
        input_output_aliases=aliases,
        compiler_params=_params(("arbitrary",)),
        name="ffn_roll",
    )(*args)


def _scan_levels(tm):
    levels, n = 0, tm
    while n > 1:
        assert n % SCAN_RADIX == 0
        n //= SCAN_RADIX
        levels += 1
    return levels


def _scan_powers(tm):
    pows = [SCAN_RADIX ** lvl * m for lvl in range(_scan_levels(tm)) for m in range(1, SCAN_RADIX)]
    return pows + [tm]


def _cmul(ar, ai, br, bi):
    return ar * br - ai * bi, ar * bi + ai * br


def _blocked_scan(hre, him, pw_ref, lv_re, lv_im, slab, slot, tm):
    levels = _scan_levels(tm)
    top = SCAN_BASE - 1

    def views(lvl):
        if lvl == 0:
            return (lambda sl: hre[slab, sl, :], lambda sl: him[slab, sl, :],
                    lambda sl, v: hre.__setitem__((slab, sl, slice(None)), v),
                    lambda sl, v: him.__setitem__((slab, sl, slice(None)), v))
        a, b = lv_re[lvl - 1], lv_im[lvl - 1]
        return (lambda sl: a[slot, sl, :], lambda sl: b[slot, sl, :],
                lambda sl, v: a.__setitem__((slot, sl, slice(None)), v),
                lambda sl, v: b.__setitem__((slot, sl, slice(None)), v))

    def power(k):
        return pw_ref[slab, pl.ds(2 * k, 1), :], pw_ref[slab, pl.ds(2 * k + 1, 1), :]

    def elems(n, tau):
        return pl.ds(SCAN_BASE + tau, n, stride=SCAN_RADIX)

    s_re, s_im = hre[slab, pl.ds(top, 1), :], him[slab, pl.ds(top, 1), :]
    for lvl in range(levels):
        get_re, get_im, put_re, put_im = views(lvl)
        n = tm // SCAN_RADIX ** (lvl + 1)
        lr, li = power((SCAN_RADIX - 1) * lvl)
        pr, pi = get_re(elems(n, 0)), get_im(elems(n, 0))
        for tau in range(1, SCAN_RADIX):
            mr, mi = _cmul(lr, li, pr, pi)
            pr, pi = mr + get_re(elems(n, tau)), mi + get_im(elems(n, tau))
            put_re(elems(n, tau), pr)
            put_im(elems(n, tau), pi)
        if lvl + 1 < levels:
            _, _, nput_re, nput_im = views(lvl + 1)
            nput_re(pl.ds(SCAN_BASE, n), pr)
            nput_im(pl.ds(SCAN_BASE, n), pi)
            nput_re(pl.ds(top, 1), s_re)
            nput_im(pl.ds(top, 1), s_im)
    for lvl in reversed(range(levels)):
        get_re, get_im, put_re, put_im = views(lvl)
        n = tm // SCAN_RADIX ** (lvl + 1)
        last = lvl + 1 == levels
        if not last:
            nget_re, nget_im, _, _ = views(lvl + 1)
            put_re(elems(n, SCAN_RADIX - 1), nget_re(pl.ds(SCAN_BASE, n)))
            put_im(elems(n, SCAN_RADIX - 1), nget_im(pl.ds(SCAN_BASE, n)))
        before = pl.ds(top, n, stride=SCAN_RADIX)
        tr, ti = get_re(before), get_im(before)
        for tau in range(SCAN_RADIX if last else SCAN_RADIX - 1):
            k = (SCAN_RADIX - 1) * lvl + tau if tau < SCAN_RADIX - 1 else (SCAN_RADIX - 1) * levels
            mr, mi = _cmul(*power(k), tr, ti)
            put_re(elems(n, tau), get_re(elems(n, tau)) + mr)
            put_im(elems(n, tau), get_im(elems(n, tau)) + mi)
    hre[slab, pl.ds(top, 1), :] = hre[slab, pl.ds(top + tm, 1), :]
    him[slab, pl.ds(top, 1), :] = him[slab, pl.ds(top + tm, 1), :]


def _s5_body(x_ref, h0re_ref, h0im_ref, g_ref, bbd_ref, cbd_ref, pw_ref, d_ref, wglu_ref,
             xo_ref, hlre_ref, hlim_ref, hre, him, *levels_scr, tm, seg):
    nseq = tm // seg
    rows = pl.ds(SCAN_BASE, tm)
    top = SCAN_BASE - 1
    n_lv = len(levels_scr) // 2
    lv_re, lv_im = levels_scr[:n_lv], levels_scr[n_lv:]

    if nseq == 1:
        @pl.when(pl.program_id(1) == 0)
        def _():
            hre[:, pl.ds(top, 1), :] = h0re_ref[0]
            him[:, pl.ds(top, 1), :] = h0im_ref[0]

    x = x_ref[0]
    u = _rms(x, g_ref[...])
    ub = u.astype(BF16)

    def project_in(k):
        bu = _dot(ub[:, k * SSM_BLOCK_IN:(k + 1) * SSM_BLOCK_IN], bbd_ref[k])
        for j in range(SSM_SLABS_PER_BLOCK):
            hre[k * SSM_SLABS_PER_BLOCK + j, rows, :] = bu[:, j * LANES:(j + 1) * LANES]
            him[k * SSM_SLABS_PER_BLOCK + j, rows, :] = bu[:, SSM_BLOCK_HALF + j * LANES:
                                                          SSM_BLOCK_HALF + (j + 1) * LANES]

    def project_out(k):
        sl = range(k * SSM_SLABS_PER_BLOCK, (k + 1) * SSM_SLABS_PER_BLOCK)
        hk = jnp.concatenate([hre[p, rows, :] for p in sl] + [him[p, rows, :] for p in sl], axis=-1)
        return _dot(hk.astype(BF16), cbd_ref[k])

    if nseq == 1:
        ys = []
        project_in(0)
        for k in range(SSM_BLOCKS):
            if k + 1 < SSM_BLOCKS:
                project_in(k + 1)
            for j in range(SSM_SLABS_PER_BLOCK):
                _blocked_scan(hre, him, pw_ref, lv_re, lv_im, k * SSM_SLABS_PER_BLOCK + j, j % SCAN_UNROLL, tm)
            ys.append(project_out(k))
        hlre_ref[0] = hre[:, pl.ds(top, 1), :]
        hlim_ref[0] = him[:, pl.ds(top, 1), :]
    else:
        for k in range(SSM_BLOCKS):
            project_in(k)

        def slab_body(p, carry):
            lr, li = pw_ref[p, pl.ds(0, 1), :], pw_ref[p, pl.ds(1, 1), :]
            hr, hi = h0re_ref[0, p], h0im_ref[0, p]
            for t in range(seg):
                step = pl.ds(SCAN_BASE + t, nseq, stride=seg)
                mr, mi = _cmul(lr, li, hr, hi)
                hr, hi = mr + hre[p, step, :], mi + him[p, step, :]
                hre[p, step, :] = hr
                him[p, step, :] = hi
            hlre_ref[0, p] = hr
            hlim_ref[0, p] = hi
            return carry

        lax.fori_loop(0, SSM_SLABS, slab_body, 0)
        ys = [project_out(k) for k in range(SSM_BLOCKS)]

    y = jnp.concatenate(ys, axis=-1) + d_ref[...] * u
    z = jax.nn.gelu(y, approximate=True).astype(BF16)
    zz = _dot(z, wglu_ref[...])
    xo_ref[0] = x + zz[:, :D_MODEL] * jax.nn.sigmoid(zz[:, D_MODEL:])


def _s5_weights(a_re, a_im, log_dt, b_re, b_im, c_re, c_im, powers):
    a_re, a_im = a_re.astype(F32), a_im.astype(F32)
    dt = jnp.exp(log_dt.astype(F32))[:, None]
    mag = jnp.exp(a_re * dt)
    lre, lim = mag * jnp.cos(a_im * dt), mag * jnp.sin(a_im * dt)
    den = a_re * a_re + a_im * a_im
    fre = ((lre - 1.0) * a_re + lim * a_im) / den
    fim = (lim * a_re - (lre - 1.0) * a_im) / den
    b_re, b_im = b_re.astype(F32), b_im.astype(F32)
    bb = jnp.stack([fre[:, :, None] * b_re - fim[:, :, None] * b_im,
                    fre[:, :, None] * b_im + fim[:, :, None] * b_re])
    eye = jnp.eye(SSM_GROUPS_PER_BLOCK, dtype=F32)
    bb = bb.reshape(2, SSM_BLOCKS, SSM_GROUPS_PER_BLOCK, SSM_STATE, SSM_GROUP)
    bbd = jnp.einsum("rkgph,gq->kghrqp", bb, eye).reshape(SSM_BLOCKS, SSM_BLOCK_IN, SSM_BLOCK_STATE)
    cc = jnp.stack([c_re.astype(F32), -c_im.astype(F32)])
    cc = cc.reshape(2, SSM_BLOCKS, SSM_GROUPS_PER_BLOCK, SSM_GROUP, SSM_STATE)
    cbd = jnp.einsum("rkghp,gq->krgpqh", cc, eye).reshape(SSM_BLOCKS, SSM_BLOCK_STATE, SSM_BLOCK_IN)
    tables = []
    for pows in powers:
        k = jnp.asarray(pows, F32)[:, None, None]
        mag_k = jnp.exp(k * (a_re * dt))
        pw = jnp.stack([mag_k * jnp.cos(k * (a_im * dt)), mag_k * jnp.sin(k * (a_im * dt))], axis=1)
        tables.append(pw.reshape(2 * len(pows), SSM_SLABS, LANES).transpose(1, 0, 2))
    return bbd.astype(BF16), cbd.astype(BF16), tables


def _state_to_slabs(h):
    b = h.shape[0]
    h = h.astype(F32).reshape(b, SSM_SLABS, LANES, 2).transpose(3, 1, 0, 2)
    return h[0], h[1]


def _slabs_to_state(re, im):
    b = re.shape[1]
    h = jnp.stack([re, im], axis=-1).transpose(1, 0, 2, 3)
    return h.reshape(b, N_SSM_GROUPS, SSM_STATE, 2)


def _s5(x, h0re, h0im, g, bbd, cbd, pw, d_skip, w_glu, *, tm, seg):
    nb, rows, _ = x.shape
    nseq = tm // seg
    assert nseq == 1 or rows == tm
    row = pl.BlockSpec((1, tm, D_MODEL), lambda b, i: (b, i, 0))
    st = pl.BlockSpec((1, SSM_SLABS, nseq, LANES), lambda b, i: (b, 0, 0, 0))
    slab = pltpu.VMEM((SSM_SLABS, SCAN_BASE + tm, LANES), F32)
    levels = []
    if nseq == 1:
        levels = [pltpu.VMEM((SCAN_UNROLL, SCAN_BASE + max(tm // SCAN_RADIX ** lvl, SUBLANES), LANES), F32)
                  for lvl in range(1, _scan_levels(tm))] * 2
    st_shape = jax.ShapeDtypeStruct(h0re.shape, F32)
    return pl.pallas_call(
        functools.partial(_s5_body, tm=tm, seg=seg),
        grid=(nb, rows // tm),
        in_specs=[row, st, st, _const_spec((1, D_MODEL)), _const_spec(bbd.shape), _const_spec(cbd.shape),
                  _const_spec(pw.shape), _const_spec((1, D_MODEL)), _const_spec(w_glu.shape)],
        out_specs=[row, st, st],
        out_shape=[jax.ShapeDtypeStruct(x.shape, F32), st_shape, st_shape],
        scratch_shapes=[slab, slab] + levels,
        compiler_params=_params(("parallel", "arbitrary")),
        name="s5_mixer",
    )(x, h0re, h0im, g.reshape(1, D_MODEL), bbd, cbd, pw, d_skip.reshape(1, D_MODEL), w_glu)


def _rope_tables(pos):
    half = ROT_DIM // 2
    inv = ROPE_THETA ** (-jnp.arange(0, ROT_DIM, 2, dtype=F32) / ROT_DIM)
    ang = pos.astype(F32)[:, None] * inv[None, :]
    cos, sin = jnp.cos(ang), jnp.sin(ang)
    n = pos.shape[0]
    pad = HEAD_DIM - ROT_DIM
    c = jnp.concatenate([cos, cos, jnp.ones((n, pad), F32)], axis=-1)
    s1 = jnp.concatenate([-sin, jnp.zeros((n, half + pad), F32)], axis=-1)
    s2 = jnp.concatenate([jnp.zeros((n, half), F32), sin, jnp.zeros((n, pad), F32)], axis=-1)
    rep = LANES // HEAD_DIM
    return tuple(jnp.tile(t, (1, rep)) for t in (c, s1, s2))


def _rope_tile(y, c, s1, s2):
    half = ROT_DIM // 2
    return y * c + pltpu.roll(y, LANES - half, 1) * s1 + pltpu.roll(y, half, 1) * s2


def _proj_body(x_ref, g_ref, w_ref, c_ref, s1_ref, s2_ref, *o_refs, rope_tiles):
    h = _rms(x_ref[...], g_ref[...]).astype(BF16)
    y = _dot(h, w_ref[...])
    c, s1, s2 = c_ref[...], s1_ref[...], s2_ref[...]
    tiles_per_out = y.shape[1] // LANES // len(o_refs)
    for j in range(y.shape[1] // LANES):
        yj = y[:, j * LANES:(j + 1) * LANES]
        if j < rope_tiles:
            yj = _rope_tile(yj, c, s1, s2)
        jo = j % tiles_per_out
        o_refs[j // tiles_per_out][:, jo * LANES:(jo + 1) * LANES] = yj


def _proj(x, g, w, tables, rope_width, n_out, tm=512):
    n = x.shape[0]
    tm = min(tm, n)
    width = w.shape[1] // n_out
    row = pl.BlockSpec((tm, D_MODEL), lambda i: (i, 0))
    tab = pl.BlockSpec((tm, LANES), lambda i: (i, 0))
    return pl.pallas_call(
        functools.partial(_proj_body, rope_tiles=rope_width // LANES),
        grid=(n // tm,),
        in_specs=[row, _const_spec((1, D_MODEL)), _const_spec(w.shape), tab, tab, tab],
        out_specs=[pl.BlockSpec((tm, width), lambda i: (i, 0))] * n_out,
        out_shape=[jax.ShapeDtypeStruct((n, width), F32)] * n_out,
        compiler_params=_params(("parallel",)),
        name="proj_rope",
    )(x, g.reshape(1, D_MODEL), w, *tables)


def _proj_heads_body(x_ref, g_ref, w_ref, c_ref, s1_ref, s2_ref, *refs, n_mat, tm, cache_from):
    n_t = n_mat if cache_from is not None else 0
    outs = refs[:N_BRANCHES * n_mat]
    t_refs = refs[N_BRANCHES * n_mat:N_BRANCHES * n_mat + n_t]
    slab = refs[-1]
    h = _rms(x_ref[0], g_ref[...]).astype(BF16)
    y = _dot(h, w_ref[...])
    c, s1, s2 = c_ref[...], s1_ref[...], s2_ref[...]
    for m in range(n_mat):
        tiles = []
        for br, d in enumerate(DILATIONS):
            lo = m * ATTN_WIDTH + br * MERGED_WIDTH
            part = [y[:, lo + j * LANES:lo + (j + 1) * LANES] for j in range(BRANCH_TILES)]
            if m == 0:
                part = [_rope_tile(t, c, s1, s2) for t in part]
            tiles += part
            out = outs[N_BRANCHES * m + br]
            if d == 1:
                for j in range(BRANCH_TILES):
                    out[0, 0, :, j * LANES:(j + 1) * LANES] = part[j].astype(BF16)
                continue
            for j in range(BRANCH_TILES):
                slab[(br - 1) * BRANCH_TILES + j] = part[j]
            for r in range(d):
                for j in range(BRANCH_TILES):
                    rows = slab[(br - 1) * BRANCH_TILES + j, pl.ds(r, tm // d, stride=d), :]
                    out[0, r, :, j * LANES:(j + 1) * LANES] = rows.astype(BF16)
        if cache_from is not None:
            @pl.when(pl.program_id(1) >= cache_from)
            def _(m=m, tiles=tiles):
                t_refs[m][0] = jnp.concatenate(tiles, axis=-1).T


def _proj_heads(x, g, w, tables, n_mat, keep=None, tm=512):
    b, l, _ = x.shape
    assert w.shape[1] == n_mat * ATTN_WIDTH and tm % (DILATIONS[-1] * 2 * SUBLANES) == 0
    cache_from = None if keep is None else (l - keep) // tm
    row = pl.BlockSpec((1, tm, D_MODEL), lambda bi, i: (bi, i, 0))
    tab = pl.BlockSpec((tm, LANES), lambda bi, i: (i, 0))
    out_specs, out_shape = [], []
    for _ in range(n_mat):
        for d in DILATIONS:
            out_specs.append(pl.BlockSpec((1, d, tm // d, MERGED_WIDTH), lambda bi, i: (bi, 0, i, 0)))
            out_shape.append(jax.ShapeDtypeStruct((b, d, l // d, MERGED_WIDTH), BF16))
    if keep is not None:
        for _ in range(n_mat):
            out_specs.append(pl.BlockSpec((1, ATTN_WIDTH, tm),
                                          lambda bi, i: (bi, 0, jnp.maximum(i - cache_from, 0))))
            out_shape.append(jax.ShapeDtypeStruct((b, ATTN_WIDTH, keep), F32))
    return pl.pallas_call(
        functools.partial(_proj_heads_body, n_mat=n_mat, tm=tm, cache_from=cache_from),
        grid=(b, l // tm),
        in_specs=[row, _const_spec((1, D_MODEL)), _const_spec(w.shape), tab, tab, tab],
        out_specs=out_specs,
        out_shape=out_shape,
        scratch_shapes=[pltpu.VMEM(((N_BRANCHES - 1) * BRANCH_TILES, tm, LANES), F32)],
        compiler_params=_params(("parallel", "arbitrary")),
        name="proj_heads",
    )(x, g.reshape(1, D_MODEL), w, *tables)


def _swa_body(q_ref, kp_ref, kc_ref, vp_ref, vc_ref, o_ref, lse_ref, *, steps_per_seq):
    n = pl.program_id(1)
    k = jnp.concatenate([kp_ref[0], kc_ref[0]], axis=0)
    v = jnp.concatenate([vp_ref[0], vc_ref[0]], axis=0)
    qi = lax.broadcasted_iota(jnp.int32, (SWA_BLOCK, 2 * SWA_BLOCK), 0)
    kj = lax.broadcasted_iota(jnp.int32, (SWA_BLOCK, 2 * SWA_BLOCK), 1)
    band = (kj >= qi) & (kj <= qi + SWA_BLOCK)
    has_prev = (n % steps_per_seq) != 0
    first = band & ((kj >= SWA_BLOCK) | has_prev)
    low = lax.broadcasted_iota(jnp.int32, (SWA_BLOCK, LANES), 1) < HEAD_DIM
    zero = jnp.zeros((SWA_BLOCK, LANES), BF16)
    for j in range(SWA_SUB):
        rows = slice(j * SWA_BLOCK, (j + 1) * SWA_BLOCK)
        keys = slice(j * SWA_BLOCK, (j + 2) * SWA_BLOCK)
        valid = first if j == 0 else band
        for t in range(BRANCH_TILES):
            cols = slice(t * LANES, (t + 1) * LANES)
            q = q_ref[0, rows, cols] * (HEAD_DIM ** -0.5)
            kt, vt = k[keys, cols], v[keys, cols]
            outs, lses = [], []
            for qh in (jnp.where(low, q, zero), jnp.where(low, zero, q)):
                s = jnp.where(valid, _dot_nt(qh, kt), NEG_INF)
                m = jnp.max(s, axis=-1, keepdims=True)
                p = jnp.exp(s - m)
                l = jnp.sum(p, axis=-1, keepdims=True)
                outs.append(_dot(p.astype(BF16), vt) / l)
                lses.append(jnp.broadcast_to(m + jnp.log(l), (SWA_BLOCK, LANES)))
            o_ref[0, rows, cols] = jnp.where(low, outs[0], outs[1])
            lse_ref[0, rows, cols] = jnp.where(low, lses[0], lses[1])


def _swa(q, k, v, dilation):
    b, l, w = q.shape
    step = SWA_SUB * SWA_BLOCK
    assert (l // dilation) % step == 0
    cur = pl.BlockSpec((1, step, w), lambda bi, n: (bi, n, 0))
    prev = pl.BlockSpec((1, SWA_BLOCK, w), lambda bi, n: (bi, jnp.maximum(SWA_SUB * n - 1, 0), 0))
    out = jax.ShapeDtypeStruct((b, l, w), F32)
    return pl.pallas_call(
        functools.partial(_swa_body, steps_per_seq=l // dilation // step),
        grid=(b, l // step),
        in_specs=[cur, prev, cur, prev, cur],
        out_specs=[cur, cur],
        out_shape=[out, out],
        compiler_params=_params(("parallel", "arbitrary")),
        name="swa",
    )(q, k, k, v, v)


def _merge_body(x_ref, *refs, tm):
    pairs, wo_ref, out_ref, slab = refs[:2 * N_BRANCHES], refs[2 * N_BRANCHES], refs[-2], refs[-1]
    for a, ref in enumerate(pairs[2:]):
        d = DILATIONS[1 + a // 2]
        for r in range(d):
            for j in range(BRANCH_TILES):
                slab[a * BRANCH_TILES + j, pl.ds(r, tm // d, stride=d), :] = ref[0, r, :, j * LANES:(j + 1) * LANES]
    tiles = []
    for j in range(BRANCH_TILES):
        cols = slice(j * LANES, (j + 1) * LANES)
        o0, l0 = pairs[0][0, 0, :, cols], pairs[1][0, 0, :, cols]
        o1, l1, o2, l2 = (slab[a * BRANCH_TILES + j] for a in range(4))
        m = jnp.maximum(jnp.maximum(l0, l1), l2)
        e0, e1, e2 = jnp.exp(l0 - m), jnp.exp(l1 - m), jnp.exp(l2 - m)
        tiles.append(((e0 * o0 + e1 * o1 + e2 * o2) / (e0 + e1 + e2)).astype(BF16))
    out_ref[0] = x_ref[0] + _dot(jnp.concatenate(tiles, axis=-1), wo_ref[...])


def _merge(x, os, ls, w_o, tm=512):
    b, l, _ = x.shape
    row = pl.BlockSpec((1, tm, D_MODEL), lambda bi, i: (bi, i, 0))
    specs, args = [], []
    for d, o, lse in zip(DILATIONS, os, ls):
        spec = pl.BlockSpec((1, d, tm // d, MERGED_WIDTH), lambda bi, i: (bi, 0, i, 0))
        specs += [spec, spec]
        args += [o, lse]
    return pl.pallas_call(
        functools.partial(_merge_body, tm=tm),
        grid=(b, l // tm),
        in_specs=[row] + specs + [_const_spec(w_o.shape)],
        out_specs=row,
        out_shape=jax.ShapeDtypeStruct(x.shape, F32),
        scratch_shapes=[pltpu.VMEM((2 * (N_BRANCHES - 1) * BRANCH_TILES, tm, LANES), F32)],
        compiler_params=_params(("parallel", "parallel")),
        name="merge_wo",
    )(x, *args, w_o)


def _decode_body(q_ref, kn_ref, vn_ref, k0_ref, k1_ref, k2_ref, v0_ref, v1_ref, v2_ref, o_ref, *, steps):
    scale = HEAD_DIM ** -0.5
    qi = lax.broadcasted_iota(jnp.int32, (steps, steps), 0)
    kj = lax.broadcasted_iota(jnp.int32, (steps, steps), 1)
    k_refs, v_refs = (k0_ref, k1_ref, k2_ref), (v0_ref, v1_ref, v2_ref)

    def slot_body(sl, carry):
        scores, values = [], []
        for br, d in enumerate(DILATIONS):
            w = WINDOWS[br]
            q = q_ref[0, sl, br].astype(BF16)
            back = (w + lax.broadcasted_iota(jnp.int32, (steps, w), 0)
                    - lax.broadcasted_iota(jnp.int32, (steps, w), 1))
            ok = ((back & (d - 1)) == 0) & (back <= w)
            scores.append(jnp.where(ok, _dot(q, k_refs[br][0, 0, sl].astype(BF16)) * scale, NEG_INF))
            values.append((v_refs[br][0, 0, sl].astype(BF16), True))
            ok = (kj <= qi) & (((qi - kj) & (d - 1)) == 0)
            scores.append(jnp.where(ok, _dot_nt(q, kn_ref[0, sl, br].astype(BF16)) * scale, NEG_INF))
            values.append((vn_ref[0, sl, br].astype(BF16), False))
        m = functools.reduce(jnp.maximum, [jnp.max(s, axis=-1, keepdims=True) for s in scores])
        num = jnp.zeros((steps, HEAD_DIM), F32)
        den = jnp.zeros((steps, 1), F32)
        for s, (v, transposed) in zip(scores, values):
            p = jnp.exp(s - m)
            den = den + jnp.sum(p, axis=-1, keepdims=True)
            num = num + (_dot_nt(p.astype(BF16), v) if transposed else _dot(p.astype(BF16), v))
        o_ref[0, sl] = num / den
        return carry

    lax.fori_loop(0, HEADS_PER_BRANCH, slot_body, 0)


def _slots(t):
    b, steps = t.shape[0], t.shape[1]
    return t.reshape(b, steps, N_BRANCHES, HEADS_PER_BRANCH, HEAD_DIM).transpose(0, 3, 2, 1, 4)


def _slots_t(t):
    b, steps = t.shape[0], t.shape[1]
    return t.reshape(b, steps, N_BRANCHES, HEADS_PER_BRANCH, HEAD_DIM).transpose(0, 2, 3, 4, 1)


def _cache_t(c):
    b, past = c.shape[0], c.shape[1]
    return c.transpose(0, 2, 3, 1).reshape(b, N_BRANCHES, HEADS_PER_BRANCH, HEAD_DIM, past)


def _cache_from_t(c):
    b, past = c.shape[0], c.shape[-1]
    return c.reshape(b, N_HEADS, HEAD_DIM, past).transpose(0, 3, 1, 2)


def _decode(q, kn, vn, cache_k, cache_v):
    b, steps = q.shape[0], q.shape[1]
    past = cache_k.shape[-1]
    assert past == MAX_WINDOW
    hb = HEADS_PER_BRANCH
    new = pl.BlockSpec((1, hb, N_BRANCHES, steps, HEAD_DIM), lambda i: (i, 0, 0, 0, 0))
    windows = [pl.BlockSpec((1, 1, hb, HEAD_DIM, w), functools.partial(lambda br, last, i: (i, br, 0, 0, last),
                                                                      br, past // w - 1))
               for br, w in enumerate(WINDOWS)]
    o = pl.pallas_call(
        functools.partial(_decode_body, steps=steps),
        grid=(b,),
        in_specs=[new, new, new] + windows + windows,
        out_specs=pl.BlockSpec((1, hb, steps, HEAD_DIM), lambda i: (i, 0, 0, 0)),
        out_shape=jax.ShapeDtypeStruct((b, hb, steps, HEAD_DIM), F32),
        compiler_params=_params(("parallel",)),
        name="decode_attn",
    )(_slots(q), _slots(kn), _slots(vn), *([cache_k] * N_BRANCHES), *([cache_v] * N_BRANCHES))
    return o.transpose(0, 2, 1, 3).reshape(b * steps, MERGED_WIDTH)


def _addproj_body(x_ref, o_ref, wo_ref, out_ref):
    out_ref[...] = x_ref[...] + _dot(o_ref[...].astype(BF16), wo_ref[...])


def _addproj(x, o, w_o):
    n = x.shape[0]
    return pl.pallas_call(
        _addproj_body,
        grid=(1,),
        in_specs=[_const_spec(x.shape), _const_spec(o.shape), _const_spec(w_o.shape)],
        out_specs=pl.BlockSpec(x.shape, lambda i: (0, 0)),
        out_shape=jax.ShapeDtypeStruct((n, D_MODEL), F32),
        compiler_params=_params(("arbitrary",)),
        name="add_wo",
    )(x, o, w_o)


def kernel(x_prompt, x_sample, state_ssm, cache_k, cache_v, ffn_norm1, ffn_w1_in, ffn_w1_out, ffn_norm2,
           ffn_w2_in, ffn_w2_out, mix_norm, ssm_a_re, ssm_a_im, ssm_log_dt, ssm_b_re, ssm_b_im, ssm_c_re,
           ssm_c_im, ssm_d, ssm_w_glu, kv_norm, w_kv, w_q, w_o, final_norm):
    bp, sp, _ = x_prompt.shape
    bs, ss, _ = x_sample.shape
    tm_scan = 256

    w1_in, w1_out, w2_in, w2_out = (w.astype(BF16) for w in (ffn_w1_in, ffn_w1_out, ffn_w2_in, ffn_w2_out))
    w_glu, w_kv_b, w_q_b, w_o_b = (ssm_w_glu[0].astype(BF16), w_kv.astype(BF16), w_q[0].astype(BF16),
                                   w_o[0].astype(BF16))
    bbd, cbd, (pw_p, pw_s) = _s5_weights(ssm_a_re[0], ssm_a_im[0], ssm_log_dt[0], ssm_b_re[0], ssm_b_im[0],
                                         ssm_c_re[0], ssm_c_im[0], (_scan_powers(tm_scan), [1]))

    def mixer(x, h0, pw, tm, seg):
        return _s5(x, *h0, mix_norm[0], bbd, cbd, pw, ssm_d[0], w_glu, tm=tm, seg=seg)

    pos_s = jnp.tile(PAST_LEN + jnp.arange(ss, dtype=jnp.int32), bs)
    tab_s = _rope_tables(pos_s)
    h0_s = tuple(t[None] for t in _state_to_slabs(state_ssm[0]))
    xs = _ffn(x_sample.reshape(bs * ss, D_MODEL), ffn_norm1, w1_in, w1_out, 0)
    xs, hl_re, hl_im = mixer(xs.reshape(1, bs * ss, D_MODEL), h0_s, pw_s, bs * ss, ss)
    xs = _ffn(xs.reshape(bs * ss, D_MODEL), ffn_norm2, w2_in, w2_out, 0)
    state_sample = _slabs_to_state(hl_re[0], hl_im[0])[None]
    k_s, v_s = (t.reshape(bs, ss, N_HEADS, HEAD_DIM) for t in _proj(xs, kv_norm, w_kv_b, tab_s, ATTN_WIDTH, 2))
    ck_t, cv_t, kn_t, vn_t = _cache_t(cache_k), _cache_t(cache_v), _slots_t(k_s), _slots_t(v_s)
    half = bs // 2
    tm_roll = bp * sp // half

    tab_p = _rope_tables(jnp.arange(sp, dtype=jnp.int32))
    zeros = jnp.zeros((bp, SSM_SLABS, 1, LANES), F32)
    xp, k_roll = _ffn_roll(x_prompt.reshape(bp * sp, D_MODEL), ffn_norm1, w1_in, w1_out, 0, ck_t, kn_t, None, 0,
                           tm=tm_roll)
    xp, hl_re, hl_im = mixer(xp.reshape(bp, sp, D_MODEL), (zeros, zeros), pw_p, tm_scan, tm_scan)
    xp, k_roll = _ffn_roll(xp.reshape(bp * sp, D_MODEL), ffn_norm2, w2_in, w2_out, 0, ck_t, kn_t, k_roll, half,
                           tm=tm_roll)
    cache_k_sample = _cache_from_t(k_roll)
    state_prompt = _slabs_to_state(hl_re[:, :, 0].transpose(1, 0, 2), hl_im[:, :, 0].transpose(1, 0, 2))[None]
    keep_p = min(MAX_WINDOW, sp)
    *kv_d, kt_p, vt_p = _proj_heads(xp.reshape(bp, sp, D_MODEL), kv_norm, w_kv_b, tab_p, 2, keep=keep_p)
    cache_k_prompt = kt_p.reshape(bp, N_HEADS, HEAD_DIM, keep_p).transpose(0, 3, 1, 2)
    cache_v_prompt = vt_p.reshape(bp, N_HEADS, HEAD_DIM, keep_p).transpose(0, 3, 1, 2)
    xp, v_roll = _ffn_roll(xp, ffn_norm1, w1_in, w1_out, 1, cv_t, vn_t, None, 0, tm=tm_roll)
    xp = xp.reshape(bp, sp, D_MODEL)
    q_d = _proj_heads(xp, mix_norm[1], w_q_b, tab_p, 1)
    outs, lses = [], []
    for br, d in enumerate(DILATIONS):
        flat = lambda t: t.reshape(bp, sp, MERGED_WIDTH)
        o, lse = _swa(flat(q_d[br]), flat(kv_d[br]), flat(kv_d[N_BRANCHES + br]), d)
        outs.append(o.reshape(bp, d, sp // d, MERGED_WIDTH))
        lses.append(lse.reshape(bp, d, sp // d, MERGED_WIDTH))
    xp = _merge(xp, outs, lses, w_o_b).reshape(bp * sp, D_MODEL)
    y_prompt, v_roll = _ffn_roll(xp, ffn_norm2, w2_in, w2_out, 1, cv_t, vn_t, v_roll, half, final_g=final_norm,
                                 tm=tm_roll)
    y_prompt = y_prompt.reshape(bp, sp, D_MODEL)
    cache_v_sample = _cache_from_t(v_roll)

    xs = _ffn(xs, ffn_norm1, w1_in, w1_out, 1)
    q_s = _proj(xs, mix_norm[1], w_q_b, tab_s, ATTN_WIDTH, 1)[0].reshape(bs, ss, N_HEADS, HEAD_DIM)
    xs = _addproj(xs, _decode(q_s, k_s, v_s, ck_t, cv_t), w_o_b)
    y_sample = _ffn(xs, ffn_norm2, w2_in, w2_out, 1, final_g=final_norm).reshape(bs, ss, D_MODEL)

    return (y_prompt, y_sample, state_prompt, cache_k_prompt, cache_v_prompt,
            state_sample, cache_k_sample, cache_v_sample)
```

```python
import functools

import jax
import jax.numpy as jnp
from jax import lax
from jax.experimental import pallas as pl
from jax.experimental.pallas import tpu as pltpu

F32 = jnp.float32
BF16 = jnp.bfloat16

D_MODEL = 1024
D_FF = 2816
SSM_GROUP = 16
N_SSM_GROUPS = D_MODEL // SSM_GROUP
SSM_STATE = 64
HEAD_DIM = 64
WINDOWS = (128, 512, 2048)
DILATIONS = (1, 4, 16)
N_BRANCHES = 3
HEADS_PER_BRANCH = 8
N_HEADS = N_BRANCHES * HEADS_PER_BRANCH
ATTN_WIDTH = N_HEADS * HEAD_DIM
MERGED_WIDTH = HEADS_PER_BRANCH * HEAD_DIM
MAX_WINDOW = max(WINDOWS)
ROT_DIM = HEAD_DIM // 4
ROPE_THETA = 500000.0
PAST_LEN = 16384
RMS_EPS = 1e-6
NEG_INF = -1e30

LANES = 128
SUBLANES = 8
BRANCH_TILES = MERGED_WIDTH // LANES

SSM_BLOCKS = 4
SSM_GROUPS_PER_BLOCK = N_SSM_GROUPS // SSM_BLOCKS
SSM_BLOCK_IN = SSM_GROUPS_PER_BLOCK * SSM_GROUP
SSM_BLOCK_HALF = SSM_GROUPS_PER_BLOCK * SSM_STATE
SSM_BLOCK_STATE = 2 * SSM_BLOCK_HALF
SSM_SLABS = N_SSM_GROUPS * SSM_STATE // LANES
SSM_SLABS_PER_BLOCK = SSM_SLABS // SSM_BLOCKS
SCAN_RADIX = 4
SCAN_BASE = SUBLANES
SCAN_UNROLL = 4

FF_CHUNK = 256
SWA_BLOCK = 128
SWA_SUB = 4
ROLL_GROUP = 4

VMEM_LIMIT = 56 * 1024 * 1024


def _params(semantics):
    return pltpu.CompilerParams(dimension_semantics=semantics, vmem_limit_bytes=VMEM_LIMIT)


def _const_spec(shape):
    nd = len(shape)
    return pl.BlockSpec(shape, lambda *_: (0,) * nd, pipeline_mode=pl.Buffered(1))


def _rms(x, g):
    return x * lax.rsqrt(jnp.mean(x * x, axis=-1, keepdims=True) + RMS_EPS) * g


def _dot(a, b):
    return jnp.dot(a, b, preferred_element_type=F32)


def _dot_nt(a, b):
    return lax.dot_general(a, b, (((1,), (1,)), ((), ())), preferred_element_type=F32)


def _ffn_body(x_ref, g_ref, win_ref, wout_ref, *rest, final):
    if final:
        fg_ref, o_ref = rest
    else:
        (o_ref,) = rest
    x = x_ref[...]
    h = _rms(x, g_ref[...]).astype(BF16)
    acc = x
    for j in range(D_FF // FF_CHUNK):
        lo = j * FF_CHUNK
        gate = _dot(h, win_ref[:, lo:lo + FF_CHUNK])
        up = _dot(h, win_ref[:, D_FF + lo:D_FF + lo + FF_CHUNK])
        act = gate * jax.nn.sigmoid(gate) * up
        acc = acc + _dot((0.5 * act).astype(BF16), wout_ref[lo:lo + FF_CHUNK, :])
    if final:
        acc = _rms(acc, fg_ref[...])
    o_ref[...] = acc


def _layer_spec(w, layer):
    return pl.BlockSpec((None,) + w.shape[1:], lambda *_: (layer, 0, 0), pipeline_mode=pl.Buffered(1))


def _ffn(x, g, w_in, w_out, layer, final_g=None, tm=1024):
    n = x.shape[0]
    tm = min(tm, n)
    row = pl.BlockSpec((tm, D_MODEL), lambda i: (i, 0))
    in_specs = [row, _const_spec((1, D_MODEL)), _layer_spec(w_in, layer), _layer_spec(w_out, layer)]
    args = [x, g[layer].reshape(1, D_MODEL), w_in, w_out]
    if final_g is not None:
        in_specs.append(_const_spec((1, D_MODEL)))
        args.append(final_g.reshape(1, D_MODEL))
    return pl.pallas_call(
        functools.partial(_ffn_body, final=final_g is not None),
        grid=(n // tm,),
        in_specs=in_specs,
        out_specs=row,
        out_shape=jax.ShapeDtypeStruct((n, D_MODEL), F32),
        compiler_params=_params(("parallel",)),
        name="ffn",
    )(*args)


def _ffn_roll_body(x_ref, g_ref, win_ref, wout_ref, *rest, final, aliased, batch0, past, steps):
    rest = list(rest)
    fg_ref = rest.pop(0) if final else None
    new_ref, cache_hbm = rest.pop(0), rest.pop(0)
    if aliased:
        rest.pop(0)
    o_ref, rolled_hbm, in_buf, out_buf, in_sem, out_sem = rest
    b = batch0 + pl.program_id(0)
    n_groups = N_HEADS // ROLL_GROUP
    n_chunks = D_FF // FF_CHUNK
    lane = lax.broadcasted_iota(jnp.int32, (HEAD_DIM, LANES), 1)
    tiles = past // LANES

    class copy:
        def __init__(self, g, inward, seq=b):
            self.parts = []
            for k in range(ROLL_GROUP):
                br, sl = divmod(g * ROLL_GROUP + k, HEADS_PER_BRANCH)
                buf = (g % 2) * ROLL_GROUP + k
                if inward:
                    self.parts.append(pltpu.make_async_copy(cache_hbm.at[seq, br, sl], in_buf.at[buf],
                                                            in_sem.at[g % 2]))
                else:
                    self.parts.append(pltpu.make_async_copy(out_buf.at[buf], rolled_hbm.at[seq, br, sl],
                                                            out_sem.at[g % 2]))

        def start(self):
            for c in self.parts:
                c.start()

        def wait(self):
            for c in self.parts:
                c.wait()

    def arrive(g):
        copy(g, True).wait()
        if g >= 2:
            copy(g - 2, False).wait()

    def shift(g):
        for k in range(ROLL_GROUP):
            br, sl = divmod(g * ROLL_GROUP + k, HEADS_PER_BRANCH)
            src, dst = in_buf.at[(g % 2) * ROLL_GROUP + k], out_buf.at[(g % 2) * ROLL_GROUP + k]
            cur = pltpu.roll(src[:, 0:LANES], LANES - steps, 1)
            for t in range(tiles):
                if t + 1 < tiles:
                    nxt = pltpu.roll(src[:, (t + 1) * LANES:(t + 2) * LANES], LANES - steps, 1)
                else:
                    nxt = jnp.concatenate([jnp.zeros((HEAD_DIM, LANES - steps), F32), new_ref[0, br, sl]], axis=1)
                dst[:, t * LANES:(t + 1) * LANES] = jnp.where(lane < LANES - steps, cur, nxt)
                cur = nxt
        copy(g, False).start()
        if g + 2 < n_groups:
            copy(g + 2, True).start()
        else:
            @pl.when(pl.program_id(0) + 1 < pl.num_programs(0))
            def _():
                copy(g + 2 - n_groups, True, b + 1).start()

    @pl.when(pl.program_id(0) == 0)
    def _():
        copy(0, True).start()
        copy(1, True).start()

    x = x_ref[...]
    h = _rms(x, g_ref[...]).astype(BF16)
    acc = x
    group_at = {g * (n_chunks - 2) // n_groups: g for g in range(n_groups)}
    for j in range(n_chunks):
        lo = j * FF_CHUNK
        if j in group_at:
            arrive(group_at[j])
        gate = _dot(h, win_ref[:, lo:lo + FF_CHUNK])
        up = _dot(h, win_ref[:, D_FF + lo:D_FF + lo + FF_CHUNK])
        if j in group_at:
            shift(group_at[j])
        act = gate * jax.nn.sigmoid(gate) * up
        acc = acc + _dot((0.5 * act).astype(BF16), wout_ref[lo:lo + FF_CHUNK, :])
    for g in range(n_groups - 2, n_groups):
        copy(g, False).wait()
    if final:
        acc = _rms(acc, fg_ref[...])
    o_ref[...] = acc


def _ffn_roll(x, g, w_in, w_out, layer, cache, new, prev, batch0, final_g=None, tm=1024):
    n = x.shape[0]
    past, steps = cache.shape[-1], new.shape[-1]
    n_groups = N_HEADS // ROLL_GROUP
    assert n % tm == 0 and batch0 + n // tm <= cache.shape[0]
    assert HEADS_PER_BRANCH % ROLL_GROUP == 0 and n_groups % 2 == 0 and n_groups < D_FF // FF_CHUNK
    row = pl.BlockSpec((tm, D_MODEL), lambda i: (i, 0))
    anywhere = pl.BlockSpec(memory_space=pl.ANY)
    in_specs = [row, _const_spec((1, D_MODEL)), _layer_spec(w_in, layer), _layer_spec(w_out, layer)]
    args = [x, g[layer].reshape(1, D_MODEL), w_in, w_out]
    if final_g is not None:
        in_specs.append(_const_spec((1, D_MODEL)))
        args.append(final_g.reshape(1, D_MODEL))
    in_specs += [pl.BlockSpec((1,) + new.shape[1:], lambda i: (batch0 + i, 0, 0, 0, 0)), anywhere]
    args += [new, cache]
    aliases = {}
    if prev is not None:
        aliases = {len(args): 1}
        in_specs.append(anywhere)
        args.append(prev)
    unit = pltpu.VMEM((2 * ROLL_GROUP, HEAD_DIM, past), F32)
    return pl.pallas_call(
        functools.partial(_ffn_roll_body, final=final_g is not None, aliased=prev is not None, batch0=batch0,
                          past=past, steps=steps),
        grid=(n // tm,),
        in_specs=in_specs,
        out_specs=[row, anywhere],
        out_shape=[jax.ShapeDtypeStruct((n, D_MODEL), F32), jax.ShapeDtypeStruct(cache.shape, F32)],
        scratch_shapes=[unit, unit, pltpu.SemaphoreType.DMA((2,)), pltpu.SemaphoreType.DMA((2,))],
        input_output_aliases=aliases,
        compiler_params=_params(("arbitrary",)),
        name="ffn_roll",
    )(*args)


def _scan_levels(tm):
    levels, n = 0, tm
    while n > 1:
        assert n % SCAN_RADIX == 0
        n //= SCAN_RADIX
        levels += 1
    return levels


def _scan_powers(tm):
    pows = [SCAN_RADIX ** lvl * m for lvl in range(_scan_levels(tm)) for m in range(1, SCAN_RADIX)]
    return pows + [tm]


def _cmul(ar, ai, br, bi):
    return ar * br - ai * bi, ar * bi + ai * br


def _blocked_scan(hre, him, pw_ref, lv_re, lv_im, slab, slot, tm):
    levels = _scan_levels(tm)
    top = SCAN_BASE - 1

    def views(lvl):
        if lvl == 0:
            return (lambda sl: hre[slab, sl, :], lambda sl: him[slab, sl, :],
                    lambda sl, v: hre.__setitem__((slab, sl, slice(None)), v),
                    lambda sl, v: him.__setitem__((slab, sl, slice(None)), v))
        a, b = lv_re[lvl - 1], lv_im[lvl - 1]
        return (lambda sl: a[slot, sl, :], lambda sl: b[slot, sl, :],
                lambda sl, v: a.__setitem__((slot, sl, slice(None)), v),
                lambda sl, v: b.__setitem__((slot, sl, slice(None)), v))

    def power(k):
        return pw_ref[slab, pl.ds(2 * k, 1), :], pw_ref[slab, pl.ds(2 * k + 1, 1), :]

    def elems(n, tau):
        return pl.ds(SCAN_BASE + tau, n, stride=SCAN_RADIX)

    s_re, s_im = hre[slab, pl.ds(top, 1), :], him[slab, pl.ds(top, 1), :]
    for lvl in range(levels):
        get_re, get_im, put_re, put_im = views(lvl)
        n = tm // SCAN_RADIX ** (lvl + 1)
        lr, li = power((SCAN_RADIX - 1) * lvl)
        pr, pi = get_re(elems(n, 0)), get_im(elems(n, 0))
        for tau in range(1, SCAN_RADIX):
            mr, mi = _cmul(lr, li, pr, pi)
            pr, pi = mr + get_re(elems(n, tau)), mi + get_im(elems(n, tau))
            put_re(elems(n, tau), pr)
            put_im(elems(n, tau), pi)
        if lvl + 1 < levels:
            _, _, nput_re, nput_im = views(lvl + 1)
            nput_re(pl.ds(SCAN_BASE, n), pr)
            nput_im(pl.ds(SCAN_BASE, n), pi)
            nput_re(pl.ds(top, 1), s_re)
            nput_im(pl.ds(top, 1), s_im)
    for lvl in reversed(range(levels)):
        get_re, get_im, put_re, put_im = views(lvl)
        n = tm // SCAN_RADIX ** (lvl + 1)
        last = lvl + 1 == levels
        if not last:
            nget_re, nget_im, _, _ = views(lvl + 1)
            put_re(elems(n, SCAN_RADIX - 1), nget_re(pl.ds(SCAN_BASE, n)))
            put_im(elems(n, SCAN_RADIX - 1), nget_im(pl.ds(SCAN_BASE, n)))
        before = pl.ds(top, n, stride=SCAN_RADIX)
        tr, ti = get_re(before), get_im(before)
        for tau in range(SCAN_RADIX if last else SCAN_RADIX - 1):
            k = (SCAN_RADIX - 1) * lvl + tau if tau < SCAN_RADIX - 1 else (SCAN_RADIX - 1) * levels
            mr, mi = _cmul(*power(k), tr, ti)
            put_re(elems(n, tau), get_re(elems(n, tau)) + mr)
            put_im(elems(n, tau), get_im(elems(n, tau)) + mi)
    hre[slab, pl.ds(top, 1), :] = hre[slab, pl.ds(top + tm, 1), :]
    him[slab, pl.ds(top, 1), :] = him[slab, pl.ds(top + tm, 1), :]


def _s5_body(x_ref, h0re_ref, h0im_ref, g_ref, bbd_ref, cbd_ref, pw_ref, d_ref, wglu_ref,
             xo_ref, hlre_ref, hlim_ref, hre, him, *levels_scr, tm, seg):
    nseq = tm // seg
    rows = pl.ds(SCAN_BASE, tm)
    top = SCAN_BASE - 1
    n_lv = len(levels_scr) // 2
    lv_re, lv_im = levels_scr[:n_lv], levels_scr[n_lv:]

    if nseq == 1:
        @pl.when(pl.program_id(1) == 0)
        def _():
            hre[:, pl.ds(top, 1), :] = h0re_ref[0]
            him[:, pl.ds(top, 1), :] = h0im_ref[0]

    x = x_ref[0]
    u = _rms(x, g_ref[...])
    ub = u.astype(BF16)

    def project_in(k):
        bu = _dot(ub[:, k * SSM_BLOCK_IN:(k + 1) * SSM_BLOCK_IN], bbd_ref[k])
        for j in range(SSM_SLABS_PER_BLOCK):
            hre[k * SSM_SLABS_PER_BLOCK + j, rows, :] = bu[:, j * LANES:(j + 1) * LANES]
            him[k * SSM_SLABS_PER_BLOCK + j, rows, :] = bu[:, SSM_BLOCK_HALF + j * LANES:
                                                          SSM_BLOCK_HALF + (j + 1) * LANES]

    def project_out(k):
        sl = range(k * SSM_SLABS_PER_BLOCK, (k + 1) * SSM_SLABS_PER_BLOCK)
        hk = jnp.concatenate([hre[p, rows, :] for p in sl] + [him[p, rows, :] for p in sl], axis=-1)
        return _dot(hk.astype(BF16), cbd_ref[k])

    if nseq == 1:
        ys = []
        project_in(0)
        for k in range(SSM_BLOCKS):
            if k + 1 < SSM_BLOCKS:
                project_in(k + 1)
            for j in range(SSM_SLABS_PER_BLOCK):
                _blocked_scan(hre, him, pw_ref, lv_re, lv_im, k * SSM_SLABS_PER_BLOCK + j, j % SCAN_UNROLL, tm)
            ys.append(project_out(k))
        hlre_ref[0] = hre[:, pl.ds(top, 1), :]
        hlim_ref[0] = him[:, pl.ds(top, 1), :]
    else:
        for k in range(SSM_BLOCKS):
            project_in(k)

        def slab_body(p, carry):
            lr, li = pw_ref[p, pl.ds(0, 1), :], pw_ref[p, pl.ds(1, 1), :]
            hr, hi = h0re_ref[0, p], h0im_ref[0, p]
            for t in range(seg):
                step = pl.ds(SCAN_BASE + t, nseq, stride=seg)
                mr, mi = _cmul(lr, li, hr, hi)
                hr, hi = mr + hre[p, step, :], mi + him[p, step, :]
                hre[p, step, :] = hr
                him[p, step, :] = hi
            hlre_ref[0, p] = hr
            hlim_ref[0, p] = hi
            return carry

        lax.fori_loop(0, SSM_SLABS, slab_body, 0)
        ys = [project_out(k) for k in range(SSM_BLOCKS)]

    y = jnp.concatenate(ys, axis=-1) + d_ref[...] * u
    z = jax.nn.gelu(y, approximate=True).astype(BF16)
    zz = _dot(z, wglu_ref[...])
    xo_ref[0] = x + zz[:, :D_MODEL] * jax.nn.sigmoid(zz[:, D_MODEL:])


def _s5_weights(a_re, a_im, log_dt, b_re, b_im, c_re, c_im, powers):
    a_re, a_im = a_re.astype(F32), a_im.astype(F32)
    dt = jnp.exp(log_dt.astype(F32))[:, None]
    mag = jnp.exp(a_re * dt)
    lre, lim = mag * jnp.cos(a_im * dt), mag * jnp.sin(a_im * dt)
    den = a_re * a_re + a_im * a_im
    fre = ((lre - 1.0) * a_re + lim * a_im) / den
    fim = (lim * a_re - (lre - 1.0) * a_im) / den
    b_re, b_im = b_re.astype(F32), b_im.astype(F32)
    bb = jnp.stack([fre[:, :, None] * b_re - fim[:, :, None] * b_im,
                    fre[:, :, None] * b_im + fim[:, :, None] * b_re])
    eye = jnp.eye(SSM_GROUPS_PER_BLOCK, dtype=F32)
    bb = bb.reshape(2, SSM_BLOCKS, SSM_GROUPS_PER_BLOCK, SSM_STATE, SSM_GROUP)
    bbd = jnp.einsum("rkgph,gq->kghrqp", bb, eye).reshape(SSM_BLOCKS, SSM_BLOCK_IN, SSM_BLOCK_STATE)
    cc = jnp.stack([c_re.astype(F32), -c_im.astype(F32)])
    cc = cc.reshape(2, SSM_BLOCKS, SSM_GROUPS_PER_BLOCK, SSM_GROUP, SSM_STATE)
    cbd = jnp.einsum("rkghp,gq->krgpqh", cc, eye).reshape(SSM_BLOCKS, SSM_BLOCK_STATE, SSM_BLOCK_IN)
    tables = []
    for pows in powers:
        k = jnp.asarray(pows, F32)[:, None, None]
        mag_k = jnp.exp(k * (a_re * dt))
        pw = jnp.stack([mag_k * jnp.cos(k * (a_im * dt)), mag_k * jnp.sin(k * (a_im * dt))], axis=1)
        tables.append(pw.reshape(2 * len(pows), SSM_SLABS, LANES).transpose(1, 0, 2))
    return bbd.astype(BF16), cbd.astype(BF16), tables


def _state_to_slabs(h):
    b = h.shape[0]
    h = h.astype(F32).reshape(b, SSM_SLABS, LANES, 2).transpose(3, 1, 0, 2)
    return h[0], h[1]


def _slabs_to_state(re, im):
    b = re.shape[1]
    h = jnp.stack([re, im], axis=-1).transpose(1, 0, 2, 3)
    return h.reshape(b, N_SSM_GROUPS, SSM_STATE, 2)


def _s5(x, h0re, h0im, g, bbd, cbd, pw, d_skip, w_glu, *, tm, seg):
    nb, rows, _ = x.shape
    nseq = tm // seg
    assert nseq == 1 or rows == tm
    row = pl.BlockSpec((1, tm, D_MODEL), lambda b, i: (b, i, 0))
    st = pl.BlockSpec((1, SSM_SLABS, nseq, LANES), lambda b, i: (b, 0, 0, 0))
    slab = pltpu.VMEM((SSM_SLABS, SCAN_BASE + tm, LANES), F32)
    levels = []
    if nseq == 1:
        levels = [pltpu.VMEM((SCAN_UNROLL, SCAN_BASE + max(tm // SCAN_RADIX ** lvl, SUBLANES), LANES), F32)
                  for lvl in range(1, _scan_levels(tm))] * 2
    st_shape = jax.ShapeDtypeStruct(h0re.shape, F32)
    return pl.pallas_call(
        functools.partial(_s5_body, tm=tm, seg=seg),
        grid=(nb, rows // tm),
        in_specs=[row, st, st, _const_spec((1, D_MODEL)), _const_spec(bbd.shape), _const_spec(cbd.shape),
                  _const_spec(pw.shape), _const_spec((1, D_MODEL)), _const_spec(w_glu.shape)],
        out_specs=[row, st, st],
        out_shape=[jax.ShapeDtypeStruct(x.shape, F32), st_shape, st_shape],
        scratch_shapes=[slab, slab] + levels,
        compiler_params=_params(("parallel", "arbitrary")),
        name="s5_mixer",
    )(x, h0re, h0im, g.reshape(1, D_MODEL), bbd, cbd, pw, d_skip.reshape(1, D_MODEL), w_glu)


def _rope_tables(pos):
    half = ROT_DIM // 2
    inv = ROPE_THETA ** (-jnp.arange(0, ROT_DIM, 2, dtype=F32) / ROT_DIM)
    ang = pos.astype(F32)[:, None] * inv[None, :]
    cos, sin = jnp.cos(ang), jnp.sin(ang)
    n = pos.shape[0]
    pad = HEAD_DIM - ROT_DIM
    c = jnp.concatenate([cos, cos, jnp.ones((n, pad), F32)], axis=-1)
    s1 = jnp.concatenate([-sin, jnp.zeros((n, half + pad), F32)], axis=-1)
    s2 = jnp.concatenate([jnp.zeros((n, half), F32), sin, jnp.zeros((n, pad), F32)], axis=-1)
    rep = LANES // HEAD_DIM
    return tuple(jnp.tile(t, (1, rep)) for t in (c, s1, s2))


def _rope_tile(y, c, s1, s2):
    half = ROT_DIM // 2
    return y * c + pltpu.roll(y, LANES - half, 1) * s1 + pltpu.roll(y, half, 1) * s2


def _proj_body(x_ref, g_ref, w_ref, c_ref, s1_ref, s2_ref, *o_refs, rope_tiles):
    h = _rms(x_ref[...], g_ref[...]).astype(BF16)
    y = _dot(h, w_ref[...])
    c, s1, s2 = c_ref[...], s1_ref[...], s2_ref[...]
    tiles_per_out = y.shape[1] // LANES // len(o_refs)
    for j in range(y.shape[1] // LANES):
        yj = y[:, j * LANES:(j + 1) * LANES]
        if j < rope_tiles:
            yj = _rope_tile(yj, c, s1, s2)
        jo = j % tiles_per_out
        o_refs[j // tiles_per_out][:, jo * LANES:(jo + 1) * LANES] = yj


def _proj(x, g, w, tables, rope_width, n_out, tm=512):
    n = x.shape[0]
    tm = min(tm, n)
    width = w.shape[1] // n_out
    row = pl.BlockSpec((tm, D_MODEL), lambda i: (i, 0))
    tab = pl.BlockSpec((tm, LANES), lambda i: (i, 0))
    return pl.pallas_call(
        functools.partial(_proj_body, rope_tiles=rope_width // LANES),
        grid=(n // tm,),
        in_specs=[row, _const_spec((1, D_MODEL)), _const_spec(w.shape), tab, tab, tab],
        out_specs=[pl.BlockSpec((tm, width), lambda i: (i, 0))] * n_out,
        out_shape=[jax.ShapeDtypeStruct((n, width), F32)] * n_out,
        compiler_params=_params(("parallel",)),
        name="proj_rope",
    )(x, g.reshape(1, D_MODEL), w, *tables)


def _proj_heads_body(x_ref, g_ref, w_ref, c_ref, s1_ref, s2_ref, *refs, n_mat, tm, cache_from):
    n_t = n_mat if cache_from is not None else 0
    outs = refs[:N_BRANCHES * n_mat]
    t_refs = refs[N_BRANCHES * n_mat:N_BRANCHES * n_mat + n_t]
    slab = refs[-1]
    h = _rms(x_ref[0], g_ref[...]).astype(BF16)
    y = _dot(h, w_ref[...])
    c, s1, s2 = c_ref[...], s1_ref[...], s2_ref[...]
    for m in range(n_mat):
        tiles = []
        for br, d in enumerate(DILATIONS):
            lo = m * ATTN_WIDTH + br * MERGED_WIDTH
            part = [y[:, lo + j * LANES:lo + (j + 1) * LANES] for j in range(BRANCH_TILES)]
            if m == 0:
                part = [_rope_tile(t, c, s1, s2) for t in part]
            tiles += part
            out = outs[N_BRANCHES * m + br]
            if d == 1:
                for j in range(BRANCH_TILES):
                    out[0, 0, :, j * LANES:(j + 1) * LANES] = part[j].astype(BF16)
                continue
            for j in range(BRANCH_TILES):
                slab[(br - 1) * BRANCH_TILES + j] = part[j]
            for r in range(d):
                for j in range(BRANCH_TILES):
                    rows = slab[(br - 1) * BRANCH_TILES + j, pl.ds(r, tm // d, stride=d), :]
                    out[0, r, :, j * LANES:(j + 1) * LANES] = rows.astype(BF16)
        if cache_from is not None:
            @pl.when(pl.program_id(1) >= cache_from)
            def _(m=m, tiles=tiles):
                t_refs[m][0] = jnp.concatenate(tiles, axis=-1).T


def _proj_heads(x, g, w, tables, n_mat, keep=None, tm=512):
    b, l, _ = x.shape
    assert w.shape[1] == n_mat * ATTN_WIDTH and tm % (DILATIONS[-1] * 2 * SUBLANES) == 0
    cache_from = None if keep is None else (l - keep) // tm
    row = pl.BlockSpec((1, tm, D_MODEL), lambda bi, i: (bi, i, 0))
    tab = pl.BlockSpec((tm, LANES), lambda bi, i: (i, 0))
    out_specs, out_shape = [], []
    for _ in range(n_mat):
        for d in DILATIONS:
            out_specs.append(pl.BlockSpec((1, d, tm // d, MERGED_WIDTH), lambda bi, i: (bi, 0, i, 0)))
            out_shape.append(jax.ShapeDtypeStruct((b, d, l // d, MERGED_WIDTH), BF16))
    if keep is not None:
        for _ in range(n_mat):
            out_specs.append(pl.BlockSpec((1, ATTN_WIDTH, tm),
                                          lambda bi, i: (bi, 0, jnp.maximum(i - cache_from, 0))))
            out_shape.append(jax.ShapeDtypeStruct((b, ATTN_WIDTH, keep), F32))
    return pl.pallas_call(
        functools.partial(_proj_heads_body, n_mat=n_mat, tm=tm, cache_from=cache_from),
        grid=(b, l // tm),
        in_specs=[row, _const_spec((1, D_MODEL)), _const_spec(w.shape), tab, tab, tab],
        out_specs=out_specs,
        out_shape=out_shape,
        scratch_shapes=[pltpu.VMEM(((N_BRANCHES - 1) * BRANCH_TILES, tm, LANES), F32)],
        compiler_params=_params(("parallel", "arbitrary")),
        name="proj_heads",
    )(x, g.reshape(1, D_MODEL), w, *tables)


def _swa_body(q_ref, kp_ref, kc_ref, vp_ref, vc_ref, o_ref, lse_ref, *, steps_per_seq):
    n = pl.program_id(1)
    k = jnp.concatenate([kp_ref[0], kc_ref[0]], axis=0)
    v = jnp.concatenate([vp_ref[0], vc_ref[0]], axis=0)
    qi = lax.broadcasted_iota(jnp.int32, (SWA_BLOCK, 2 * SWA_BLOCK), 0)
    kj = lax.broadcasted_iota(jnp.int32, (SWA_BLOCK, 2 * SWA_BLOCK), 1)
    band = (kj >= qi) & (kj <= qi + SWA_BLOCK)
    has_prev = (n % steps_per_seq) != 0
    first = band & ((kj >= SWA_BLOCK) | has_prev)
    low = lax.broadcasted_iota(jnp.int32, (SWA_BLOCK, LANES), 1) < HEAD_DIM
    zero = jnp.zeros((SWA_BLOCK, LANES), BF16)
    for j in range(SWA_SUB):
        rows = slice(j * SWA_BLOCK, (j + 1) * SWA_BLOCK)
        keys = slice(j * SWA_BLOCK, (j + 2) * SWA_BLOCK)
        valid = first if j == 0 else band
        for t in range(BRANCH_TILES):
            cols = slice(t * LANES, (t + 1) * LANES)
            q = q_ref[0, rows, cols] * (HEAD_DIM ** -0.5)
            kt, vt = k[keys, cols], v[keys, cols]
            outs, lses = [], []
            for qh in (jnp.where(low, q, zero), jnp.where(low, zero, q)):
                s = jnp.where(valid, _dot_nt(qh, kt), NEG_INF)
                m = jnp.max(s, axis=-1, keepdims=True)
                p = jnp.exp(s - m)
                l = jnp.sum(p, axis=-1, keepdims=True)
                outs.append(_dot(p.astype(BF16), vt) / l)
                lses.append(jnp.broadcast_to(m + jnp.log(l), (SWA_BLOCK, LANES)))
            o_ref[0, rows, cols] = jnp.where(low, outs[0], outs[1])
            lse_ref[0, rows, cols] = jnp.where(low, lses[0], lses[1])


def _swa(q, k, v, dilation):
    b, l, w = q.shape
    step = SWA_SUB * SWA_BLOCK
    assert (l // dilation) % step == 0
    cur = pl.BlockSpec((1, step, w), lambda bi, n: (bi, n, 0))
    prev = pl.BlockSpec((1, SWA_BLOCK, w), lambda bi, n: (bi, jnp.maximum(SWA_SUB * n - 1, 0), 0))
    out = jax.ShapeDtypeStruct((b, l, w), F32)
    return pl.pallas_call(
        functools.partial(_swa_body, steps_per_seq=l // dilation // step),
        grid=(b, l // step),
        in_specs=[cur, prev, cur, prev, cur],
        out_specs=[cur, cur],
        out_shape=[out, out],
        compiler_params=_params(("parallel", "arbitrary")),
        name="swa",
    )(q, k, k, v, v)


def _merge_body(x_ref, *refs, tm):
    pairs, wo_ref, out_ref, slab = refs[:2 * N_BRANCHES], refs[2 * N_BRANCHES], refs[-2], refs[-1]
    for a, ref in enumerate(pairs[2:]):
        d = DILATIONS[1 + a // 2]
        for r in range(d):
            for j in range(BRANCH_TILES):
                slab[a * BRANCH_TILES + j, pl.ds(r, tm // d, stride=d), :] = ref[0, r, :, j * LANES:(j + 1) * LANES]
    tiles = []
    for j in range(BRANCH_TILES):
        cols = slice(j * LANES, (j + 1) * LANES)
        o0, l0 = pairs[0][0, 0, :, cols], pairs[1][0, 0, :, cols]
        o1, l1, o2, l2 = (slab[a * BRANCH_TILES + j] for a in range(4))
        m = jnp.maximum(jnp.maximum(l0, l1), l2)
        e0, e1, e2 = jnp.exp(l0 - m), jnp.exp(l1 - m), jnp.exp(l2 - m)
        tiles.append(((e0 * o0 + e1 * o1 + e2 * o2) / (e0 + e1 + e2)).astype(BF16))
    out_ref[0] = x_ref[0] + _dot(jnp.concatenate(tiles, axis=-1), wo_ref[...])


def _merge(x, os, ls, w_o, tm=512):
    b, l, _ = x.shape
    row = pl.BlockSpec((1, tm, D_MODEL), lambda bi, i: (bi, i, 0))
    specs, args = [], []
    for d, o, lse in zip(DILATIONS, os, ls):
        spec = pl.BlockSpec((1, d, tm // d, MERGED_WIDTH), lambda bi, i: (bi, 0, i, 0))
        specs += [spec, spec]
        args += [o, lse]
    return pl.pallas_call(
        functools.partial(_merge_body, tm=tm),
        grid=(b, l // tm),
        in_specs=[row] + specs + [_const_spec(w_o.shape)],
        out_specs=row,
        out_shape=jax.ShapeDtypeStruct(x.shape, F32),
        scratch_shapes=[pltpu.VMEM((2 * (N_BRANCHES - 1) * BRANCH_TILES, tm, LANES), F32)],
        compiler_params=_params(("parallel", "parallel")),
        name="merge_wo",
    )(x, *args, w_o)


def _decode_body(q_ref, kn_ref, vn_ref, k0_ref, k1_ref, k2_ref, v0_ref, v1_ref, v2_ref, o_ref, *, steps):
    scale = HEAD_DIM ** -0.5
    qi = lax.broadcasted_iota(jnp.int32, (steps, steps), 0)
    kj = lax.broadcasted_iota(jnp.int32, (steps, steps), 1)
    k_refs, v_refs = (k0_ref, k1_ref, k2_ref), (v0_ref, v1_ref, v2_ref)

    def slot_body(sl, carry):
        scores, values = [], []
        for br, d in enumerate(DILATIONS):
            w = WINDOWS[br]
            q = q_ref[0, sl, br].astype(BF16)
            back = (w + lax.broadcasted_iota(jnp.int32, (steps, w), 0)
                    - lax.broadcasted_iota(jnp.int32, (steps, w), 1))
            ok = ((back & (d - 1)) == 0) & (back <= w)
            scores.append(jnp.where(ok, _dot(q, k_refs[br][0, 0, sl].astype(BF16)) * scale, NEG_INF))
            values.append((v_refs[br][0, 0, sl].astype(BF16), True))
            ok = (kj <= qi) & (((qi - kj) & (d - 1)) == 0)
            scores.append(jnp.where(ok, _dot_nt(q, kn_ref[0, sl, br].astype(BF16)) * scale, NEG_INF))
            values.append((vn_ref[0, sl, br].astype(BF16), False))
        m = functools.reduce(jnp.maximum, [jnp.max(s, axis=-1, keepdims=True) for s in scores])
        num = jnp.zeros((steps, HEAD_DIM), F32)
        den = jnp.zeros((steps, 1), F32)
        for s, (v, transposed) in zip(scores, values):
            p = jnp.exp(s - m)
            den = den + jnp.sum(p, axis=-1, keepdims=True)
            num = num + (_dot_nt(p.astype(BF16), v) if transposed else _dot(p.astype(BF16), v))
        o_ref[0, sl] = num / den
        return carry

    lax.fori_loop(0, HEADS_PER_BRANCH, slot_body, 0)


def _slots(t):
    b, steps = t.shape[0], t.shape[1]
    return t.reshape(b, steps, N_BRANCHES, HEADS_PER_BRANCH, HEAD_DIM).transpose(0, 3, 2, 1, 4)


def _slots_t(t):
    b, steps = t.shape[0], t.shape[1]
    return t.reshape(b, steps, N_BRANCHES, HEADS_PER_BRANCH, HEAD_DIM).transpose(0, 2, 3, 4, 1)


def _cache_t(c):
    b, past = c.shape[0], c.shape[1]
    return c.transpose(0, 2, 3, 1).reshape(b, N_BRANCHES, HEADS_PER_BRANCH, HEAD_DIM, past)


def _cache_from_t(c):
    b, past = c.shape[0], c.shape[-1]
    return c.reshape(b, N_HEADS, HEAD_DIM, past).transpose(0, 3, 1, 2)


def _decode(q, kn, vn, cache_k, cache_v):
    b, steps = q.shape[0], q.shape[1]
    past = cache_k.shape[-1]
    assert past == MAX_WINDOW
    hb = HEADS_PER_BRANCH
    new = pl.BlockSpec((1, hb, N_BRANCHES, steps, HEAD_DIM), lambda i: (i, 0, 0, 0, 0))
    windows = [pl.BlockSpec((1, 1, hb, HEAD_DIM, w), functools.partial(lambda br, last, i: (i, br, 0, 0, last),
                                                                      br, past // w - 1))
               for br, w in enumerate(WINDOWS)]
    o = pl.pallas_call(
        functools.partial(_decode_body, steps=steps),
        grid=(b,),
        in_specs=[new, new, new] + windows + windows,
        out_specs=pl.BlockSpec((1, hb, steps, HEAD_DIM), lambda i: (i, 0, 0, 0)),
        out_shape=jax.ShapeDtypeStruct((b, hb, steps, HEAD_DIM), F32),
        compiler_params=_params(("parallel",)),
        name="decode_attn",
    )(_slots(q), _slots(kn), _slots(vn), *([cache_k] * N_BRANCHES), *([cache_v] * N_BRANCHES))
    return o.transpose(0, 2, 1, 3).reshape(b * steps, MERGED_WIDTH)


def _addproj_body(x_ref, o_ref, wo_ref, out_ref):
    out_ref[...] = x_ref[...] + _dot(o_ref[...].astype(BF16), wo_ref[...])


def _addproj(x, o, w_o):
    n = x.shape[0]
    return pl.pallas_call(
        _addproj_body,
        grid=(1,),
        in_specs=[_const_spec(x.shape), _const_spec(o.shape), _const_spec(w_o.shape)],
        out_specs=pl.BlockSpec(x.shape, lambda i: (0, 0)),
        out_shape=jax.ShapeDtypeStruct((n, D_MODEL), F32),
        compiler_params=_params(("arbitrary",)),
        name="add_wo",
    )(x, o, w_o)


def kernel(x_prompt, x_sample, state_ssm, cache_k, cache_v, ffn_norm1, ffn_w1_in, ffn_w1_out, ffn_norm2,
           ffn_w2_in, ffn_w2_out, mix_norm, ssm_a_re, ssm_a_im, ssm_log_dt, ssm_b_re, ssm_b_im, ssm_c_re,
           ssm_c_im, ssm_d, ssm_w_glu, kv_norm, w_kv, w_q, w_o, final_norm):
    bp, sp, _ = x_prompt.shape
    bs, ss, _ = x_sample.shape
    tm_scan = 256

    w1_in, w1_out, w2_in, w2_out = (w.astype(BF16) for w in (ffn_w1_in, ffn_w1_out, ffn_w2_in, ffn_w2_out))
    w_glu, w_kv_b, w_q_b, w_o_b = (ssm_w_glu[0].astype(BF16), w_kv.astype(BF16), w_q[0].astype(BF16),
                                   w_o[0].astype(BF16))
    bbd, cbd, (pw_p, pw_s) = _s5_weights(ssm_a_re[0], ssm_a_im[0], ssm_log_dt[0], ssm_b_re[0], ssm_b_im[0],
                                         ssm_c_re[0], ssm_c_im[0], (_scan_powers(tm_scan), [1]))

    def mixer(x, h0, pw, tm, seg):
        return _s5(x, *h0, mix_norm[0], bbd, cbd, pw, ssm_d[0], w_glu, tm=tm, seg=seg)

    pos_s = jnp.tile(PAST_LEN + jnp.arange(ss, dtype=jnp.int32), bs)
    tab_s = _rope_tables(pos_s)
    h0_s = tuple(t[None] for t in _state_to_slabs(state_ssm[0]))
    xs = _ffn(x_sample.reshape(bs * ss, D_MODEL), ffn_norm1, w1_in, w1_out, 0)
    xs, hl_re, hl_im = mixer(xs.reshape(1, bs * ss, D_MODEL), h0_s, pw_s, bs * ss, ss)
    xs = _ffn(xs.reshape(bs * ss, D_MODEL), ffn_norm2, w2_in, w2_out, 0)
    state_sample = _slabs_to_state(hl_re[0], hl_im[0])[None]
    k_s, v_s = (t.reshape(bs, ss, N_HEADS, HEAD_DIM) for t in _proj(xs, kv_norm, w_kv_b, tab_s, ATTN_WIDTH, 2))
    ck_t, cv_t, kn_t, vn_t = _cache_t(cache_k), _cache_t(cache_v), _slots_t(k_s), _slots_t(v_s)
    half = bs // 2
    tm_roll = bp * sp // half

    tab_p = _rope_tables(jnp.arange(sp, dtype=jnp.int32))
    zeros = jnp.zeros((bp, SSM_SLABS, 1, LANES), F32)
    xp, k_roll = _ffn_roll(x_prompt.reshape(bp * sp, D_MODEL), ffn_norm1, w1_in, w1_out, 0, ck_t, kn_t, None, 0,
                           tm=tm_roll)
    xp, hl_re, hl_im = mixer(xp.reshape(bp, sp, D_MODEL), (zeros, zeros), pw_p, tm_scan, tm_scan)
    xp, k_roll = _ffn_roll(xp.reshape(bp * sp, D_MODEL), ffn_norm2, w2_in, w2_out, 0, ck_t, kn_t, k_roll, half,
                           tm=tm_roll)
    cache_k_sample = _cache_from_t(k_roll)
    state_prompt = _slabs_to_state(hl_re[:, :, 0].transpose(1, 0, 2), hl_im[:, :, 0].transpose(1, 0, 2))[None]
    keep_p = min(MAX_WINDOW, sp)
    *kv_d, kt_p, vt_p = _proj_heads(xp.reshape(bp, sp, D_MODEL), kv_norm, w_kv_b, tab_p, 2, keep=keep_p)
    cache_k_prompt = kt_p.reshape(bp, N_HEADS, HEAD_DIM, keep_p).transpose(0, 3, 1, 2)
    cache_v_prompt = vt_p.reshape(bp, N_HEADS, HEAD_DIM, keep_p).transpose(0, 3, 1, 2)
    xp, v_roll = _ffn_roll(xp, ffn_norm1, w1_in, w1_out, 1, cv_t, vn_t, None, 0, tm=tm_roll)
    xp = xp.reshape(bp, sp, D_MODEL)
    q_d = _proj_heads(xp, mix_norm[1], w_q_b, tab_p, 1)
    outs, lses = [], []
    for br, d in enumerate(DILATIONS):
        flat = lambda t: t.reshape(bp, sp, MERGED_WIDTH)
        o, lse = _swa(flat(q_d[br]), flat(kv_d[br]), flat(kv_d[N_BRANCHES + br]), d)
        outs.append(o.reshape(bp, d, sp // d, MERGED_WIDTH))
        lses.append(lse.reshape(bp, d, sp // d, MERGED_WIDTH))
    xp = _merge(xp, outs, lses, w_o_b).reshape(bp * sp, D_MODEL)
    y_prompt, v_roll = _ffn_roll(xp, ffn_norm2, w2_in, w2_out, 1, cv_t, vn_t, v_roll, half, final_g=final_norm,
                                 tm=tm_roll)
    y_prompt = y_prompt.reshape(bp, sp, D_MODEL)
    cache_v_sample = _cache_from_t(v_roll)

    xs = _ffn(xs, ffn_norm1, w1_in, w1_out, 1)
    q_s = _proj(xs, mix_norm[1], w_q_b, tab_s, ATTN_WIDTH, 1)[0].reshape(bs, ss, N_HEADS, HEAD_DIM)
    xs = _addproj(xs, _decode(q_s, k_s, v_s, ck_t, cv_t), w_o_b)
    y_sample = _ffn(xs, ffn_norm2, w2_in, w2_out, 1, final_g=final_norm).reshape(bs, ss, D_MODEL)

    return (y_prompt, y_sample, state_prompt, cache_k_prompt, cache_v_prompt,
            state_sample, cache_k_sample, cache_v_sample)
```

```python
import functools

import jax
import jax.numpy as jnp
from jax import lax
from jax.experimental import pallas as pl
from jax.experimental.pallas import tpu as pltpu

F32 = jnp.float32
BF16 = jnp.bfloat16

D_MODEL = 1024
D_FF = 2816
SSM_GROUP = 16
N_SSM_GROUPS = D_MODEL // SSM_GROUP
SSM_STATE = 64
HEAD_DIM = 64
WINDOWS = (128, 512, 2048)
DILATIONS = (1, 4, 16)
N_BRANCHES = 3
HEADS_PER_BRANCH = 8
N_HEADS = N_BRANCHES * HEADS_PER_BRANCH
ATTN_WIDTH = N_HEADS * HEAD_DIM
MERGED_WIDTH = HEADS_PER_BRANCH * HEAD_DIM
MAX_WINDOW = max(WINDOWS)
ROT_DIM = HEAD_DIM // 4
ROPE_THETA = 500000.0
PAST_LEN = 16384
RMS_EPS = 1e-6
NEG_INF = -1e30

LANES = 128
SUBLANES = 8
BRANCH_TILES = MERGED_WIDTH // LANES

SSM_BLOCKS = 4
SSM_GROUPS_PER_BLOCK = N_SSM_GROUPS // SSM_BLOCKS
SSM_BLOCK_IN = SSM_GROUPS_PER_BLOCK * SSM_GROUP
SSM_BLOCK_HALF = SSM_GROUPS_PER_BLOCK * SSM_STATE
SSM_BLOCK_STATE = 2 * SSM_BLOCK_HALF
SSM_SLABS = N_SSM_GROUPS * SSM_STATE // LANES
SSM_SLABS_PER_BLOCK = SSM_SLABS // SSM_BLOCKS
SCAN_RADIX = 4
SCAN_BASE = SUBLANES
SCAN_UNROLL = 4

FF_CHUNK = 256
SWA_BLOCK = 128
SWA_SUB = 4
DECODE_UNROLL = 4
ROLL_GROUP = 4

VMEM_LIMIT = 56 * 1024 * 1024


def _params(semantics):
    return pltpu.CompilerParams(dimension_semantics=semantics, vmem_limit_bytes=VMEM_LIMIT)


def _const_spec(shape):
    nd = len(shape)
    return pl.BlockSpec(shape, lambda *_: (0,) * nd, pipeline_mode=pl.Buffered(1))


def _rms(x, g):
    return x * lax.rsqrt(jnp.mean(x * x, axis=-1, keepdims=True) + RMS_EPS) * g


def _dot(a, b):
    return jnp.dot(a, b, preferred_element_type=F32)


def _dot_nt(a, b):
    return lax.dot_general(a, b, (((1,), (1,)), ((), ())), preferred_element_type=F32)


def _ffn_body(x_ref, g_ref, win_ref, wout_ref, *rest, final):
    if final:
        fg_ref, o_ref = rest
    else:
        (o_ref,) = rest
    x = x_ref[...]
    h = _rms(x, g_ref[...]).astype(BF16)
    acc = x
    for j in range(D_FF // FF_CHUNK):
        lo = j * FF_CHUNK
        gate = _dot(h, win_ref[:, lo:lo + FF_CHUNK])
        up = _dot(h, win_ref[:, D_FF + lo:D_FF + lo + FF_CHUNK])
        act = gate * jax.nn.sigmoid(gate) * up
        acc = acc + _dot((0.5 * act).astype(BF16), wout_ref[lo:lo + FF_CHUNK, :])
    if final:
        acc = _rms(acc, fg_ref[...])
    o_ref[...] = acc


def _layer_spec(w, layer):
    return pl.BlockSpec((None,) + w.shape[1:], lambda *_: (layer, 0, 0), pipeline_mode=pl.Buffered(1))


def _ffn(x, g, w_in, w_out, layer, final_g=None, tm=1024):
    n = x.shape[0]
    tm = min(tm, n)
    row = pl.BlockSpec((tm, D_MODEL), lambda i: (i, 0))
    in_specs = [row, _const_spec((1, D_MODEL)), _layer_spec(w_in, layer), _layer_spec(w_out, layer)]
    args = [x, g[layer].reshape(1, D_MODEL), w_in, w_out]
    if final_g is not None:
        in_specs.append(_const_spec((1, D_MODEL)))
        args.append(final_g.reshape(1, D_MODEL))
    return pl.pallas_call(
        functools.partial(_ffn_body, final=final_g is not None),
        grid=(n // tm,),
        in_specs=in_specs,
        out_specs=row,
        out_shape=jax.ShapeDtypeStruct((n, D_MODEL), F32),
        compiler_params=_params(("parallel",)),
        name="ffn",
    )(*args)


def _ffn_roll_body(x_ref, g_ref, win_ref, wout_ref, *rest, final, aliased, batch0, past, steps):
    rest = list(rest)
    fg_ref = rest.pop(0) if final else None
    new_ref, cache_hbm = rest.pop(0), rest.pop(0)
    if aliased:
        rest.pop(0)
    o_ref, rolled_hbm, in_buf, out_buf, in_sem, out_sem = rest
    b = batch0 + pl.program_id(0)
    n_groups = N_HEADS // ROLL_GROUP
    n_chunks = D_FF // FF_CHUNK
    lane = lax.broadcasted_iota(jnp.int32, (HEAD_DIM, LANES), 1)
    tiles = past // LANES

    def copy(g, inward, seq=b):
        br, sl = divmod(g * ROLL_GROUP, HEADS_PER_BRANCH)
        half = pl.ds((g % 2) * ROLL_GROUP, ROLL_GROUP)
        if inward:
            return pltpu.make_async_copy(cache_hbm.at[seq, br, pl.ds(sl, ROLL_GROUP)], in_buf.at[half],
                                         in_sem.at[g % 2])
        return pltpu.make_async_copy(out_buf.at[half], rolled_hbm.at[seq, br, pl.ds(sl, ROLL_GROUP)],
                                     out_sem.at[g % 2])

    def arrive(g):
        copy(g, True).wait()
        if g >= 2:
            copy(g - 2, False).wait()

    def shift(g):
        for k in range(ROLL_GROUP):
            br, sl = divmod(g * ROLL_GROUP + k, HEADS_PER_BRANCH)
            src, dst = in_buf.at[(g % 2) * ROLL_GROUP + k], out_buf.at[(g % 2) * ROLL_GROUP + k]
            cur = pltpu.roll(src[:, 0:LANES], LANES - steps, 1)
            for t in range(tiles):
                if t + 1 < tiles:
                    nxt = pltpu.roll(src[:, (t + 1) * LANES:(t + 2) * LANES], LANES - steps, 1)
                else:
                    nxt = jnp.concatenate([jnp.zeros((HEAD_DIM, LANES - steps), F32), new_ref[0, br, sl]], axis=1)
                dst[:, t * LANES:(t + 1) * LANES] = jnp.where(lane < LANES - steps, cur, nxt)
                cur = nxt
        copy(g, False).start()
        if g + 2 < n_groups:
            copy(g + 2, True).start()
        else:
            @pl.when(pl.program_id(0) + 1 < pl.num_programs(0))
            def _():
                copy(g + 2 - n_groups, True, b + 1).start()

    @pl.when(pl.program_id(0) == 0)
    def _():
        copy(0, True).start()
        copy(1, True).start()

    x = x_ref[...]
    h = _rms(x, g_ref[...]).astype(BF16)
    acc = x
    group_at = {1 + g * (n_chunks - 1) // n_groups: g for g in range(n_groups)}
    for j in range(n_chunks):
        lo = j * FF_CHUNK
        if j in group_at:
            arrive(group_at[j])
        gate = _dot(h, win_ref[:, lo:lo + FF_CHUNK])
        up = _dot(h, win_ref[:, D_FF + lo:D_FF + lo + FF_CHUNK])
        if j in group_at:
            shift(group_at[j])
        act = gate * jax.nn.sigmoid(gate) * up
        acc = acc + _dot((0.5 * act).astype(BF16), wout_ref[lo:lo + FF_CHUNK, :])
    for g in range(n_groups - 2, n_groups):
        copy(g, False).wait()
    if final:
        acc = _rms(acc, fg_ref[...])
    o_ref[...] = acc


def _ffn_roll(x, g, w_in, w_out, layer, cache, new, prev, batch0, final_g=None, tm=1024):
    n = x.shape[0]
    past, steps = cache.shape[-1], new.shape[-1]
    n_groups = N_HEADS // ROLL_GROUP
    assert n % tm == 0 and batch0 + n // tm <= cache.shape[0]
    assert HEADS_PER_BRANCH % ROLL_GROUP == 0 and n_groups % 2 == 0 and n_groups < D_FF // FF_CHUNK
    row = pl.BlockSpec((tm, D_MODEL), lambda i: (i, 0))
    anywhere = pl.BlockSpec(memory_space=pl.ANY)
    in_specs = [row, _const_spec((1, D_MODEL)), _layer_spec(w_in, layer), _layer_spec(w_out, layer)]
    args = [x, g[layer].reshape(1, D_MODEL), w_in, w_out]
    if final_g is not None:
        in_specs.append(_const_spec((1, D_MODEL)))
        args.append(final_g.reshape(1, D_MODEL))
    in_specs += [pl.BlockSpec((1,) + new.shape[1:], lambda i: (batch0 + i, 0, 0, 0, 0)), anywhere]
    args += [new, cache]
    aliases = {}
    if prev is not None:
        aliases = {len(args): 1}
        in_specs.append(anywhere)
        args.append(prev)
    unit = pltpu.VMEM((2 * ROLL_GROUP, HEAD_DIM, past), F32)
    return pl.pallas_call(
        functools.partial(_ffn_roll_body, final=final_g is not None, aliased=prev is not None, batch0=batch0,
                          past=past, steps=steps),
        grid=(n // tm,),
        in_specs=in_specs,
        out_specs=[row, anywhere],
        out_shape=[jax.ShapeDtypeStruct((n, D_MODEL), F32), jax.ShapeDtypeStruct(cache.shape, F32)],
        scratch_shapes=[unit, unit, pltpu.SemaphoreType.DMA((2,)), pltpu.SemaphoreType.DMA((2,))],
        input_output_aliases=aliases,
        compiler_params=_params(("arbitrary",)),
        name="ffn_roll",
    )(*args)


def _scan_levels(tm):
    levels, n = 0, tm
    while n > 1:
        assert n % SCAN_RADIX == 0
        n //= SCAN_RADIX
        levels += 1
    return levels


def _scan_powers(tm):
    pows = [SCAN_RADIX ** lvl * m for lvl in range(_scan_levels(tm)) for m in range(1, SCAN_RADIX)]
    return pows + [tm]


def _cmul(ar, ai, br, bi):
    return ar * br - ai * bi, ar * bi + ai * br


def _blocked_scan(hre, him, pw_ref, lv_re, lv_im, slab, slot, tm):
    levels = _scan_levels(tm)
    top = SCAN_BASE - 1

    def views(lvl):
        if lvl == 0:
            return (lambda sl: hre[slab, sl, :], lambda sl: him[slab, sl, :],
                    lambda sl, v: hre.__setitem__((slab, sl, slice(None)), v),
                    lambda sl, v: him.__setitem__((slab, sl, slice(None)), v))
        a, b = lv_re[lvl - 1], lv_im[lvl - 1]
        return (lambda sl: a[slot, sl, :], lambda sl: b[slot, sl, :],
                lambda sl, v: a.__setitem__((slot, sl, slice(None)), v),
                lambda sl, v: b.__setitem__((slot, sl, slice(None)), v))

    def power(k):
        return pw_ref[slab, pl.ds(2 * k, 1), :], pw_ref[slab, pl.ds(2 * k + 1, 1), :]

    def elems(n, tau):
        return pl.ds(SCAN_BASE + tau, n, stride=SCAN_RADIX)

    s_re, s_im = hre[slab, pl.ds(top, 1), :], him[slab, pl.ds(top, 1), :]
    for lvl in range(levels):
        get_re, get_im, put_re, put_im = views(lvl)
        n = tm // SCAN_RADIX ** (lvl + 1)
        lr, li = power((SCAN_RADIX - 1) * lvl)
        pr, pi = get_re(elems(n, 0)), get_im(elems(n, 0))
        for tau in range(1, SCAN_RADIX):
            mr, mi = _cmul(lr, li, pr, pi)
            pr, pi = mr + get_re(elems(n, tau)), mi + get_im(elems(n, tau))
            put_re(elems(n, tau), pr)
            put_im(elems(n, tau), pi)
        if lvl + 1 < levels:
            _, _, nput_re, nput_im = views(lvl + 1)
            nput_re(pl.ds(SCAN_BASE, n), pr)
            nput_im(pl.ds(SCAN_BASE, n), pi)
            nput_re(pl.ds(top, 1), s_re)
            nput_im(pl.ds(top, 1), s_im)
    for lvl in reversed(range(levels)):
        get_re, get_im, put_re, put_im = views(lvl)
        n = tm // SCAN_RADIX ** (lvl + 1)
        last = lvl + 1 == levels
        if not last:
            nget_re, nget_im, _, _ = views(lvl + 1)
            put_re(elems(n, SCAN_RADIX - 1), nget_re(pl.ds(SCAN_BASE, n)))
            put_im(elems(n, SCAN_RADIX - 1), nget_im(pl.ds(SCAN_BASE, n)))
        before = pl.ds(top, n, stride=SCAN_RADIX)
        tr, ti = get_re(before), get_im(before)
        for tau in range(SCAN_RADIX if last else SCAN_RADIX - 1):
            k = (SCAN_RADIX - 1) * lvl + tau if tau < SCAN_RADIX - 1 else (SCAN_RADIX - 1) * levels
            mr, mi = _cmul(*power(k), tr, ti)
            put_re(elems(n, tau), get_re(elems(n, tau)) + mr)
            put_im(elems(n, tau), get_im(elems(n, tau)) + mi)
    hre[slab, pl.ds(top, 1), :] = hre[slab, pl.ds(top + tm, 1), :]
    him[slab, pl.ds(top, 1), :] = him[slab, pl.ds(top + tm, 1), :]


def _s5_body(x_ref, h0re_ref, h0im_ref, g_ref, bbd_ref, cbd_ref, pw_ref, d_ref, wglu_ref,
             xo_ref, hlre_ref, hlim_ref, hre, him, *levels_scr, tm, seg):
    nseq = tm // seg
    rows = pl.ds(SCAN_BASE, tm)
    top = SCAN_BASE - 1
    n_lv = len(levels_scr) // 2
    lv_re, lv_im = levels_scr[:n_lv], levels_scr[n_lv:]

    if nseq == 1:
        @pl.when(pl.program_id(1) == 0)
        def _():
            hre[:, pl.ds(top, 1), :] = h0re_ref[0]
            him[:, pl.ds(top, 1), :] = h0im_ref[0]

    x = x_ref[0]
    u = _rms(x, g_ref[...])
    ub = u.astype(BF16)

    def project_in(k):
        bu = _dot(ub[:, k * SSM_BLOCK_IN:(k + 1) * SSM_BLOCK_IN], bbd_ref[k])
        for j in range(SSM_SLABS_PER_BLOCK):
            hre[k * SSM_SLABS_PER_BLOCK + j, rows, :] = bu[:, j * LANES:(j + 1) * LANES]
            him[k * SSM_SLABS_PER_BLOCK + j, rows, :] = bu[:, SSM_BLOCK_HALF + j * LANES:
                                                          SSM_BLOCK_HALF + (j + 1) * LANES]

    def project_out(k):
        sl = range(k * SSM_SLABS_PER_BLOCK, (k + 1) * SSM_SLABS_PER_BLOCK)
        hk = jnp.concatenate([hre[p, rows, :] for p in sl] + [him[p, rows, :] for p in sl], axis=-1)
        return _dot(hk.astype(BF16), cbd_ref[k])

    if nseq == 1:
        ys = []
        project_in(0)
        for k in range(SSM_BLOCKS):
            if k + 1 < SSM_BLOCKS:
                project_in(k + 1)
            for j in range(SSM_SLABS_PER_BLOCK):
                _blocked_scan(hre, him, pw_ref, lv_re, lv_im, k * SSM_SLABS_PER_BLOCK + j, j % SCAN_UNROLL, tm)
            ys.append(project_out(k))
        hlre_ref[0] = hre[:, pl.ds(top, 1), :]
        hlim_ref[0] = him[:, pl.ds(top, 1), :]
    else:
        for k in range(SSM_BLOCKS):
            project_in(k)

        def slab_body(p, carry):
            lr, li = pw_ref[p, pl.ds(0, 1), :], pw_ref[p, pl.ds(1, 1), :]
            hr, hi = h0re_ref[0, p], h0im_ref[0, p]
            for t in range(seg):
                step = pl.ds(SCAN_BASE + t, nseq, stride=seg)
                mr, mi = _cmul(lr, li, hr, hi)
                hr, hi = mr + hre[p, step, :], mi + him[p, step, :]
                hre[p, step, :] = hr
                him[p, step, :] = hi
            hlre_ref[0, p] = hr
            hlim_ref[0, p] = hi
            return carry

        lax.fori_loop(0, SSM_SLABS, slab_body, 0)
        ys = [project_out(k) for k in range(SSM_BLOCKS)]

    y = jnp.concatenate(ys, axis=-1) + d_ref[...] * u
    z = jax.nn.gelu(y, approximate=True).astype(BF16)
    zz = _dot(z, wglu_ref[...])
    xo_ref[0] = x + zz[:, :D_MODEL] * jax.nn.sigmoid(zz[:, D_MODEL:])


def _s5_weights(a_re, a_im, log_dt, b_re, b_im, c_re, c_im, powers):
    a_re, a_im = a_re.astype(F32), a_im.astype(F32)
    dt = jnp.exp(log_dt.astype(F32))[:, None]
    mag = jnp.exp(a_re * dt)
    lre, lim = mag * jnp.cos(a_im * dt), mag * jnp.sin(a_im * dt)
    den = a_re * a_re + a_im * a_im
    fre = ((lre - 1.0) * a_re + lim * a_im) / den
    fim = (lim * a_re - (lre - 1.0) * a_im) / den
    b_re, b_im = b_re.astype(F32), b_im.astype(F32)
    bb = jnp.stack([fre[:, :, None] * b_re - fim[:, :, None] * b_im,
                    fre[:, :, None] * b_im + fim[:, :, None] * b_re])
    eye = jnp.eye(SSM_GROUPS_PER_BLOCK, dtype=F32)
    bb = bb.reshape(2, SSM_BLOCKS, SSM_GROUPS_PER_BLOCK, SSM_STATE, SSM_GROUP)
    bbd = jnp.einsum("rkgph,gq->kghrqp", bb, eye).reshape(SSM_BLOCKS, SSM_BLOCK_IN, SSM_BLOCK_STATE)
    cc = jnp.stack([c_re.astype(F32), -c_im.astype(F32)])
    cc = cc.reshape(2, SSM_BLOCKS, SSM_GROUPS_PER_BLOCK, SSM_GROUP, SSM_STATE)
    cbd = jnp.einsum("rkghp,gq->krgpqh", cc, eye).reshape(SSM_BLOCKS, SSM_BLOCK_STATE, SSM_BLOCK_IN)
    tables = []
    for pows in powers:
        k = jnp.asarray(pows, F32)[:, None, None]
        mag_k = jnp.exp(k * (a_re * dt))
        pw = jnp.stack([mag_k * jnp.cos(k * (a_im * dt)), mag_k * jnp.sin(k * (a_im * dt))], axis=1)
        tables.append(pw.reshape(2 * len(pows), SSM_SLABS, LANES).transpose(1, 0, 2))
    return bbd.astype(BF16), cbd.astype(BF16), tables


def _state_to_slabs(h):
    b = h.shape[0]
    h = h.astype(F32).reshape(b, SSM_SLABS, LANES, 2).transpose(3, 1, 0, 2)
    return h[0], h[1]


def _slabs_to_state(re, im):
    b = re.shape[1]
    h = jnp.stack([re, im], axis=-1).transpose(1, 0, 2, 3)
    return h.reshape(b, N_SSM_GROUPS, SSM_STATE, 2)


def _s5(x, h0re, h0im, g, bbd, cbd, pw, d_skip, w_glu, *, tm, seg):
    nb, rows, _ = x.shape
    nseq = tm // seg
    assert nseq == 1 or rows == tm
    row = pl.BlockSpec((1, tm, D_MODEL), lambda b, i: (b, i, 0))
    st = pl.BlockSpec((1, SSM_SLABS, nseq, LANES), lambda b, i: (b, 0, 0, 0))
    slab = pltpu.VMEM((SSM_SLABS, SCAN_BASE + tm, LANES), F32)
    levels = []
    if nseq == 1:
        levels = [pltpu.VMEM((SCAN_UNROLL, SCAN_BASE + max(tm // SCAN_RADIX ** lvl, SUBLANES), LANES), F32)
                  for lvl in range(1, _scan_levels(tm))] * 2
    st_shape = jax.ShapeDtypeStruct(h0re.shape, F32)
    return pl.pallas_call(
        functools.partial(_s5_body, tm=tm, seg=seg),
        grid=(nb, rows // tm),
        in_specs=[row, st, st, _const_spec((1, D_MODEL)), _const_spec(bbd.shape), _const_spec(cbd.shape),
                  _const_spec(pw.shape), _const_spec((1, D_MODEL)), _const_spec(w_glu.shape)],
        out_specs=[row, st, st],
        out_shape=[jax.ShapeDtypeStruct(x.shape, F32), st_shape, st_shape],
        scratch_shapes=[slab, slab] + levels,
        compiler_params=_params(("parallel", "arbitrary")),
        name="s5_mixer",
    )(x, h0re, h0im, g.reshape(1, D_MODEL), bbd, cbd, pw, d_skip.reshape(1, D_MODEL), w_glu)


def _rope_tables(pos):
    half = ROT_DIM // 2
    inv = ROPE_THETA ** (-jnp.arange(0, ROT_DIM, 2, dtype=F32) / ROT_DIM)
    ang = pos.astype(F32)[:, None] * inv[None, :]
    cos, sin = jnp.cos(ang), jnp.sin(ang)
    n = pos.shape[0]
    pad = HEAD_DIM - ROT_DIM
    c = jnp.concatenate([cos, cos, jnp.ones((n, pad), F32)], axis=-1)
    s1 = jnp.concatenate([-sin, jnp.zeros((n, half + pad), F32)], axis=-1)
    s2 = jnp.concatenate([jnp.zeros((n, half), F32), sin, jnp.zeros((n, pad), F32)], axis=-1)
    rep = LANES // HEAD_DIM
    return tuple(jnp.tile(t, (1, rep)) for t in (c, s1, s2))


def _rope_tile(y, c, s1, s2):
    half = ROT_DIM // 2
    return y * c + pltpu.roll(y, LANES - half, 1) * s1 + pltpu.roll(y, half, 1) * s2


def _proj_body(x_ref, g_ref, w_ref, c_ref, s1_ref, s2_ref, *o_refs, rope_tiles):
    h = _rms(x_ref[...], g_ref[...]).astype(BF16)
    y = _dot(h, w_ref[...])
    c, s1, s2 = c_ref[...], s1_ref[...], s2_ref[...]
    tiles_per_out = y.shape[1] // LANES // len(o_refs)
    for j in range(y.shape[1] // LANES):
        yj = y[:, j * LANES:(j + 1) * LANES]
        if j < rope_tiles:
            yj = _rope_tile(yj, c, s1, s2)
        jo = j % tiles_per_out
        o_refs[j // tiles_per_out][:, jo * LANES:(jo + 1) * LANES] = yj


def _proj(x, g, w, tables, rope_width, n_out, tm=512):
    n = x.shape[0]
    tm = min(tm, n)
    width = w.shape[1] // n_out
    row = pl.BlockSpec((tm, D_MODEL), lambda i: (i, 0))
    tab = pl.BlockSpec((tm, LANES), lambda i: (i, 0))
    return pl.pallas_call(
        functools.partial(_proj_body, rope_tiles=rope_width // LANES),
        grid=(n // tm,),
        in_specs=[row, _const_spec((1, D_MODEL)), _const_spec(w.shape), tab, tab, tab],
        out_specs=[pl.BlockSpec((tm, width), lambda i: (i, 0))] * n_out,
        out_shape=[jax.ShapeDtypeStruct((n, width), F32)] * n_out,
        compiler_params=_params(("parallel",)),
        name="proj_rope",
    )(x, g.reshape(1, D_MODEL), w, *tables)


def _proj_heads_body(x_ref, g_ref, w_ref, c_ref, s1_ref, s2_ref, *refs, n_mat, tm, cache_from):
    n_t = n_mat if cache_from is not None else 0
    outs = refs[:N_BRANCHES * n_mat]
    t_refs = refs[N_BRANCHES * n_mat:N_BRANCHES * n_mat + n_t]
    slab = refs[-1]
    h = _rms(x_ref[0], g_ref[...]).astype(BF16)
    y = _dot(h, w_ref[...])
    c, s1, s2 = c_ref[...], s1_ref[...], s2_ref[...]
    for m in range(n_mat):
        tiles = []
        for br, d in enumerate(DILATIONS):
            lo = m * ATTN_WIDTH + br * MERGED_WIDTH
            part = [y[:, lo + j * LANES:lo + (j + 1) * LANES] for j in range(BRANCH_TILES)]
            if m == 0:
                part = [_rope_tile(t, c, s1, s2) for t in part]
            tiles += part
            out = outs[N_BRANCHES * m + br]
            if d == 1:
                for j in range(BRANCH_TILES):
                    out[0, 0, :, j * LANES:(j + 1) * LANES] = part[j].astype(BF16)
                continue
            for j in range(BRANCH_TILES):
                slab[(br - 1) * BRANCH_TILES + j] = part[j]
            for r in range(d):
                for j in range(BRANCH_TILES):
                    rows = slab[(br - 1) * BRANCH_TILES + j, pl.ds(r, tm // d, stride=d), :]
                    out[0, r, :, j * LANES:(j + 1) * LANES] = rows.astype(BF16)
        if cache_from is not None:
            @pl.when(pl.program_id(1) >= cache_from)
            def _(m=m, tiles=tiles):
                t_refs[m][0] = jnp.concatenate(tiles, axis=-1).T


def _proj_heads(x, g, w, tables, n_mat, keep=None, tm=512):
    b, l, _ = x.shape
    assert w.shape[1] == n_mat * ATTN_WIDTH and tm % (DILATIONS[-1] * 2 * SUBLANES) == 0
    cache_from = None if keep is None else (l - keep) // tm
    row = pl.BlockSpec((1, tm, D_MODEL), lambda bi, i: (bi, i, 0))
    tab = pl.BlockSpec((tm, LANES), lambda bi, i: (i, 0))
    out_specs, out_shape = [], []
    for _ in range(n_mat):
        for d in DILATIONS:
            out_specs.append(pl.BlockSpec((1, d, tm // d, MERGED_WIDTH), lambda bi, i: (bi, 0, i, 0)))
            out_shape.append(jax.ShapeDtypeStruct((b, d, l // d, MERGED_WIDTH), BF16))
    if keep is not None:
        for _ in range(n_mat):
            out_specs.append(pl.BlockSpec((1, ATTN_WIDTH, tm),
                                          lambda bi, i: (bi, 0, jnp.maximum(i - cache_from, 0))))
            out_shape.append(jax.ShapeDtypeStruct((b, ATTN_WIDTH, keep), F32))
    return pl.pallas_call(
        functools.partial(_proj_heads_body, n_mat=n_mat, tm=tm, cache_from=cache_from),
        grid=(b, l // tm),
        in_specs=[row, _const_spec((1, D_MODEL)), _const_spec(w.shape), tab, tab, tab],
        out_specs=out_specs,
        out_shape=out_shape,
        scratch_shapes=[pltpu.VMEM(((N_BRANCHES - 1) * BRANCH_TILES, tm, LANES), F32)],
        compiler_params=_params(("parallel", "arbitrary")),
        name="proj_heads",
    )(x, g.reshape(1, D_MODEL), w, *tables)


def _swa_body(q_ref, kp_ref, kc_ref, vp_ref, vc_ref, o_ref, lse_ref, *, steps_per_seq):
    n = pl.program_id(1)
    k = jnp.concatenate([kp_ref[0], kc_ref[0]], axis=0)
    v = jnp.concatenate([vp_ref[0], vc_ref[0]], axis=0)
    qi = lax.broadcasted_iota(jnp.int32, (SWA_BLOCK, 2 * SWA_BLOCK), 0)
    kj = lax.broadcasted_iota(jnp.int32, (SWA_BLOCK, 2 * SWA_BLOCK), 1)
    band = (kj >= qi) & (kj <= qi + SWA_BLOCK)
    has_prev = (n % steps_per_seq) != 0
    first = band & ((kj >= SWA_BLOCK) | has_prev)
    low = lax.broadcasted_iota(jnp.int32, (SWA_BLOCK, LANES), 1) < HEAD_DIM
    zero = jnp.zeros((SWA_BLOCK, LANES), BF16)
    for j in range(SWA_SUB):
        rows = slice(j * SWA_BLOCK, (j + 1) * SWA_BLOCK)
        keys = slice(j * SWA_BLOCK, (j + 2) * SWA_BLOCK)
        valid = first if j == 0 else band
        for t in range(BRANCH_TILES):
            cols = slice(t * LANES, (t + 1) * LANES)
            q = q_ref[0, rows, cols] * (HEAD_DIM ** -0.5)
            kt, vt = k[keys, cols], v[keys, cols]
            outs, lses = [], []
            for qh in (jnp.where(low, q, zero), jnp.where(low, zero, q)):
                s = jnp.where(valid, _dot_nt(qh, kt), NEG_INF)
                m = jnp.max(s, axis=-1, keepdims=True)
                p = jnp.exp(s - m)
                l = jnp.sum(p, axis=-1, keepdims=True)
                outs.append(_dot(p.astype(BF16), vt) / l)
                lses.append(jnp.broadcast_to(m + jnp.log(l), (SWA_BLOCK, LANES)))
            o_ref[0, rows, cols] = jnp.where(low, outs[0], outs[1])
            lse_ref[0, rows, cols] = jnp.where(low, lses[0], lses[1])


def _swa(q, k, v, dilation):
    b, l, w = q.shape
    step = SWA_SUB * SWA_BLOCK
    assert (l // dilation) % step == 0
    cur = pl.BlockSpec((1, step, w), lambda bi, n: (bi, n, 0))
    prev = pl.BlockSpec((1, SWA_BLOCK, w), lambda bi, n: (bi, jnp.maximum(SWA_SUB * n - 1, 0), 0))
    out = jax.ShapeDtypeStruct((b, l, w), F32)
    return pl.pallas_call(
        functools.partial(_swa_body, steps_per_seq=l // dilation // step),
        grid=(b, l // step),
        in_specs=[cur, prev, cur, prev, cur],
        out_specs=[cur, cur],
        out_shape=[out, out],
        compiler_params=_params(("parallel", "arbitrary")),
        name="swa",
    )(q, k, k, v, v)


def _merge_body(x_ref, *refs, tm):
    pairs, wo_ref, out_ref, slab = refs[:2 * N_BRANCHES], refs[2 * N_BRANCHES], refs[-2], refs[-1]
    for a, ref in enumerate(pairs[2:]):
        d = DILATIONS[1 + a // 2]
        for r in range(d):
            for j in range(BRANCH_TILES):
                slab[a * BRANCH_TILES + j, pl.ds(r, tm // d, stride=d), :] = ref[0, r, :, j * LANES:(j + 1) * LANES]
    tiles = []
    for j in range(BRANCH_TILES):
        cols = slice(j * LANES, (j + 1) * LANES)
        o0, l0 = pairs[0][0, 0, :, cols], pairs[1][0, 0, :, cols]
        o1, l1, o2, l2 = (slab[a * BRANCH_TILES + j] for a in range(4))
        m = jnp.maximum(jnp.maximum(l0, l1), l2)
        e0, e1, e2 = jnp.exp(l0 - m), jnp.exp(l1 - m), jnp.exp(l2 - m)
        tiles.append(((e0 * o0 + e1 * o1 + e2 * o2) / (e0 + e1 + e2)).astype(BF16))
    out_ref[0] = x_ref[0] + _dot(jnp.concatenate(tiles, axis=-1), wo_ref[...])


def _merge(x, os, ls, w_o, tm=512):
    b, l, _ = x.shape
    row = pl.BlockSpec((1, tm, D_MODEL), lambda bi, i: (bi, i, 0))
    specs, args = [], []
    for d, o, lse in zip(DILATIONS, os, ls):
        spec = pl.BlockSpec((1, d, tm // d, MERGED_WIDTH), lambda bi, i: (bi, 0, i, 0))
        specs += [spec, spec]
        args += [o, lse]
    return pl.pallas_call(
        functools.partial(_merge_body, tm=tm),
        grid=(b, l // tm),
        in_specs=[row] + specs + [_const_spec(w_o.shape)],
        out_specs=row,
        out_shape=jax.ShapeDtypeStruct(x.shape, F32),
        scratch_shapes=[pltpu.VMEM((2 * (N_BRANCHES - 1) * BRANCH_TILES, tm, LANES), F32)],
        compiler_params=_params(("parallel", "parallel")),
        name="merge_wo",
    )(x, *args, w_o)


def _decode_body(q_ref, kn_ref, vn_ref, k0_ref, k1_ref, k2_ref, v0_ref, v1_ref, v2_ref, o_ref, *, steps):
    scale = HEAD_DIM ** -0.5
    qi = lax.broadcasted_iota(jnp.int32, (steps, steps), 0)
    kj = lax.broadcasted_iota(jnp.int32, (steps, steps), 1)
    k_refs, v_refs = (k0_ref, k1_ref, k2_ref), (v0_ref, v1_ref, v2_ref)

    def slot(sl):
        scores, values = [], []
        for br, d in enumerate(DILATIONS):
            w = WINDOWS[br]
            q = q_ref[0, sl, br].astype(BF16)
            back = (w + lax.broadcasted_iota(jnp.int32, (steps, w), 0)
                    - lax.broadcasted_iota(jnp.int32, (steps, w), 1))
            ok = ((back & (d - 1)) == 0) & (back <= w)
            scores.append(jnp.where(ok, _dot(q, k_refs[br][0, 0, sl].astype(BF16)) * scale, NEG_INF))
            values.append((v_refs[br][0, 0, sl].astype(BF16), True))
            ok = (kj <= qi) & (((qi - kj) & (d - 1)) == 0)
            scores.append(jnp.where(ok, _dot_nt(q, kn_ref[0, sl, br].astype(BF16)) * scale, NEG_INF))
            values.append((vn_ref[0, sl, br].astype(BF16), False))
        m = functools.reduce(jnp.maximum, [jnp.max(s, axis=-1, keepdims=True) for s in scores])
        num = jnp.zeros((steps, HEAD_DIM), F32)
        den = jnp.zeros((steps, 1), F32)
        for s, (v, transposed) in zip(scores, values):
            p = jnp.exp(s - m)
            den = den + jnp.sum(p, axis=-1, keepdims=True)
            num = num + (_dot_nt(p.astype(BF16), v) if transposed else _dot(p.astype(BF16), v))
        o_ref[0, sl] = num / den

    def slots_body(it, carry):
        for k in range(DECODE_UNROLL):
            slot(it * DECODE_UNROLL + k)
        return carry

    lax.fori_loop(0, HEADS_PER_BRANCH // DECODE_UNROLL, slots_body, 0)


def _slots(t):
    b, steps = t.shape[0], t.shape[1]
    return t.reshape(b, steps, N_BRANCHES, HEADS_PER_BRANCH, HEAD_DIM).transpose(0, 3, 2, 1, 4)


def _slots_t(t):
    b, steps = t.shape[0], t.shape[1]
    return t.reshape(b, steps, N_BRANCHES, HEADS_PER_BRANCH, HEAD_DIM).transpose(0, 2, 3, 4, 1)


def _cache_t(c):
    b, past = c.shape[0], c.shape[1]
    return c.transpose(0, 2, 3, 1).reshape(b, N_BRANCHES, HEADS_PER_BRANCH, HEAD_DIM, past)


def _cache_from_t(c):
    b, past = c.shape[0], c.shape[-1]
    return c.reshape(b, N_HEADS, HEAD_DIM, past).transpose(0, 3, 1, 2)


def _decode(q, kn, vn, cache_k, cache_v):
    b, steps = q.shape[0], q.shape[1]
    past = cache_k.shape[-1]
    assert past == MAX_WINDOW
    hb = HEADS_PER_BRANCH
    new = pl.BlockSpec((1, hb, N_BRANCHES, steps, HEAD_DIM), lambda i: (i, 0, 0, 0, 0))
    windows = [pl.BlockSpec((1, 1, hb, HEAD_DIM, w), functools.partial(lambda br, last, i: (i, br, 0, 0, last),
                                                                      br, past // w - 1))
               for br, w in enumerate(WINDOWS)]
    o = pl.pallas_call(
        functools.partial(_decode_body, steps=steps),
        grid=(b,),
        in_specs=[new, new, new] + windows + windows,
        out_specs=pl.BlockSpec((1, hb, steps, HEAD_DIM), lambda i: (i, 0, 0, 0)),
        out_shape=jax.ShapeDtypeStruct((b, hb, steps, HEAD_DIM), F32),
        compiler_params=_params(("parallel",)),
        name="decode_attn",
    )(_slots(q), _slots(kn), _slots(vn), *([cache_k] * N_BRANCHES), *([cache_v] * N_BRANCHES))
    return o.transpose(0, 2, 1, 3).reshape(b * steps, MERGED_WIDTH)


def _addproj_body(x_ref, o_ref, wo_ref, out_ref):
    out_ref[...] = x_ref[...] + _dot(o_ref[...].astype(BF16), wo_ref[...])


def _addproj(x, o, w_o):
    n = x.shape[0]
    return pl.pallas_call(
        _addproj_body,
        grid=(1,),
        in_specs=[_const_spec(x.shape), _const_spec(o.shape), _const_spec(w_o.shape)],
        out_specs=pl.BlockSpec(x.shape, lambda i: (0, 0)),
        out_shape=jax.ShapeDtypeStruct((n, D_MODEL), F32),
        compiler_params=_params(("arbitrary",)),
        name="add_wo",
    )(x, o, w_o)


def kernel(x_prompt, x_sample, state_ssm, cache_k, cache_v, ffn_norm1, ffn_w1_in, ffn_w1_out, ffn_norm2,
           ffn_w2_in, ffn_w2_out, mix_norm, ssm_a_re, ssm_a_im, ssm_log_dt, ssm_b_re, ssm_b_im, ssm_c_re,
           ssm_c_im, ssm_d, ssm_w_glu, kv_norm, w_kv, w_q, w_o, final_norm):
    bp, sp, _ = x_prompt.shape
    bs, ss, _ = x_sample.shape
    tm_scan = 256

    w1_in, w1_out, w2_in, w2_out = (w.astype(BF16) for w in (ffn_w1_in, ffn_w1_out, ffn_w2_in, ffn_w2_out))
    w_glu, w_kv_b, w_q_b, w_o_b = (ssm_w_glu[0].astype(BF16), w_kv.astype(BF16), w_q[0].astype(BF16),
                                   w_o[0].astype(BF16))
    bbd, cbd, (pw_p, pw_s) = _s5_weights(ssm_a_re[0], ssm_a_im[0], ssm_log_dt[0], ssm_b_re[0], ssm_b_im[0],
                                         ssm_c_re[0], ssm_c_im[0], (_scan_powers(tm_scan), [1]))

    def mixer(x, h0, pw, tm, seg):
        return _s5(x, *h0, mix_norm[0], bbd, cbd, pw, ssm_d[0], w_glu, tm=tm, seg=seg)

    pos_s = jnp.tile(PAST_LEN + jnp.arange(ss, dtype=jnp.int32), bs)
    tab_s = _rope_tables(pos_s)
    h0_s = tuple(t[None] for t in _state_to_slabs(state_ssm[0]))
    xs = _ffn(x_sample.reshape(bs * ss, D_MODEL), ffn_norm1, w1_in, w1_out, 0)
    xs, hl_re, hl_im = mixer(xs.reshape(1, bs * ss, D_MODEL), h0_s, pw_s, bs * ss, ss)
    xs = _ffn(xs.reshape(bs * ss, D_MODEL), ffn_norm2, w2_in, w2_out, 0)
    state_sample = _slabs_to_state(hl_re[0], hl_im[0])[None]
    k_s, v_s = (t.reshape(bs, ss, N_HEADS, HEAD_DIM) for t in _proj(xs, kv_norm, w_kv_b, tab_s, ATTN_WIDTH, 2))
    ck_t, cv_t, kn_t, vn_t = _cache_t(cache_k), _cache_t(cache_v), _slots_t(k_s), _slots_t(v_s)
    half = bs // 2
    tm_roll = bp * sp // half

    tab_p = _rope_tables(jnp.arange(sp, dtype=jnp.int32))
    zeros = jnp.zeros((bp, SSM_SLABS, 1, LANES), F32)
    xp, k_roll = _ffn_roll(x_prompt.reshape(bp * sp, D_MODEL), ffn_norm1, w1_in, w1_out, 0, ck_t, kn_t, None, 0,
                           tm=tm_roll)
    xp, hl_re, hl_im = mixer(xp.reshape(bp, sp, D_MODEL), (zeros, zeros), pw_p, tm_scan, tm_scan)
    xp, k_roll = _ffn_roll(xp.reshape(bp * sp, D_MODEL), ffn_norm2, w2_in, w2_out, 0, ck_t, kn_t, k_roll, half,
                           tm=tm_roll)
    cache_k_sample = _cache_from_t(k_roll)
    state_prompt = _slabs_to_state(hl_re[:, :, 0].transpose(1, 0, 2), hl_im[:, :, 0].transpose(1, 0, 2))[None]
    keep_p = min(MAX_WINDOW, sp)
    *kv_d, kt_p, vt_p = _proj_heads(xp.reshape(bp, sp, D_MODEL), kv_norm, w_kv_b, tab_p, 2, keep=keep_p)
    cache_k_prompt = kt_p.reshape(bp, N_HEADS, HEAD_DIM, keep_p).transpose(0, 3, 1, 2)
    cache_v_prompt = vt_p.reshape(bp, N_HEADS, HEAD_DIM, keep_p).transpose(0, 3, 1, 2)
    xp, v_roll = _ffn_roll(xp, ffn_norm1, w1_in, w1_out, 1, cv_t, vn_t, None, 0, tm=tm_roll)
    xp = xp.reshape(bp, sp, D_MODEL)
    q_d = _proj_heads(xp, mix_norm[1], w_q_b, tab_p, 1)
    outs, lses = [], []
    for br, d in enumerate(DILATIONS):
        flat = lambda t: t.reshape(bp, sp, MERGED_WIDTH)
        o, lse = _swa(flat(q_d[br]), flat(kv_d[br]), flat(kv_d[N_BRANCHES + br]), d)
        outs.append(o.reshape(bp, d, sp // d, MERGED_WIDTH))
        lses.append(lse.reshape(bp, d, sp // d, MERGED_WIDTH))
    xp = _merge(xp, outs, lses, w_o_b).reshape(bp * sp, D_MODEL)
    y_prompt, v_roll = _ffn_roll(xp, ffn_norm2, w2_in, w2_out, 1, cv_t, vn_t, v_roll, half, final_g=final_norm,
                                 tm=tm_roll)
    y_prompt = y_prompt.reshape(bp, sp, D_MODEL)
    cache_v_sample = _cache_from_t(v_roll)

    xs = _ffn(xs, ffn_norm1, w1_in, w1_out, 1)
    q_s = _proj(xs, mix_norm[1], w_q_b, tab_s, ATTN_WIDTH, 1)[0].reshape(bs, ss, N_HEADS, HEAD_DIM)
    xs = _addproj(xs, _decode(q_s, k_s, v_s, ck_t, cv_t), w_o_b)
    y_sample = _ffn(xs, ffn_norm2, w2_in, w2_out, 1, final_g=final_norm).reshape(bs, ss, D_MODEL)

    return (y_prompt, y_sample, state_prompt, cache_k_prompt, cache_v_prompt,
            state_sample, cache_k_sample, cache_v_sample)
```

```python
import functools

import jax
import jax.numpy as jnp
from jax import lax
from jax.experimental import pallas as pl
from jax.experimental.pallas import tpu as pltpu

F32 = jnp.float32
BF16 = jnp.bfloat16

D_MODEL = 1024
D_FF = 2816
SSM_GROUP = 16
N_SSM_GROUPS = D_MODEL // SSM_GROUP
SSM_STATE = 64
HEAD_DIM = 64
WINDOWS = (128, 512, 2048)
DILATIONS = (1, 4, 16)
N_BRANCHES = 3
HEADS_PER_BRANCH = 8
N_HEADS = N_BRANCHES * HEADS_PER_BRANCH
ATTN_WIDTH = N_HEADS * HEAD_DIM
MERGED_WIDTH = HEADS_PER_BRANCH * HEAD_DIM
MAX_WINDOW = max(WINDOWS)
ROT_DIM = HEAD_DIM // 4
ROPE_THETA = 500000.0
PAST_LEN = 16384
RMS_EPS = 1e-6
NEG_INF = -1e30

LANES = 128
SUBLANES = 8
BRANCH_TILES = MERGED_WIDTH // LANES

SSM_BLOCKS = 4
SSM_GROUPS_PER_BLOCK = N_SSM_GROUPS // SSM_BLOCKS
SSM_BLOCK_IN = SSM_GROUPS_PER_BLOCK * SSM_GROUP
SSM_BLOCK_HALF = SSM_GROUPS_PER_BLOCK * SSM_STATE
SSM_BLOCK_STATE = 2 * SSM_BLOCK_HALF
SSM_SLABS = N_SSM_GROUPS * SSM_STATE // LANES
SSM_SLABS_PER_BLOCK = SSM_SLABS // SSM_BLOCKS
SCAN_RADIX = 4
SCAN_BASE = SUBLANES
SCAN_UNROLL = 4

FF_CHUNK = 256
SWA_BLOCK = 128
SWA_SUB = 4
ROLL_GROUP = 4

VMEM_LIMIT = 56 * 1024 * 1024


def _params(semantics):
    return pltpu.CompilerParams(dimension_semantics=semantics, vmem_limit_bytes=VMEM_LIMIT)


def _const_spec(shape):
    nd = len(shape)
    return pl.BlockSpec(shape, lambda *_: (0,) * nd, pipeline_mode=pl.Buffered(1))


def _rms(x, g):
    return x * lax.rsqrt(jnp.mean(x * x, axis=-1, keepdims=True) + RMS_EPS) * g


def _dot(a, b):
    return jnp.dot(a, b, preferred_element_type=F32)


def _dot_nt(a, b):
    return lax.dot_general(a, b, (((1,), (1,)), ((), ())), preferred_element_type=F32)


def _ffn_body(x_ref, g_ref, win_ref, wout_ref, *rest, final):
    if final:
        fg_ref, o_ref = rest
    else:
        (o_ref,) = rest
    x = x_ref[...]
    h = _rms(x, g_ref[...]).astype(BF16)
    acc = x
    for j in range(D_FF // FF_CHUNK):
        lo = j * FF_CHUNK
        gate = _dot(h, win_ref[:, lo:lo + FF_CHUNK])
        up = _dot(h, win_ref[:, D_FF + lo:D_FF + lo + FF_CHUNK])
        act = gate * jax.nn.sigmoid(gate) * up
        acc = acc + _dot((0.5 * act).astype(BF16), wout_ref[lo:lo + FF_CHUNK, :])
    if final:
        acc = _rms(acc, fg_ref[...])
    o_ref[...] = acc


def _layer_spec(w, layer):
    return pl.BlockSpec((None,) + w.shape[1:], lambda *_: (layer, 0, 0), pipeline_mode=pl.Buffered(1))


def _ffn(x, g, w_in, w_out, layer, final_g=None, tm=1024):
    n = x.shape[0]
    tm = min(tm, n)
    row = pl.BlockSpec((tm, D_MODEL), lambda i: (i, 0))
    in_specs = [row, _const_spec((1, D_MODEL)), _layer_spec(w_in, layer), _layer_spec(w_out, layer)]
    args = [x, g[layer].reshape(1, D_MODEL), w_in, w_out]
    if final_g is not None:
        in_specs.append(_const_spec((1, D_MODEL)))
        args.append(final_g.reshape(1, D_MODEL))
    return pl.pallas_call(
        functools.partial(_ffn_body, final=final_g is not None),
        grid=(n // tm,),
        in_specs=in_specs,
        out_specs=row,
        out_shape=jax.ShapeDtypeStruct((n, D_MODEL), F32),
        compiler_params=_params(("parallel",)),
        name="ffn",
    )(*args)


def _ffn_roll_body(x_ref, g_ref, win_ref, wout_ref, *rest, final, aliased, batch0, past, steps):
    rest = list(rest)
    fg_ref = rest.pop(0) if final else None
    new_ref, cache_hbm = rest.pop(0), rest.pop(0)
    if aliased:
        rest.pop(0)
    o_ref, rolled_hbm, in_buf, out_buf, in_sem, out_sem = rest
    b = batch0 + pl.program_id(0)
    n_groups = N_HEADS // ROLL_GROUP
    n_chunks = D_FF // FF_CHUNK
    lane = lax.broadcasted_iota(jnp.int32, (HEAD_DIM, LANES), 1)
    tiles = past // LANES

    def copy(g, inward, seq=b):
        br, sl = divmod(g * ROLL_GROUP, HEADS_PER_BRANCH)
        half = pl.ds((g % 2) * ROLL_GROUP, ROLL_GROUP)
        if inward:
            return pltpu.make_async_copy(cache_hbm.at[seq, br, pl.ds(sl, ROLL_GROUP)], in_buf.at[half],
                                         in_sem.at[g % 2])
        return pltpu.make_async_copy(out_buf.at[half], rolled_hbm.at[seq, br, pl.ds(sl, ROLL_GROUP)],
                                     out_sem.at[g % 2])

    def arrive(g):
        copy(g, True).wait()
        if g >= 2:
            copy(g - 2, False).wait()

    def shift(g):
        for k in range(ROLL_GROUP):
            br, sl = divmod(g * ROLL_GROUP + k, HEADS_PER_BRANCH)
            src, dst = in_buf.at[(g % 2) * ROLL_GROUP + k], out_buf.at[(g % 2) * ROLL_GROUP + k]
            cur = pltpu.roll(src[:, 0:LANES], LANES - steps, 1)
            for t in range(tiles):
                if t + 1 < tiles:
                    nxt = pltpu.roll(src[:, (t + 1) * LANES:(t + 2) * LANES], LANES - steps, 1)
                else:
                    nxt = jnp.concatenate([jnp.zeros((HEAD_DIM, LANES - steps), F32), new_ref[0, br, sl]], axis=1)
                dst[:, t * LANES:(t + 1) * LANES] = jnp.where(lane < LANES - steps, cur, nxt)
                cur = nxt
        copy(g, False).start()
        if g + 2 < n_groups:
            copy(g + 2, True).start()
        else:
            @pl.when(pl.program_id(0) + 1 < pl.num_programs(0))
            def _():
                copy(g + 2 - n_groups, True, b + 1).start()

    @pl.when(pl.program_id(0) == 0)
    def _():
        copy(0, True).start()
        copy(1, True).start()

    x = x_ref[...]
    h = _rms(x, g_ref[...]).astype(BF16)
    acc = x
    group_at = {1 + g * (n_chunks - 1) // n_groups: g for g in range(n_groups)}
    for j in range(n_chunks):
        lo = j * FF_CHUNK
        if j in group_at:
            arrive(group_at[j])
        gate = _dot(h, win_ref[:, lo:lo + FF_CHUNK])
        up = _dot(h, win_ref[:, D_FF + lo:D_FF + lo + FF_CHUNK])
        if j in group_at:
            shift(group_at[j])
        act = gate * jax.nn.sigmoid(gate) * up
        acc = acc + _dot((0.5 * act).astype(BF16), wout_ref[lo:lo + FF_CHUNK, :])
    for g in range(n_groups - 2, n_groups):
        copy(g, False).wait()
    if final:
        acc = _rms(acc, fg_ref[...])
    o_ref[...] = acc


def _ffn_roll(x, g, w_in, w_out, layer, cache, new, prev, batch0, final_g=None, tm=1024):
    n = x.shape[0]
    past, steps = cache.shape[-1], new.shape[-1]
    n_groups = N_HEADS // ROLL_GROUP
    assert n % tm == 0 and batch0 + n // tm <= cache.shape[0]
    assert HEADS_PER_BRANCH % ROLL_GROUP == 0 and n_groups % 2 == 0 and n_groups < D_FF // FF_CHUNK
    row = pl.BlockSpec((tm, D_MODEL), lambda i: (i, 0))
    anywhere = pl.BlockSpec(memory_space=pl.ANY)
    in_specs = [row, _const_spec((1, D_MODEL)), _layer_spec(w_in, layer), _layer_spec(w_out, layer)]
    args = [x, g[layer].reshape(1, D_MODEL), w_in, w_out]
    if final_g is not None:
        in_specs.append(_const_spec((1, D_MODEL)))
        args.append(final_g.reshape(1, D_MODEL))
    in_specs += [pl.BlockSpec((1,) + new.shape[1:], lambda i: (batch0 + i, 0, 0, 0, 0)), anywhere]
    args += [new, cache]
    aliases = {}
    if prev is not None:
        aliases = {len(args): 1}
        in_specs.append(anywhere)
        args.append(prev)
    unit = pltpu.VMEM((2 * ROLL_GROUP, HEAD_DIM, past), F32)
    return pl.pallas_call(
        functools.partial(_ffn_roll_body, final=final_g is not None, aliased=prev is not None, batch0=batch0,
                          past=past, steps=steps),
        grid=(n // tm,),
        in_specs=in_specs,
        out_specs=[row, anywhere],
        out_shape=[jax.ShapeDtypeStruct((n, D_MODEL), F32), jax.ShapeDtypeStruct(cache.shape, F32)],
        scratch_shapes=[unit, unit, pltpu.SemaphoreType.DMA((2,)), pltpu.SemaphoreType.DMA((2,))],
        input_output_aliases=aliases,
        compiler_params=_params(("arbitrary",)),
        name="ffn_roll",
    )(*args)


def _scan_levels(tm):
    levels, n = 0, tm
    while n > 1:
        assert n % SCAN_RADIX == 0
        n //= SCAN_RADIX
        levels += 1
    return levels


def _scan_powers(tm):
    pows = [SCAN_RADIX ** lvl * m for lvl in range(_scan_levels(tm)) for m in range(1, SCAN_RADIX)]
    return pows + [tm]


def _cmul(ar, ai, br, bi):
    return ar * br - ai * bi, ar * bi + ai * br


def _blocked_scan(hre, him, pw_ref, lv_re, lv_im, slab, slot, tm):
    levels = _scan_levels(tm)
    top = SCAN_BASE - 1

    def views(lvl):
        if lvl == 0:
            return (lambda sl: hre[slab, sl, :], lambda sl: him[slab, sl, :],
                    lambda sl, v: hre.__setitem__((slab, sl, slice(None)), v),
                    lambda sl, v: him.__setitem__((slab, sl, slice(None)), v))
        a, b = lv_re[lvl - 1], lv_im[lvl - 1]
        return (lambda sl: a[slot, sl, :], lambda sl: b[slot, sl, :],
                lambda sl, v: a.__setitem__((slot, sl, slice(None)), v),
                lambda sl, v: b.__setitem__((slot, sl, slice(None)), v))

    def power(k):
        return pw_ref[slab, pl.ds(2 * k, 1), :], pw_ref[slab, pl.ds(2 * k + 1, 1), :]

    def elems(n, tau):
        return pl.ds(SCAN_BASE + tau, n, stride=SCAN_RADIX)

    s_re, s_im = hre[slab, pl.ds(top, 1), :], him[slab, pl.ds(top, 1), :]
    for lvl in range(levels):
        get_re, get_im, put_re, put_im = views(lvl)
        n = tm // SCAN_RADIX ** (lvl + 1)
        lr, li = power((SCAN_RADIX - 1) * lvl)
        pr, pi = get_re(elems(n, 0)), get_im(elems(n, 0))
        for tau in range(1, SCAN_RADIX):
            mr, mi = _cmul(lr, li, pr, pi)
            pr, pi = mr + get_re(elems(n, tau)), mi + get_im(elems(n, tau))
            put_re(elems(n, tau), pr)
            put_im(elems(n, tau), pi)
        if lvl + 1 < levels:
            _, _, nput_re, nput_im = views(lvl + 1)
            nput_re(pl.ds(SCAN_BASE, n), pr)
            nput_im(pl.ds(SCAN_BASE, n), pi)
            nput_re(pl.ds(top, 1), s_re)
            nput_im(pl.ds(top, 1), s_im)
    for lvl in reversed(range(levels)):
        get_re, get_im, put_re, put_im = views(lvl)
        n = tm // SCAN_RADIX ** (lvl + 1)
        last = lvl + 1 == levels
        if not last:
            nget_re, nget_im, _, _ = views(lvl + 1)
            put_re(elems(n, SCAN_RADIX - 1), nget_re(pl.ds(SCAN_BASE, n)))
            put_im(elems(n, SCAN_RADIX - 1), nget_im(pl.ds(SCAN_BASE, n)))
        before = pl.ds(top, n, stride=SCAN_RADIX)
        tr, ti = get_re(before), get_im(before)
        for tau in range(SCAN_RADIX if last else SCAN_RADIX - 1):
            k = (SCAN_RADIX - 1) * lvl + tau if tau < SCAN_RADIX - 1 else (SCAN_RADIX - 1) * levels
            mr, mi = _cmul(*power(k), tr, ti)
            put_re(elems(n, tau), get_re(elems(n, tau)) + mr)
            put_im(elems(n, tau), get_im(elems(n, tau)) + mi)
    hre[slab, pl.ds(top, 1), :] = hre[slab, pl.ds(top + tm, 1), :]
    him[slab, pl.ds(top, 1), :] = him[slab, pl.ds(top + tm, 1), :]


def _s5_body(x_ref, h0re_ref, h0im_ref, g_ref, bbd_ref, cbd_ref, pw_ref, d_ref, wglu_ref,
             xo_ref, hlre_ref, hlim_ref, hre, him, *levels_scr, tm, seg):
    nseq = tm // seg
    rows = pl.ds(SCAN_BASE, tm)
    top = SCAN_BASE - 1
    n_lv = len(levels_scr) // 2
    lv_re, lv_im = levels_scr[:n_lv], levels_scr[n_lv:]

    if nseq == 1:
        @pl.when(pl.program_id(1) == 0)
        def _():
            hre[:, pl.ds(top, 1), :] = h0re_ref[0]
            him[:, pl.ds(top, 1), :] = h0im_ref[0]

    x = x_ref[0]
    u = _rms(x, g_ref[...])
    ub = u.astype(BF16)

    def project_in(k):
        bu = _dot(ub[:, k * SSM_BLOCK_IN:(k + 1) * SSM_BLOCK_IN], bbd_ref[k])
        for j in range(SSM_SLABS_PER_BLOCK):
            hre[k * SSM_SLABS_PER_BLOCK + j, rows, :] = bu[:, j * LANES:(j + 1) * LANES]
            him[k * SSM_SLABS_PER_BLOCK + j, rows, :] = bu[:, SSM_BLOCK_HALF + j * LANES:
                                                          SSM_BLOCK_HALF + (j + 1) * LANES]

    def project_out(k):
        sl = range(k * SSM_SLABS_PER_BLOCK, (k + 1) * SSM_SLABS_PER_BLOCK)
        hk = jnp.concatenate([hre[p, rows, :] for p in sl] + [him[p, rows, :] for p in sl], axis=-1)
        return _dot(hk.astype(BF16), cbd_ref[k])

    if nseq == 1:
        ys = []
        project_in(0)
        for k in range(SSM_BLOCKS):
            if k + 1 < SSM_BLOCKS:
                project_in(k + 1)
            for j in range(SSM_SLABS_PER_BLOCK):
                _blocked_scan(hre, him, pw_ref, lv_re, lv_im, k * SSM_SLABS_PER_BLOCK + j, j % SCAN_UNROLL, tm)
            ys.append(project_out(k))
        hlre_ref[0] = hre[:, pl.ds(top, 1), :]
        hlim_ref[0] = him[:, pl.ds(top, 1), :]
    else:
        for k in range(SSM_BLOCKS):
            project_in(k)

        def slab_body(p, carry):
            lr, li = pw_ref[p, pl.ds(0, 1), :], pw_ref[p, pl.ds(1, 1), :]
            hr, hi = h0re_ref[0, p], h0im_ref[0, p]
            for t in range(seg):
                step = pl.ds(SCAN_BASE + t, nseq, stride=seg)
                mr, mi = _cmul(lr, li, hr, hi)
                hr, hi = mr + hre[p, step, :], mi + him[p, step, :]
                hre[p, step, :] = hr
                him[p, step, :] = hi
            hlre_ref[0, p] = hr
            hlim_ref[0, p] = hi
            return carry

        lax.fori_loop(0, SSM_SLABS, slab_body, 0)
        ys = [project_out(k) for k in range(SSM_BLOCKS)]

    y = jnp.concatenate(ys, axis=-1) + d_ref[...] * u
    z = jax.nn.gelu(y, approximate=True).astype(BF16)
    zz = _dot(z, wglu_ref[...])
    xo_ref[0] = x + zz[:, :D_MODEL] * jax.nn.sigmoid(zz[:, D_MODEL:])


def _s5_weights(a_re, a_im, log_dt, b_re, b_im, c_re, c_im, powers):
    a_re, a_im = a_re.astype(F32), a_im.astype(F32)
    dt = jnp.exp(log_dt.astype(F32))[:, None]
    mag = jnp.exp(a_re * dt)
    lre, lim = mag * jnp.cos(a_im * dt), mag * jnp.sin(a_im * dt)
    den = a_re * a_re + a_im * a_im
    fre = ((lre - 1.0) * a_re + lim * a_im) / den
    fim = (lim * a_re - (lre - 1.0) * a_im) / den
    b_re, b_im = b_re.astype(F32), b_im.astype(F32)
    bb = jnp.stack([fre[:, :, None] * b_re - fim[:, :, None] * b_im,
                    fre[:, :, None] * b_im + fim[:, :, None] * b_re])
    eye = jnp.eye(SSM_GROUPS_PER_BLOCK, dtype=F32)
    bb = bb.reshape(2, SSM_BLOCKS, SSM_GROUPS_PER_BLOCK, SSM_STATE, SSM_GROUP)
    bbd = jnp.einsum("rkgph,gq->kghrqp", bb, eye).reshape(SSM_BLOCKS, SSM_BLOCK_IN, SSM_BLOCK_STATE)
    cc = jnp.stack([c_re.astype(F32), -c_im.astype(F32)])
    cc = cc.reshape(2, SSM_BLOCKS, SSM_GROUPS_PER_BLOCK, SSM_GROUP, SSM_STATE)
    cbd = jnp.einsum("rkghp,gq->krgpqh", cc, eye).reshape(SSM_BLOCKS, SSM_BLOCK_STATE, SSM_BLOCK_IN)
    tables = []
    for pows in powers:
        k = jnp.asarray(pows, F32)[:, None, None]
        mag_k = jnp.exp(k * (a_re * dt))
        pw = jnp.stack([mag_k * jnp.cos(k * (a_im * dt)), mag_k * jnp.sin(k * (a_im * dt))], axis=1)
        tables.append(pw.reshape(2 * len(pows), SSM_SLABS, LANES).transpose(1, 0, 2))
    return bbd.astype(BF16), cbd.astype(BF16), tables


def _state_to_slabs(h):
    b = h.shape[0]
    h = h.astype(F32).reshape(b, SSM_SLABS, LANES, 2).transpose(3, 1, 0, 2)
    return h[0], h[1]


def _slabs_to_state(re, im):
    b = re.shape[1]
    h = jnp.stack([re, im], axis=-1).transpose(1, 0, 2, 3)
    return h.reshape(b, N_SSM_GROUPS, SSM_STATE, 2)


def _s5(x, h0re, h0im, g, bbd, cbd, pw, d_skip, w_glu, *, tm, seg):
    nb, rows, _ = x.shape
    nseq = tm // seg
    assert nseq == 1 or rows == tm
    row = pl.BlockSpec((1, tm, D_MODEL), lambda b, i: (b, i, 0))
    st = pl.BlockSpec((1, SSM_SLABS, nseq, LANES), lambda b, i: (b, 0, 0, 0))
    slab = pltpu.VMEM((SSM_SLABS, SCAN_BASE + tm, LANES), F32)
    levels = []
    if nseq == 1:
        levels = [pltpu.VMEM((SCAN_UNROLL, SCAN_BASE + max(tm // SCAN_RADIX ** lvl, SUBLANES), LANES), F32)
                  for lvl in range(1, _scan_levels(tm))] * 2
    st_shape = jax.ShapeDtypeStruct(h0re.shape, F32)
    return pl.pallas_call(
        functools.partial(_s5_body, tm=tm, seg=seg),
        grid=(nb, rows // tm),
        in_specs=[row, st, st, _const_spec((1, D_MODEL)), _const_spec(bbd.shape), _const_spec(cbd.shape),
                  _const_spec(pw.shape), _const_spec((1, D_MODEL)), _const_spec(w_glu.shape)],
        out_specs=[row, st, st],
        out_shape=[jax.ShapeDtypeStruct(x.shape, F32), st_shape, st_shape],
        scratch_shapes=[slab, slab] + levels,
        compiler_params=_params(("parallel", "arbitrary")),
        name="s5_mixer",
    )(x, h0re, h0im, g.reshape(1, D_MODEL), bbd, cbd, pw, d_skip.reshape(1, D_MODEL), w_glu)


def _rope_tables(pos):
    half = ROT_DIM // 2
    inv = ROPE_THETA ** (-jnp.arange(0, ROT_DIM, 2, dtype=F32) / ROT_DIM)
    ang = pos.astype(F32)[:, None] * inv[None, :]
    cos, sin = jnp.cos(ang), jnp.sin(ang)
    n = pos.shape[0]
    pad = HEAD_DIM - ROT_DIM
    c = jnp.concatenate([cos, cos, jnp.ones((n, pad), F32)], axis=-1)
    s1 = jnp.concatenate([-sin, jnp.zeros((n, half + pad), F32)], axis=-1)
    s2 = jnp.concatenate([jnp.zeros((n, half), F32), sin, jnp.zeros((n, pad), F32)], axis=-1)
    rep = LANES // HEAD_DIM
    return tuple(jnp.tile(t, (1, rep)) for t in (c, s1, s2))


def _rope_tile(y, c, s1, s2):
    half = ROT_DIM // 2
    return y * c + pltpu.roll(y, LANES - half, 1) * s1 + pltpu.roll(y, half, 1) * s2


def _proj_body(x_ref, g_ref, w_ref, c_ref, s1_ref, s2_ref, *o_refs, rope_tiles):
    h = _rms(x_ref[...], g_ref[...]).astype(BF16)
    y = _dot(h, w_ref[...])
    c, s1, s2 = c_ref[...], s1_ref[...], s2_ref[...]
    tiles_per_out = y.shape[1] // LANES // len(o_refs)
    for j in range(y.shape[1] // LANES):
        yj = y[:, j * LANES:(j + 1) * LANES]
        if j < rope_tiles:
            yj = _rope_tile(yj, c, s1, s2)
        jo = j % tiles_per_out
        o_refs[j // tiles_per_out][:, jo * LANES:(jo + 1) * LANES] = yj


def _proj(x, g, w, tables, rope_width, n_out, tm=512):
    n = x.shape[0]
    tm = min(tm, n)
    width = w.shape[1] // n_out
    row = pl.BlockSpec((tm, D_MODEL), lambda i: (i, 0))
    tab = pl.BlockSpec((tm, LANES), lambda i: (i, 0))
    return pl.pallas_call(
        functools.partial(_proj_body, rope_tiles=rope_width // LANES),
        grid=(n // tm,),
        in_specs=[row, _const_spec((1, D_MODEL)), _const_spec(w.shape), tab, tab, tab],
        out_specs=[pl.BlockSpec((tm, width), lambda i: (i, 0))] * n_out,
        out_shape=[jax.ShapeDtypeStruct((n, width), F32)] * n_out,
        compiler_params=_params(("parallel",)),
        name="proj_rope",
    )(x, g.reshape(1, D_MODEL), w, *tables)


def _proj_heads_body(x_ref, g_ref, w_ref, c_ref, s1_ref, s2_ref, *refs, n_mat, tm, cache_from):
    n_t = n_mat if cache_from is not None else 0
    outs = refs[:N_BRANCHES * n_mat]
    t_refs = refs[N_BRANCHES * n_mat:N_BRANCHES * n_mat + n_t]
    slab = refs[-1]
    h = _rms(x_ref[0], g_ref[...]).astype(BF16)
    y = _dot(h, w_ref[...])
    c, s1, s2 = c_ref[...], s1_ref[...], s2_ref[...]
    for m in range(n_mat):
        tiles = []
        for br, d in enumerate(DILATIONS):
            lo = m * ATTN_WIDTH + br * MERGED_WIDTH
            part = [y[:, lo + j * LANES:lo + (j + 1) * LANES] for j in range(BRANCH_TILES)]
            if m == 0:
                part = [_rope_tile(t, c, s1, s2) for t in part]
            tiles += part
            out = outs[N_BRANCHES * m + br]
            if d == 1:
                for j in range(BRANCH_TILES):
                    out[0, 0, :, j * LANES:(j + 1) * LANES] = part[j].astype(BF16)
                continue
            for j in range(BRANCH_TILES):
                slab[(br - 1) * BRANCH_TILES + j] = part[j]
            for r in range(d):
                for j in range(BRANCH_TILES):
                    rows = slab[(br - 1) * BRANCH_TILES + j, pl.ds(r, tm // d, stride=d), :]
                    out[0, r, :, j * LANES:(j + 1) * LANES] = rows.astype(BF16)
        if cache_from is not None:
            @pl.when(pl.program_id(1) >= cache_from)
            def _(m=m, tiles=tiles):
                t_refs[m][0] = jnp.concatenate(tiles, axis=-1).T


def _proj_heads(x, g, w, tables, n_mat, keep=None, tm=512):
    b, l, _ = x.shape
    assert w.shape[1] == n_mat * ATTN_WIDTH and tm % (DILATIONS[-1] * 2 * SUBLANES) == 0
    cache_from = None if keep is None else (l - keep) // tm
    row = pl.BlockSpec((1, tm, D_MODEL), lambda bi, i: (bi, i, 0))
    tab = pl.BlockSpec((tm, LANES), lambda bi, i: (i, 0))
    out_specs, out_shape = [], []
    for _ in range(n_mat):
        for d in DILATIONS:
            out_specs.append(pl.BlockSpec((1, d, tm // d, MERGED_WIDTH), lambda bi, i: (bi, 0, i, 0)))
            out_shape.append(jax.ShapeDtypeStruct((b, d, l // d, MERGED_WIDTH), BF16))
    if keep is not None:
        for _ in range(n_mat):
            out_specs.append(pl.BlockSpec((1, ATTN_WIDTH, tm),
                                          lambda bi, i: (bi, 0, jnp.maximum(i - cache_from, 0))))
            out_shape.append(jax.ShapeDtypeStruct((b, ATTN_WIDTH, keep), F32))
    return pl.pallas_call(
        functools.partial(_proj_heads_body, n_mat=n_mat, tm=tm, cache_from=cache_from),
        grid=(b, l // tm),
        in_specs=[row, _const_spec((1, D_MODEL)), _const_spec(w.shape), tab, tab, tab],
        out_specs=out_specs,
        out_shape=out_shape,
        scratch_shapes=[pltpu.VMEM(((N_BRANCHES - 1) * BRANCH_TILES, tm, LANES), F32)],
        compiler_params=_params(("parallel", "arbitrary")),
        name="proj_heads",
    )(x, g.reshape(1, D_MODEL), w, *tables)


def _swa_body(q_ref, kp_ref, kc_ref, vp_ref, vc_ref, o_ref, lse_ref, *, steps_per_seq):
    n = pl.program_id(1)
    k = jnp.concatenate([kp_ref[0], kc_ref[0]], axis=0)
    v = jnp.concatenate([vp_ref[0], vc_ref[0]], axis=0)
    qi = lax.broadcasted_iota(jnp.int32, (SWA_BLOCK, 2 * SWA_BLOCK), 0)
    kj = lax.broadcasted_iota(jnp.int32, (SWA_BLOCK, 2 * SWA_BLOCK), 1)
    band = (kj >= qi) & (kj <= qi + SWA_BLOCK)
    has_prev = (n % steps_per_seq) != 0
    first = band & ((kj >= SWA_BLOCK) | has_prev)
    low = lax.broadcasted_iota(jnp.int32, (SWA_BLOCK, LANES), 1) < HEAD_DIM
    zero = jnp.zeros((SWA_BLOCK, LANES), BF16)
    for j in range(SWA_SUB):
        rows = slice(j * SWA_BLOCK, (j + 1) * SWA_BLOCK)
        keys = slice(j * SWA_BLOCK, (j + 2) * SWA_BLOCK)
        valid = first if j == 0 else band
        for t in range(BRANCH_TILES):
            cols = slice(t * LANES, (t + 1) * LANES)
            q = q_ref[0, rows, cols] * (HEAD_DIM ** -0.5)
            kt, vt = k[keys, cols], v[keys, cols]
            outs, lses = [], []
            for qh in (jnp.where(low, q, zero), jnp.where(low, zero, q)):
                s = jnp.where(valid, _dot_nt(qh, kt), NEG_INF)
                m = jnp.max(s, axis=-1, keepdims=True)
                p = jnp.exp(s - m)
                l = jnp.sum(p, axis=-1, keepdims=True)
                outs.append(_dot(p.astype(BF16), vt) / l)
                lses.append(jnp.broadcast_to(m + jnp.log(l), (SWA_BLOCK, LANES)))
            o_ref[0, rows, cols] = jnp.where(low, outs[0], outs[1])
            lse_ref[0, rows, cols] = jnp.where(low, lses[0], lses[1])


def _swa(q, k, v, dilation):
    b, l, w = q.shape
    step = SWA_SUB * SWA_BLOCK
    assert (l // dilation) % step == 0
    cur = pl.BlockSpec((1, step, w), lambda bi, n: (bi, n, 0))
    prev = pl.BlockSpec((1, SWA_BLOCK, w), lambda bi, n: (bi, jnp.maximum(SWA_SUB * n - 1, 0), 0))
    out = jax.ShapeDtypeStruct((b, l, w), F32)
    return pl.pallas_call(
        functools.partial(_swa_body, steps_per_seq=l // dilation // step),
        grid=(b, l // step),
        in_specs=[cur, prev, cur, prev, cur],
        out_specs=[cur, cur],
        out_shape=[out, out],
        compiler_params=_params(("parallel", "arbitrary")),
        name="swa",
    )(q, k, k, v, v)


def _merge_body(x_ref, *refs, tm):
    pairs, wo_ref, out_ref, slab = refs[:2 * N_BRANCHES], refs[2 * N_BRANCHES], refs[-2], refs[-1]
    for a, ref in enumerate(pairs[2:]):
        d = DILATIONS[1 + a // 2]
        for r in range(d):
            for j in range(BRANCH_TILES):
                slab[a * BRANCH_TILES + j, pl.ds(r, tm // d, stride=d), :] = ref[0, r, :, j * LANES:(j + 1) * LANES]
    tiles = []
    for j in range(BRANCH_TILES):
        cols = slice(j * LANES, (j + 1) * LANES)
        o0, l0 = pairs[0][0, 0, :, cols], pairs[1][0, 0, :, cols]
        o1, l1, o2, l2 = (slab[a * BRANCH_TILES + j] for a in range(4))
        m = jnp.maximum(jnp.maximum(l0, l1), l2)
        e0, e1, e2 = jnp.exp(l0 - m), jnp.exp(l1 - m), jnp.exp(l2 - m)
        tiles.append(((e0 * o0 + e1 * o1 + e2 * o2) / (e0 + e1 + e2)).astype(BF16))
    out_ref[0] = x_ref[0] + _dot(jnp.concatenate(tiles, axis=-1), wo_ref[...])


def _merge(x, os, ls, w_o, tm=512):
    b, l, _ = x.shape
    row = pl.BlockSpec((1, tm, D_MODEL), lambda bi, i: (bi, i, 0))
    specs, args = [], []
    for d, o, lse in zip(DILATIONS, os, ls):
        spec = pl.BlockSpec((1, d, tm // d, MERGED_WIDTH), lambda bi, i: (bi, 0, i, 0))
        specs += [spec, spec]
        args += [o, lse]
    return pl.pallas_call(
        functools.partial(_merge_body, tm=tm),
        grid=(b, l // tm),
        in_specs=[row] + specs + [_const_spec(w_o.shape)],
        out_specs=row,
        out_shape=jax.ShapeDtypeStruct(x.shape, F32),
        scratch_shapes=[pltpu.VMEM((2 * (N_BRANCHES - 1) * BRANCH_TILES, tm, LANES), F32)],
        compiler_params=_params(("parallel", "parallel")),
        name="merge_wo",
    )(x, *args, w_o)


def _decode_body(q_ref, kn_ref, vn_ref, k0_ref, k1_ref, k2_ref, v0_ref, v1_ref, v2_ref, o_ref, *, steps):
    scale = HEAD_DIM ** -0.5
    qi = lax.broadcasted_iota(jnp.int32, (steps, steps), 0)
    kj = lax.broadcasted_iota(jnp.int32, (steps, steps), 1)
    k_refs, v_refs = (k0_ref, k1_ref, k2_ref), (v0_ref, v1_ref, v2_ref)

    def slot(sl):
        scores, values = [], []
        for br, d in enumerate(DILATIONS):
            w = WINDOWS[br]
            head = slice((br * HEADS_PER_BRANCH + sl) * HEAD_DIM, (br * HEADS_PER_BRANCH + sl + 1) * HEAD_DIM)
            q = q_ref[0, :, head].astype(BF16)
            back = (w + lax.broadcasted_iota(jnp.int32, (steps, w), 0)
                    - lax.broadcasted_iota(jnp.int32, (steps, w), 1))
            ok = ((back & (d - 1)) == 0) & (back <= w)
            scores.append(jnp.where(ok, _dot(q, k_refs[br][0, 0, sl].astype(BF16)) * scale, NEG_INF))
            values.append((v_refs[br][0, 0, sl].astype(BF16), True))
            ok = (kj <= qi) & (((qi - kj) & (d - 1)) == 0)
            scores.append(jnp.where(ok, _dot_nt(q, kn_ref[0, :, head].astype(BF16)) * scale, NEG_INF))
            values.append((vn_ref[0, :, head].astype(BF16), False))
        m = functools.reduce(jnp.maximum, [jnp.max(s, axis=-1, keepdims=True) for s in scores])
        num = jnp.zeros((steps, HEAD_DIM), F32)
        den = jnp.zeros((steps, 1), F32)
        for s, (v, transposed) in zip(scores, values):
            p = jnp.exp(s - m)
            den = den + jnp.sum(p, axis=-1, keepdims=True)
            num = num + (_dot_nt(p.astype(BF16), v) if transposed else _dot(p.astype(BF16), v))
        o_ref[0, :, sl * HEAD_DIM:(sl + 1) * HEAD_DIM] = num / den

    for sl in range(HEADS_PER_BRANCH):
        slot(sl)


def _slots_t(t):
    b, steps = t.shape[0], t.shape[1]
    return t.reshape(b, steps, N_BRANCHES, HEADS_PER_BRANCH, HEAD_DIM).transpose(0, 2, 3, 4, 1)


def _cache_t(c):
    b, past = c.shape[0], c.shape[1]
    return c.transpose(0, 2, 3, 1).reshape(b, N_BRANCHES, HEADS_PER_BRANCH, HEAD_DIM, past)


def _cache_from_t(c):
    b, past = c.shape[0], c.shape[-1]
    return c.reshape(b, N_HEADS, HEAD_DIM, past).transpose(0, 3, 1, 2)


def _decode(q, kn, vn, cache_k, cache_v):
    b, steps = q.shape[0], q.shape[1]
    past = cache_k.shape[-1]
    assert past == MAX_WINDOW
    hb = HEADS_PER_BRANCH
    new = pl.BlockSpec((1, steps, ATTN_WIDTH), lambda i: (i, 0, 0))
    windows = [pl.BlockSpec((1, 1, hb, HEAD_DIM, w), functools.partial(lambda br, last, i: (i, br, 0, 0, last),
                                                                      br, past // w - 1))
               for br, w in enumerate(WINDOWS)]
    o = pl.pallas_call(
        functools.partial(_decode_body, steps=steps),
        grid=(b,),
        in_specs=[new, new, new] + windows + windows,
        out_specs=pl.BlockSpec((1, steps, MERGED_WIDTH), lambda i: (i, 0, 0)),
        out_shape=jax.ShapeDtypeStruct((b, steps, MERGED_WIDTH), F32),
        compiler_params=_params(("parallel",)),
        name="decode_attn",
    )(q, kn, vn, *([cache_k] * N_BRANCHES), *([cache_v] * N_BRANCHES))
    return o.reshape(b * steps, MERGED_WIDTH)


def _addproj_body(x_ref, o_ref, wo_ref, out_ref):
    out_ref[...] = x_ref[...] + _dot(o_ref[...].astype(BF16), wo_ref[...])


def _addproj(x, o, w_o):
    n = x.shape[0]
    return pl.pallas_call(
        _addproj_body,
        grid=(1,),
        in_specs=[_const_spec(x.shape), _const_spec(o.shape), _const_spec(w_o.shape)],
        out_specs=pl.BlockSpec(x.shape, lambda i: (0, 0)),
        out_shape=jax.ShapeDtypeStruct((n, D_MODEL), F32),
        compiler_params=_params(("arbitrary",)),
        name="add_wo",
    )(x, o, w_o)


def kernel(x_prompt, x_sample, state_ssm, cache_k, cache_v, ffn_norm1, ffn_w1_in, ffn_w1_out, ffn_norm2,
           ffn_w2_in, ffn_w2_out, mix_norm, ssm_a_re, ssm_a_im, ssm_log_dt, ssm_b_re, ssm_b_im, ssm_c_re,
           ssm_c_im, ssm_d, ssm_w_glu, kv_norm, w_kv, w_q, w_o, final_norm):
    bp, sp, _ = x_prompt.shape
    bs, ss, _ = x_sample.shape
    tm_scan = 256

    w1_in, w1_out, w2_in, w2_out = (w.astype(BF16) for w in (ffn_w1_in, ffn_w1_out, ffn_w2_in, ffn_w2_out))
    w_glu, w_kv_b, w_q_b, w_o_b = (ssm_w_glu[0].astype(BF16), w_kv.astype(BF16), w_q[0].astype(BF16),
                                   w_o[0].astype(BF16))
    bbd, cbd, (pw_p, pw_s) = _s5_weights(ssm_a_re[0], ssm_a_im[0], ssm_log_dt[0], ssm_b_re[0], ssm_b_im[0],
                                         ssm_c_re[0], ssm_c_im[0], (_scan_powers(tm_scan), [1]))

    def mixer(x, h0, pw, tm, seg):
        return _s5(x, *h0, mix_norm[0], bbd, cbd, pw, ssm_d[0], w_glu, tm=tm, seg=seg)

    pos_s = jnp.tile(PAST_LEN + jnp.arange(ss, dtype=jnp.int32), bs)
    tab_s = _rope_tables(pos_s)
    h0_s = tuple(t[None] for t in _state_to_slabs(state_ssm[0]))
    xs = _ffn(x_sample.reshape(bs * ss, D_MODEL), ffn_norm1, w1_in, w1_out, 0)
    xs, hl_re, hl_im = mixer(xs.reshape(1, bs * ss, D_MODEL), h0_s, pw_s, bs * ss, ss)
    xs = _ffn(xs.reshape(bs * ss, D_MODEL), ffn_norm2, w2_in, w2_out, 0)
    state_sample = _slabs_to_state(hl_re[0], hl_im[0])[None]
    k_s, v_s = (t.reshape(bs, ss, N_HEADS, HEAD_DIM) for t in _proj(xs, kv_norm, w_kv_b, tab_s, ATTN_WIDTH, 2))
    ck_t, cv_t, kn_t, vn_t = _cache_t(cache_k), _cache_t(cache_v), _slots_t(k_s), _slots_t(v_s)
    half = bs // 2
    tm_roll = bp * sp // half

    tab_p = _rope_tables(jnp.arange(sp, dtype=jnp.int32))
    zeros = jnp.zeros((bp, SSM_SLABS, 1, LANES), F32)
    xp, k_roll = _ffn_roll(x_prompt.reshape(bp * sp, D_MODEL), ffn_norm1, w1_in, w1_out, 0, ck_t, kn_t, None, 0,
                           tm=tm_roll)
    xp, hl_re, hl_im = mixer(xp.reshape(bp, sp, D_MODEL), (zeros, zeros), pw_p, tm_scan, tm_scan)
    xp, k_roll = _ffn_roll(xp.reshape(bp * sp, D_MODEL), ffn_norm2, w2_in, w2_out, 0, ck_t, kn_t, k_roll, half,
                           tm=tm_roll)
    cache_k_sample = _cache_from_t(k_roll)
    state_prompt = _slabs_to_state(hl_re[:, :, 0].transpose(1, 0, 2), hl_im[:, :, 0].transpose(1, 0, 2))[None]
    keep_p = min(MAX_WINDOW, sp)
    *kv_d, kt_p, vt_p = _proj_heads(xp.reshape(bp, sp, D_MODEL), kv_norm, w_kv_b, tab_p, 2, keep=keep_p)
    cache_k_prompt = kt_p.reshape(bp, N_HEADS, HEAD_DIM, keep_p).transpose(0, 3, 1, 2)
    cache_v_prompt = vt_p.reshape(bp, N_HEADS, HEAD_DIM, keep_p).transpose(0, 3, 1, 2)
    xp, v_roll = _ffn_roll(xp, ffn_norm1, w1_in, w1_out, 1, cv_t, vn_t, None, 0, tm=tm_roll)
    xp = xp.reshape(bp, sp, D_MODEL)
    q_d = _proj_heads(xp, mix_norm[1], w_q_b, tab_p, 1)
    outs, lses = [], []
    for br, d in enumerate(DILATIONS):
        flat = lambda t: t.reshape(bp, sp, MERGED_WIDTH)
        o, lse = _swa(flat(q_d[br]), flat(kv_d[br]), flat(kv_d[N_BRANCHES + br]), d)
        outs.append(o.reshape(bp, d, sp // d, MERGED_WIDTH))
        lses.append(lse.reshape(bp, d, sp // d, MERGED_WIDTH))
    xp = _merge(xp, outs, lses, w_o_b).reshape(bp * sp, D_MODEL)
    y_prompt, v_roll = _ffn_roll(xp, ffn_norm2, w2_in, w2_out, 1, cv_t, vn_t, v_roll, half, final_g=final_norm,
                                 tm=tm_roll)
    y_prompt = y_prompt.reshape(bp, sp, D_MODEL)
    cache_v_sample = _cache_from_t(v_roll)

    xs = _ffn(xs, ffn_norm1, w1_in, w1_out, 1)
    q_s = _proj(xs, mix_norm[1], w_q_b, tab_s, ATTN_WIDTH, 1)[0]
    flat = lambda t: t.reshape(bs, ss, ATTN_WIDTH)
    xs = _addproj(xs, _decode(flat(q_s), flat(k_s), flat(v_s), ck_t, cv_t), w_o_b)
    y_sample = _ffn(xs, ffn_norm2, w2_in, w2_out, 1, final_g=final_norm).reshape(bs, ss, D_MODEL)

    return (y_prompt, y_sample, state_prompt, cache_k_prompt, cache_v_prompt,
            state_sample, cache_k_sample, cache_v_sample)
```

```python
import functools

import jax
import jax.numpy as jnp
from jax import lax
from jax.experimental import pallas as pl
from jax.experimental.pallas import tpu as pltpu

F32 = jnp.float32
BF16 = jnp.bfloat16

D_MODEL = 1024
D_FF = 2816
SSM_GROUP = 16
N_SSM_GROUPS = D_MODEL // SSM_GROUP
SSM_STATE = 64
HEAD_DIM = 64
WINDOWS = (128, 512, 2048)
DILATIONS = (1, 4, 16)
N_BRANCHES = 3
HEADS_PER_BRANCH = 8
N_HEADS = N_BRANCHES * HEADS_PER_BRANCH
ATTN_WIDTH = N_HEADS * HEAD_DIM
MERGED_WIDTH = HEADS_PER_BRANCH * HEAD_DIM
MAX_WINDOW = max(WINDOWS)
ROT_DIM = HEAD_DIM // 4
ROPE_THETA = 500000.0
PAST_LEN = 16384
RMS_EPS = 1e-6
NEG_INF = -1e30

LANES = 128
SUBLANES = 8
BRANCH_TILES = MERGED_WIDTH // LANES

SSM_BLOCKS = 4
SSM_GROUPS_PER_BLOCK = N_SSM_GROUPS // SSM_BLOCKS
SSM_BLOCK_IN = SSM_GROUPS_PER_BLOCK * SSM_GROUP
SSM_BLOCK_HALF = SSM_GROUPS_PER_BLOCK * SSM_STATE
SSM_BLOCK_STATE = 2 * SSM_BLOCK_HALF
SSM_SLABS = N_SSM_GROUPS * SSM_STATE // LANES
SSM_SLABS_PER_BLOCK = SSM_SLABS // SSM_BLOCKS
SCAN_RADIX = 4
SCAN_BASE = SUBLANES
SCAN_UNROLL = 4

FF_CHUNK = 256
SWA_BLOCK = 128
SWA_SUB = 4
ROLL_GROUP = 4

VMEM_LIMIT = 56 * 1024 * 1024


def _params(semantics):
    return pltpu.CompilerParams(dimension_semantics=semantics, vmem_limit_bytes=VMEM_LIMIT)


def _const_spec(shape):
    nd = len(shape)
    return pl.BlockSpec(shape, lambda *_: (0,) * nd, pipeline_mode=pl.Buffered(1))


def _rms(x, g):
    return x * lax.rsqrt(jnp.mean(x * x, axis=-1, keepdims=True) + RMS_EPS) * g


def _dot(a, b):
    return jnp.dot(a, b, preferred_element_type=F32)


def _dot_nt(a, b):
    return lax.dot_general(a, b, (((1,), (1,)), ((), ())), preferred_element_type=F32)


def _ffn_body(x_ref, g_ref, win_ref, wout_ref, *rest, final):
    if final:
        fg_ref, o_ref = rest
    else:
        (o_ref,) = rest
    x = x_ref[...]
    h = _rms(x, g_ref[...]).astype(BF16)
    acc = x
    for j in range(D_FF // FF_CHUNK):
        lo = j * FF_CHUNK
        gate = _dot(h, win_ref[:, lo:lo + FF_CHUNK])
        up = _dot(h, win_ref[:, D_FF + lo:D_FF + lo + FF_CHUNK])
        act = gate * jax.nn.sigmoid(gate) * up
        acc = acc + _dot((0.5 * act).astype(BF16), wout_ref[lo:lo + FF_CHUNK, :])
    if final:
        acc = _rms(acc, fg_ref[...])
    o_ref[...] = acc


def _layer_spec(w, layer):
    return pl.BlockSpec((None,) + w.shape[1:], lambda *_: (layer, 0, 0), pipeline_mode=pl.Buffered(1))


def _ffn(x, g, w_in, w_out, layer, final_g=None, tm=1024):
    n = x.shape[0]
    tm = min(tm, n)
    row = pl.BlockSpec((tm, D_MODEL), lambda i: (i, 0))
    in_specs = [row, _const_spec((1, D_MODEL)), _layer_spec(w_in, layer), _layer_spec(w_out, layer)]
    args = [x, g[layer].reshape(1, D_MODEL), w_in, w_out]
    if final_g is not None:
        in_specs.append(_const_spec((1, D_MODEL)))
        args.append(final_g.reshape(1, D_MODEL))
    return pl.pallas_call(
        functools.partial(_ffn_body, final=final_g is not None),
        grid=(n // tm,),
        in_specs=in_specs,
        out_specs=row,
        out_shape=jax.ShapeDtypeStruct((n, D_MODEL), F32),
        compiler_params=_params(("parallel",)),
        name="ffn",
    )(*args)


def _ffn_roll_body(x_ref, g_ref, win_ref, wout_ref, *rest, final, aliased, batch0, past, steps):
    rest = list(rest)
    fg_ref = rest.pop(0) if final else None
    new_ref, cache_hbm = rest.pop(0), rest.pop(0)
    if aliased:
        rest.pop(0)
    o_ref, rolled_hbm, in_buf, out_buf, in_sem, out_sem = rest
    b = batch0 + pl.program_id(0)
    n_groups = N_HEADS // ROLL_GROUP
    n_chunks = D_FF // FF_CHUNK
    lane = lax.broadcasted_iota(jnp.int32, (HEAD_DIM, LANES), 1)
    tiles = past // LANES

    def copy(g, inward, seq=b):
        br, sl = divmod(g * ROLL_GROUP, HEADS_PER_BRANCH)
        half = pl.ds((g % 2) * ROLL_GROUP, ROLL_GROUP)
        if inward:
            return pltpu.make_async_copy(cache_hbm.at[seq, br, pl.ds(sl, ROLL_GROUP)], in_buf.at[half],
                                         in_sem.at[g % 2])
        return pltpu.make_async_copy(out_buf.at[half], rolled_hbm.at[seq, br, pl.ds(sl, ROLL_GROUP)],
                                     out_sem.at[g % 2])

    def arrive(g):
        copy(g, True).wait()
        if g >= 2:
            copy(g - 2, False).wait()

    def shift(g):
        for k in range(ROLL_GROUP):
            br, sl = divmod(g * ROLL_GROUP + k, HEADS_PER_BRANCH)
            src, dst = in_buf.at[(g % 2) * ROLL_GROUP + k], out_buf.at[(g % 2) * ROLL_GROUP + k]
            cur = pltpu.roll(src[:, 0:LANES], LANES - steps, 1)
            for t in range(tiles):
                if t + 1 < tiles:
                    nxt = pltpu.roll(src[:, (t + 1) * LANES:(t + 2) * LANES], LANES - steps, 1)
                else:
                    head = (br * HEADS_PER_BRANCH + sl) * HEAD_DIM
                    new_t = new_ref[0, :, head:head + HEAD_DIM].T
                    nxt = jnp.concatenate([jnp.zeros((HEAD_DIM, LANES - steps), F32), new_t], axis=1)
                dst[:, t * LANES:(t + 1) * LANES] = jnp.where(lane < LANES - steps, cur, nxt)
                cur = nxt
        copy(g, False).start()
        if g + 2 < n_groups:
            copy(g + 2, True).start()
        else:
            @pl.when(pl.program_id(0) + 1 < pl.num_programs(0))
            def _():
                copy(g + 2 - n_groups, True, b + 1).start()

    @pl.when(pl.program_id(0) == 0)
    def _():
        copy(0, True).start()
        copy(1, True).start()

    x = x_ref[...]
    h = _rms(x, g_ref[...]).astype(BF16)
    acc = x
    group_at = {1 + g * (n_chunks - 1) // n_groups: g for g in range(n_groups)}
    for j in range(n_chunks):
        lo = j * FF_CHUNK
        if j in group_at:
            arrive(group_at[j])
        gate = _dot(h, win_ref[:, lo:lo + FF_CHUNK])
        up = _dot(h, win_ref[:, D_FF + lo:D_FF + lo + FF_CHUNK])
        if j in group_at:
            shift(group_at[j])
        act = gate * jax.nn.sigmoid(gate) * up
        acc = acc + _dot((0.5 * act).astype(BF16), wout_ref[lo:lo + FF_CHUNK, :])
    for g in range(n_groups - 2, n_groups):
        copy(g, False).wait()
    if final:
        acc = _rms(acc, fg_ref[...])
    o_ref[...] = acc


def _ffn_roll(x, g, w_in, w_out, layer, cache, new, prev, batch0, final_g=None, tm=1024):
    n = x.shape[0]
    past, steps = cache.shape[-1], new.shape[1]
    n_groups = N_HEADS // ROLL_GROUP
    assert n % tm == 0 and batch0 + n // tm <= cache.shape[0]
    assert HEADS_PER_BRANCH % ROLL_GROUP == 0 and n_groups % 2 == 0 and n_groups < D_FF // FF_CHUNK
    row = pl.BlockSpec((tm, D_MODEL), lambda i: (i, 0))
    anywhere = pl.BlockSpec(memory_space=pl.ANY)
    in_specs = [row, _const_spec((1, D_MODEL)), _layer_spec(w_in, layer), _layer_spec(w_out, layer)]
    args = [x, g[layer].reshape(1, D_MODEL), w_in, w_out]
    if final_g is not None:
        in_specs.append(_const_spec((1, D_MODEL)))
        args.append(final_g.reshape(1, D_MODEL))
    in_specs += [pl.BlockSpec((1,) + new.shape[1:], lambda i: (batch0 + i, 0, 0)), anywhere]
    args += [new, cache]
    aliases = {}
    if prev is not None:
        aliases = {len(args): 1}
        in_specs.append(anywhere)
        args.append(prev)
    unit = pltpu.VMEM((2 * ROLL_GROUP, HEAD_DIM, past), F32)
    return pl.pallas_call(
        functools.partial(_ffn_roll_body, final=final_g is not None, aliased=prev is not None, batch0=batch0,
                          past=past, steps=steps),
        grid=(n // tm,),
        in_specs=in_specs,
        out_specs=[row, anywhere],
        out_shape=[jax.ShapeDtypeStruct((n, D_MODEL), F32), jax.ShapeDtypeStruct(cache.shape, F32)],
        scratch_shapes=[unit, unit, pltpu.SemaphoreType.DMA((2,)), pltpu.SemaphoreType.DMA((2,))],
        input_output_aliases=aliases,
        compiler_params=_params(("arbitrary",)),
        name="ffn_roll",
    )(*args)


def _scan_levels(tm):
    levels, n = 0, tm
    while n > 1:
        assert n % SCAN_RADIX == 0
        n //= SCAN_RADIX
        levels += 1
    return levels


def _scan_powers(tm):
    pows = [SCAN_RADIX ** lvl * m for lvl in range(_scan_levels(tm)) for m in range(1, SCAN_RADIX)]
    return pows + [tm]


def _cmul(ar, ai, br, bi):
    return ar * br - ai * bi, ar * bi + ai * br


def _blocked_scan(hre, him, pw_ref, lv_re, lv_im, slab, slot, tm):
    levels = _scan_levels(tm)
    top = SCAN_BASE - 1

    def views(lvl):
        if lvl == 0:
            return (lambda sl: hre[slab, sl, :], lambda sl: him[slab, sl, :],
                    lambda sl, v: hre.__setitem__((slab, sl, slice(None)), v),
                    lambda sl, v: him.__setitem__((slab, sl, slice(None)), v))
        a, b = lv_re[lvl - 1], lv_im[lvl - 1]
        return (lambda sl: a[slot, sl, :], lambda sl: b[slot, sl, :],
                lambda sl, v: a.__setitem__((slot, sl, slice(None)), v),
                lambda sl, v: b.__setitem__((slot, sl, slice(None)), v))

    def power(k):
        return pw_ref[slab, pl.ds(2 * k, 1), :], pw_ref[slab, pl.ds(2 * k + 1, 1), :]

    def elems(n, tau):
        return pl.ds(SCAN_BASE + tau, n, stride=SCAN_RADIX)

    s_re, s_im = hre[slab, pl.ds(top, 1), :], him[slab, pl.ds(top, 1), :]
    for lvl in range(levels):
        get_re, get_im, put_re, put_im = views(lvl)
        n = tm // SCAN_RADIX ** (lvl + 1)
        lr, li = power((SCAN_RADIX - 1) * lvl)
        pr, pi = get_re(elems(n, 0)), get_im(elems(n, 0))
        for tau in range(1, SCAN_RADIX):
            mr, mi = _cmul(lr, li, pr, pi)
            pr, pi = mr + get_re(elems(n, tau)), mi + get_im(elems(n, tau))
            put_re(elems(n, tau), pr)
            put_im(elems(n, tau), pi)
        if lvl + 1 < levels:
            _, _, nput_re, nput_im = views(lvl + 1)
            nput_re(pl.ds(SCAN_BASE, n), pr)
            nput_im(pl.ds(SCAN_BASE, n), pi)
            nput_re(pl.ds(top, 1), s_re)
            nput_im(pl.ds(top, 1), s_im)
    for lvl in reversed(range(levels)):
        get_re, get_im, put_re, put_im = views(lvl)
        n = tm // SCAN_RADIX ** (lvl + 1)
        last = lvl + 1 == levels
        if not last:
            nget_re, nget_im, _, _ = views(lvl + 1)
            put_re(elems(n, SCAN_RADIX - 1), nget_re(pl.ds(SCAN_BASE, n)))
            put_im(elems(n, SCAN_RADIX - 1), nget_im(pl.ds(SCAN_BASE, n)))
        before = pl.ds(top, n, stride=SCAN_RADIX)
        tr, ti = get_re(before), get_im(before)
        for tau in range(SCAN_RADIX if last else SCAN_RADIX - 1):
            k = (SCAN_RADIX - 1) * lvl + tau if tau < SCAN_RADIX - 1 else (SCAN_RADIX - 1) * levels
            mr, mi = _cmul(*power(k), tr, ti)
            put_re(elems(n, tau), get_re(elems(n, tau)) + mr)
            put_im(elems(n, tau), get_im(elems(n, tau)) + mi)
    hre[slab, pl.ds(top, 1), :] = hre[slab, pl.ds(top + tm, 1), :]
    him[slab, pl.ds(top, 1), :] = him[slab, pl.ds(top + tm, 1), :]


def _s5_body(x_ref, h0re_ref, h0im_ref, g_ref, bbd_ref, cbd_ref, pw_ref, d_ref, wglu_ref,
             xo_ref, hlre_ref, hlim_ref, hre, him, *levels_scr, tm, seg):
    nseq = tm // seg
    rows = pl.ds(SCAN_BASE, tm)
    top = SCAN_BASE - 1
    n_lv = len(levels_scr) // 2
    lv_re, lv_im = levels_scr[:n_lv], levels_scr[n_lv:]

    if nseq == 1:
        @pl.when(pl.program_id(1) == 0)
        def _():
            hre[:, pl.ds(top, 1), :] = h0re_ref[0]
            him[:, pl.ds(top, 1), :] = h0im_ref[0]

    x = x_ref[0]
    u = _rms(x, g_ref[...])
    ub = u.astype(BF16)

    def project_in(k):
        bu = _dot(ub[:, k * SSM_BLOCK_IN:(k + 1) * SSM_BLOCK_IN], bbd_ref[k])
        for j in range(SSM_SLABS_PER_BLOCK):
            hre[k * SSM_SLABS_PER_BLOCK + j, rows, :] = bu[:, j * LANES:(j + 1) * LANES]
            him[k * SSM_SLABS_PER_BLOCK + j, rows, :] = bu[:, SSM_BLOCK_HALF + j * LANES:
                                                          SSM_BLOCK_HALF + (j + 1) * LANES]

    def project_out(k):
        sl = range(k * SSM_SLABS_PER_BLOCK, (k + 1) * SSM_SLABS_PER_BLOCK)
        hk = jnp.concatenate([hre[p, rows, :] for p in sl] + [him[p, rows, :] for p in sl], axis=-1)
        return _dot(hk.astype(BF16), cbd_ref[k])

    if nseq == 1:
        ys = []
        project_in(0)
        for k in range(SSM_BLOCKS):
            if k + 1 < SSM_BLOCKS:
                project_in(k + 1)
            for j in range(SSM_SLABS_PER_BLOCK):
                _blocked_scan(hre, him, pw_ref, lv_re, lv_im, k * SSM_SLABS_PER_BLOCK + j, j % SCAN_UNROLL, tm)
            ys.append(project_out(k))
        hlre_ref[0] = hre[:, pl.ds(top, 1), :]
        hlim_ref[0] = him[:, pl.ds(top, 1), :]
    else:
        for k in range(SSM_BLOCKS):
            project_in(k)

        def slab_body(p, carry):
            lr, li = pw_ref[p, pl.ds(0, 1), :], pw_ref[p, pl.ds(1, 1), :]
            hr, hi = h0re_ref[0, p], h0im_ref[0, p]
            for t in range(seg):
                step = pl.ds(SCAN_BASE + t, nseq, stride=seg)
                mr, mi = _cmul(lr, li, hr, hi)
                hr, hi = mr + hre[p, step, :], mi + him[p, step, :]
                hre[p, step, :] = hr
                him[p, step, :] = hi
            hlre_ref[0, p] = hr
            hlim_ref[0, p] = hi
            return carry

        lax.fori_loop(0, SSM_SLABS, slab_body, 0)
        ys = [project_out(k) for k in range(SSM_BLOCKS)]

    y = jnp.concatenate(ys, axis=-1) + d_ref[...] * u
    z = jax.nn.gelu(y, approximate=True).astype(BF16)
    zz = _dot(z, wglu_ref[...])
    xo_ref[0] = x + zz[:, :D_MODEL] * jax.nn.sigmoid(zz[:, D_MODEL:])


def _s5_weights(a_re, a_im, log_dt, b_re, b_im, c_re, c_im, powers):
    a_re, a_im = a_re.astype(F32), a_im.astype(F32)
    dt = jnp.exp(log_dt.astype(F32))[:, None]
    mag = jnp.exp(a_re * dt)
    lre, lim = mag * jnp.cos(a_im * dt), mag * jnp.sin(a_im * dt)
    den = a_re * a_re + a_im * a_im
    fre = ((lre - 1.0) * a_re + lim * a_im) / den
    fim = (lim * a_re - (lre - 1.0) * a_im) / den
    b_re, b_im = b_re.astype(F32), b_im.astype(F32)
    bb = jnp.stack([fre[:, :, None] * b_re - fim[:, :, None] * b_im,
                    fre[:, :, None] * b_im + fim[:, :, None] * b_re])
    eye = jnp.eye(SSM_GROUPS_PER_BLOCK, dtype=F32)
    bb = bb.reshape(2, SSM_BLOCKS, SSM_GROUPS_PER_BLOCK, SSM_STATE, SSM_GROUP).transpose(1, 2, 4, 0, 3)
    bbd = bb[:, :, :, :, None, :] * eye[None, :, None, None, :, None]
    bbd = bbd.reshape(SSM_BLOCKS, SSM_BLOCK_IN, SSM_BLOCK_STATE)
    cc = jnp.stack([c_re.astype(F32), -c_im.astype(F32)])
    cc = cc.reshape(2, SSM_BLOCKS, SSM_GROUPS_PER_BLOCK, SSM_GROUP, SSM_STATE).transpose(1, 0, 2, 4, 3)
    cbd = cc[:, :, :, :, None, :] * eye[None, None, :, None, :, None]
    cbd = cbd.reshape(SSM_BLOCKS, SSM_BLOCK_STATE, SSM_BLOCK_IN)
    tables = []
    for pows in powers:
        k = jnp.asarray(pows, F32)[:, None, None]
        mag_k = jnp.exp(k * (a_re * dt))
        pw = jnp.stack([mag_k * jnp.cos(k * (a_im * dt)), mag_k * jnp.sin(k * (a_im * dt))], axis=1)
        tables.append(pw.reshape(2 * len(pows), SSM_SLABS, LANES).transpose(1, 0, 2))
    return bbd.astype(BF16), cbd.astype(BF16), tables


def _state_to_slabs(h):
    b = h.shape[0]
    h = h.astype(F32).reshape(b, SSM_SLABS, LANES, 2).transpose(3, 1, 0, 2)
    return h[0], h[1]


def _slabs_to_state(re, im):
    b = re.shape[1]
    h = jnp.stack([re, im], axis=-1).transpose(1, 0, 2, 3)
    return h.reshape(b, N_SSM_GROUPS, SSM_STATE, 2)


def _s5(x, h0re, h0im, g, bbd, cbd, pw, d_skip, w_glu, *, tm, seg):
    nb, rows, _ = x.shape
    nseq = tm // seg
    assert nseq == 1 or rows == tm
    row = pl.BlockSpec((1, tm, D_MODEL), lambda b, i: (b, i, 0))
    st = pl.BlockSpec((1, SSM_SLABS, nseq, LANES), lambda b, i: (b, 0, 0, 0))
    slab = pltpu.VMEM((SSM_SLABS, SCAN_BASE + tm, LANES), F32)
    levels = []
    if nseq == 1:
        levels = [pltpu.VMEM((SCAN_UNROLL, SCAN_BASE + max(tm // SCAN_RADIX ** lvl, SUBLANES), LANES), F32)
                  for lvl in range(1, _scan_levels(tm))] * 2
    st_shape = jax.ShapeDtypeStruct(h0re.shape, F32)
    return pl.pallas_call(
        functools.partial(_s5_body, tm=tm, seg=seg),
        grid=(nb, rows // tm),
        in_specs=[row, st, st, _const_spec((1, D_MODEL)), _const_spec(bbd.shape), _const_spec(cbd.shape),
                  _const_spec(pw.shape), _const_spec((1, D_MODEL)), _const_spec(w_glu.shape)],
        out_specs=[row, st, st],
        out_shape=[jax.ShapeDtypeStruct(x.shape, F32), st_shape, st_shape],
        scratch_shapes=[slab, slab] + levels,
        compiler_params=_params(("parallel", "arbitrary")),
        name="s5_mixer",
    )(x, h0re, h0im, g.reshape(1, D_MODEL), bbd, cbd, pw, d_skip.reshape(1, D_MODEL), w_glu)


def _rope_tables(pos):
    half = ROT_DIM // 2
    inv = ROPE_THETA ** (-jnp.arange(0, ROT_DIM, 2, dtype=F32) / ROT_DIM)
    ang = pos.astype(F32)[:, None] * inv[None, :]
    cos, sin = jnp.cos(ang), jnp.sin(ang)
    n = pos.shape[0]
    pad = HEAD_DIM - ROT_DIM
    c = jnp.concatenate([cos, cos, jnp.ones((n, pad), F32)], axis=-1)
    s1 = jnp.concatenate([-sin, jnp.zeros((n, half + pad), F32)], axis=-1)
    s2 = jnp.concatenate([jnp.zeros((n, half), F32), sin, jnp.zeros((n, pad), F32)], axis=-1)
    rep = LANES // HEAD_DIM
    return tuple(jnp.tile(t, (1, rep)) for t in (c, s1, s2))


def _rope_tile(y, c, s1, s2):
    half = ROT_DIM // 2
    return y * c + pltpu.roll(y, LANES - half, 1) * s1 + pltpu.roll(y, half, 1) * s2


def _proj_body(x_ref, g_ref, w_ref, c_ref, s1_ref, s2_ref, *o_refs, rope_tiles):
    h = _rms(x_ref[...], g_ref[...]).astype(BF16)
    y = _dot(h, w_ref[...])
    c, s1, s2 = c_ref[...], s1_ref[...], s2_ref[...]
    tiles_per_out = y.shape[1] // LANES // len(o_refs)
    for j in range(y.shape[1] // LANES):
        yj = y[:, j * LANES:(j + 1) * LANES]
        if j < rope_tiles:
            yj = _rope_tile(yj, c, s1, s2)
        jo = j % tiles_per_out
        o_refs[j // tiles_per_out][:, jo * LANES:(jo + 1) * LANES] = yj


def _proj(x, g, w, tables, rope_width, n_out, tm=512):
    n = x.shape[0]
    tm = min(tm, n)
    width = w.shape[1] // n_out
    row = pl.BlockSpec((tm, D_MODEL), lambda i: (i, 0))
    tab = pl.BlockSpec((tm, LANES), lambda i: (i, 0))
    return pl.pallas_call(
        functools.partial(_proj_body, rope_tiles=rope_width // LANES),
        grid=(n // tm,),
        in_specs=[row, _const_spec((1, D_MODEL)), _const_spec(w.shape), tab, tab, tab],
        out_specs=[pl.BlockSpec((tm, width), lambda i: (i, 0))] * n_out,
        out_shape=[jax.ShapeDtypeStruct((n, width), F32)] * n_out,
        compiler_params=_params(("parallel",)),
        name="proj_rope",
    )(x, g.reshape(1, D_MODEL), w, *tables)


def _proj_heads_body(x_ref, g_ref, w_ref, c_ref, s1_ref, s2_ref, *refs, n_mat, tm, cache_from):
    n_t = n_mat if cache_from is not None else 0
    outs = refs[:N_BRANCHES * n_mat]
    t_refs = refs[N_BRANCHES * n_mat:N_BRANCHES * n_mat + n_t]
    slab = refs[-1]
    h = _rms(x_ref[0], g_ref[...]).astype(BF16)
    y = _dot(h, w_ref[...])
    c, s1, s2 = c_ref[...], s1_ref[...], s2_ref[...]
    for m in range(n_mat):
        tiles = []
        for br, d in enumerate(DILATIONS):
            lo = m * ATTN_WIDTH + br * MERGED_WIDTH
            part = [y[:, lo + j * LANES:lo + (j + 1) * LANES] for j in range(BRANCH_TILES)]
            if m == 0:
                part = [_rope_tile(t, c, s1, s2) for t in part]
            tiles += part
            out = outs[N_BRANCHES * m + br]
            if d == 1:
                for j in range(BRANCH_TILES):
                    out[0, 0, :, j * LANES:(j + 1) * LANES] = part[j].astype(BF16)
                continue
            for j in range(BRANCH_TILES):
                slab[(br - 1) * BRANCH_TILES + j] = part[j]
            for r in range(d):
                for j in range(BRANCH_TILES):
                    rows = slab[(br - 1) * BRANCH_TILES + j, pl.ds(r, tm // d, stride=d), :]
                    out[0, r, :, j * LANES:(j + 1) * LANES] = rows.astype(BF16)
        if cache_from is not None:
            @pl.when(pl.program_id(1) >= cache_from)
            def _(m=m, tiles=tiles):
                t_refs[m][0] = jnp.concatenate(tiles, axis=-1).T


def _proj_heads(x, g, w, tables, n_mat, keep=None, tm=512):
    b, l, _ = x.shape
    assert w.shape[1] == n_mat * ATTN_WIDTH and tm % (DILATIONS[-1] * 2 * SUBLANES) == 0
    cache_from = None if keep is None else (l - keep) // tm
    row = pl.BlockSpec((1, tm, D_MODEL), lambda bi, i: (bi, i, 0))
    tab = pl.BlockSpec((tm, LANES), lambda bi, i: (i, 0))
    out_specs, out_shape = [], []
    for _ in range(n_mat):
        for d in DILATIONS:
            out_specs.append(pl.BlockSpec((1, d, tm // d, MERGED_WIDTH), lambda bi, i: (bi, 0, i, 0)))
            out_shape.append(jax.ShapeDtypeStruct((b, d, l // d, MERGED_WIDTH), BF16))
    if keep is not None:
        for _ in range(n_mat):
            out_specs.append(pl.BlockSpec((1, ATTN_WIDTH, tm),
                                          lambda bi, i: (bi, 0, jnp.maximum(i - cache_from, 0))))
            out_shape.append(jax.ShapeDtypeStruct((b, ATTN_WIDTH, keep), F32))
    return pl.pallas_call(
        functools.partial(_proj_heads_body, n_mat=n_mat, tm=tm, cache_from=cache_from),
        grid=(b, l // tm),
        in_specs=[row, _const_spec((1, D_MODEL)), _const_spec(w.shape), tab, tab, tab],
        out_specs=out_specs,
        out_shape=out_shape,
        scratch_shapes=[pltpu.VMEM(((N_BRANCHES - 1) * BRANCH_TILES, tm, LANES), F32)],
        compiler_params=_params(("parallel", "arbitrary")),
        name="proj_heads",
    )(x, g.reshape(1, D_MODEL), w, *tables)


def _swa_body(q_ref, kp_ref, kc_ref, vp_ref, vc_ref, o_ref, lse_ref, *, steps_per_seq):
    n = pl.program_id(1)
    k = jnp.concatenate([kp_ref[0], kc_ref[0]], axis=0)
    v = jnp.concatenate([vp_ref[0], vc_ref[0]], axis=0)
    qi = lax.broadcasted_iota(jnp.int32, (SWA_BLOCK, 2 * SWA_BLOCK), 0)
    kj = lax.broadcasted_iota(jnp.int32, (SWA_BLOCK, 2 * SWA_BLOCK), 1)
    band = (kj >= qi) & (kj <= qi + SWA_BLOCK)
    has_prev = (n % steps_per_seq) != 0
    first = band & ((kj >= SWA_BLOCK) | has_prev)
    low = lax.broadcasted_iota(jnp.int32, (SWA_BLOCK, LANES), 1) < HEAD_DIM
    zero = jnp.zeros((SWA_BLOCK, LANES), BF16)
    for j in range(SWA_SUB):
        rows = slice(j * SWA_BLOCK, (j + 1) * SWA_BLOCK)
        keys = slice(j * SWA_BLOCK, (j + 2) * SWA_BLOCK)
        valid = first if j == 0 else band
        for t in range(BRANCH_TILES):
            cols = slice(t * LANES, (t + 1) * LANES)
            q = q_ref[0, rows, cols] * (HEAD_DIM ** -0.5)
            kt, vt = k[keys, cols], v[keys, cols]
            outs, lses = [], []
            for qh in (jnp.where(low, q, zero), jnp.where(low, zero, q)):
                s = jnp.where(valid, _dot_nt(qh, kt), NEG_INF)
                m = jnp.max(s, axis=-1, keepdims=True)
                p = jnp.exp(s - m)
                l = jnp.sum(p, axis=-1, keepdims=True)
                outs.append(_dot(p.astype(BF16), vt) / l)
                lses.append(jnp.broadcast_to(m + jnp.log(l), (SWA_BLOCK, LANES)))
            o_ref[0, rows, cols] = jnp.where(low, outs[0], outs[1])
            lse_ref[0, rows, cols] = jnp.where(low, lses[0], lses[1])


def _swa(q, k, v, dilation):
    b, l, w = q.shape
    step = SWA_SUB * SWA_BLOCK
    assert (l // dilation) % step == 0
    cur = pl.BlockSpec((1, step, w), lambda bi, n: (bi, n, 0))
    prev = pl.BlockSpec((1, SWA_BLOCK, w), lambda bi, n: (bi, jnp.maximum(SWA_SUB * n - 1, 0), 0))
    out = jax.ShapeDtypeStruct((b, l, w), F32)
    return pl.pallas_call(
        functools.partial(_swa_body, steps_per_seq=l // dilation // step),
        grid=(b, l // step),
        in_specs=[cur, prev, cur, prev, cur],
        out_specs=[cur, cur],
        out_shape=[out, out],
        compiler_params=_params(("parallel", "arbitrary")),
        name="swa",
    )(q, k, k, v, v)


def _merge_body(x_ref, *refs, tm):
    pairs, wo_ref, out_ref, slab = refs[:2 * N_BRANCHES], refs[2 * N_BRANCHES], refs[-2], refs[-1]
    for a, ref in enumerate(pairs[2:]):
        d = DILATIONS[1 + a // 2]
        for r in range(d):
            for j in range(BRANCH_TILES):
                slab[a * BRANCH_TILES + j, pl.ds(r, tm // d, stride=d), :] = ref[0, r, :, j * LANES:(j + 1) * LANES]
    tiles = []
    for j in range(BRANCH_TILES):
        cols = slice(j * LANES, (j + 1) * LANES)
        o0, l0 = pairs[0][0, 0, :, cols], pairs[1][0, 0, :, cols]
        o1, l1, o2, l2 = (slab[a * BRANCH_TILES + j] for a in range(4))
        m = jnp.maximum(jnp.maximum(l0, l1), l2)
        e0, e1, e2 = jnp.exp(l0 - m), jnp.exp(l1 - m), jnp.exp(l2 - m)
        tiles.append(((e0 * o0 + e1 * o1 + e2 * o2) / (e0 + e1 + e2)).astype(BF16))
    out_ref[0] = x_ref[0] + _dot(jnp.concatenate(tiles, axis=-1), wo_ref[...])


def _merge(x, os, ls, w_o, tm=512):
    b, l, _ = x.shape
    row = pl.BlockSpec((1, tm, D_MODEL), lambda bi, i: (bi, i, 0))
    specs, args = [], []
    for d, o, lse in zip(DILATIONS, os, ls):
        spec = pl.BlockSpec((1, d, tm // d, MERGED_WIDTH), lambda bi, i: (bi, 0, i, 0))
        specs += [spec, spec]
        args += [o, lse]
    return pl.pallas_call(
        functools.partial(_merge_body, tm=tm),
        grid=(b, l // tm),
        in_specs=[row] + specs + [_const_spec(w_o.shape)],
        out_specs=row,
        out_shape=jax.ShapeDtypeStruct(x.shape, F32),
        scratch_shapes=[pltpu.VMEM((2 * (N_BRANCHES - 1) * BRANCH_TILES, tm, LANES), F32)],
        compiler_params=_params(("parallel", "parallel")),
        name="merge_wo",
    )(x, *args, w_o)


def _decode_body(q_ref, kn_ref, vn_ref, k0_ref, k1_ref, k2_ref, v0_ref, v1_ref, v2_ref, o_ref, *, steps):
    scale = HEAD_DIM ** -0.5
    qi = lax.broadcasted_iota(jnp.int32, (steps, steps), 0)
    kj = lax.broadcasted_iota(jnp.int32, (steps, steps), 1)
    k_refs, v_refs = (k0_ref, k1_ref, k2_ref), (v0_ref, v1_ref, v2_ref)

    def slot(sl):
        scores, values = [], []
        for br, d in enumerate(DILATIONS):
            w = WINDOWS[br]
            head = slice((br * HEADS_PER_BRANCH + sl) * HEAD_DIM, (br * HEADS_PER_BRANCH + sl + 1) * HEAD_DIM)
            q = q_ref[0, :, head].astype(BF16)
            back = (w + lax.broadcasted_iota(jnp.int32, (steps, w), 0)
                    - lax.broadcasted_iota(jnp.int32, (steps, w), 1))
            ok = ((back & (d - 1)) == 0) & (back <= w)
            scores.append(jnp.where(ok, _dot(q, k_refs[br][0, 0, sl].astype(BF16)) * scale, NEG_INF))
            values.append((v_refs[br][0, 0, sl].astype(BF16), True))
            ok = (kj <= qi) & (((qi - kj) & (d - 1)) == 0)
            scores.append(jnp.where(ok, _dot_nt(q, kn_ref[0, :, head].astype(BF16)) * scale, NEG_INF))
            values.append((vn_ref[0, :, head].astype(BF16), False))
        m = functools.reduce(jnp.maximum, [jnp.max(s, axis=-1, keepdims=True) for s in scores])
        num = jnp.zeros((steps, HEAD_DIM), F32)
        den = jnp.zeros((steps, 1), F32)
        for s, (v, transposed) in zip(scores, values):
            p = jnp.exp(s - m)
            den = den + jnp.sum(p, axis=-1, keepdims=True)
            num = num + (_dot_nt(p.astype(BF16), v) if transposed else _dot(p.astype(BF16), v))
        o_ref[0, :, sl * HEAD_DIM:(sl + 1) * HEAD_DIM] = num / den

    for sl in range(HEADS_PER_BRANCH):
        slot(sl)


def _cache_t(c):
    b, past = c.shape[0], c.shape[1]
    return c.transpose(0, 2, 3, 1).reshape(b, N_BRANCHES, HEADS_PER_BRANCH, HEAD_DIM, past)


def _cache_from_t(c):
    b, past = c.shape[0], c.shape[-1]
    return c.reshape(b, N_HEADS, HEAD_DIM, past).transpose(0, 3, 1, 2)


def _decode(q, kn, vn, cache_k, cache_v):
    b, steps = q.shape[0], q.shape[1]
    past = cache_k.shape[-1]
    assert past == MAX_WINDOW
    hb = HEADS_PER_BRANCH
    new = pl.BlockSpec((1, steps, ATTN_WIDTH), lambda i: (i, 0, 0))
    windows = [pl.BlockSpec((1, 1, hb, HEAD_DIM, w), functools.partial(lambda br, last, i: (i, br, 0, 0, last),
                                                                      br, past // w - 1))
               for br, w in enumerate(WINDOWS)]
    o = pl.pallas_call(
        functools.partial(_decode_body, steps=steps),
        grid=(b,),
        in_specs=[new, new, new] + windows + windows,
        out_specs=pl.BlockSpec((1, steps, MERGED_WIDTH), lambda i: (i, 0, 0)),
        out_shape=jax.ShapeDtypeStruct((b, steps, MERGED_WIDTH), F32),
        compiler_params=_params(("parallel",)),
        name="decode_attn",
    )(q, kn, vn, *([cache_k] * N_BRANCHES), *([cache_v] * N_BRANCHES))
    return o.reshape(b * steps, MERGED_WIDTH)


def _addproj_body(x_ref, o_ref, wo_ref, out_ref):
    out_ref[...] = x_ref[...] + _dot(o_ref[...].astype(BF16), wo_ref[...])


def _addproj(x, o, w_o):
    n = x.shape[0]
    return pl.pallas_call(
        _addproj_body,
        grid=(1,),
        in_specs=[_const_spec(x.shape), _const_spec(o.shape), _const_spec(w_o.shape)],
        out_specs=pl.BlockSpec(x.shape, lambda i: (0, 0)),
        out_shape=jax.ShapeDtypeStruct((n, D_MODEL), F32),
        compiler_params=_params(("arbitrary",)),
        name="add_wo",
    )(x, o, w_o)


def kernel(x_prompt, x_sample, state_ssm, cache_k, cache_v, ffn_norm1, ffn_w1_in, ffn_w1_out, ffn_norm2,
           ffn_w2_in, ffn_w2_out, mix_norm, ssm_a_re, ssm_a_im, ssm_log_dt, ssm_b_re, ssm_b_im, ssm_c_re,
           ssm_c_im, ssm_d, ssm_w_glu, kv_norm, w_kv, w_q, w_o, final_norm):
    bp, sp, _ = x_prompt.shape
    bs, ss, _ = x_sample.shape
    tm_scan = 256

    w1_in, w1_out, w2_in, w2_out = (w.astype(BF16) for w in (ffn_w1_in, ffn_w1_out, ffn_w2_in, ffn_w2_out))
    w_glu, w_kv_b, w_q_b, w_o_b = (ssm_w_glu[0].astype(BF16), w_kv.astype(BF16), w_q[0].astype(BF16),
                                   w_o[0].astype(BF16))
    bbd, cbd, (pw_p, pw_s) = _s5_weights(ssm_a_re[0], ssm_a_im[0], ssm_log_dt[0], ssm_b_re[0], ssm_b_im[0],
                                         ssm_c_re[0], ssm_c_im[0], (_scan_powers(tm_scan), [1]))

    def mixer(x, h0, pw, tm, seg):
        return _s5(x, *h0, mix_norm[0], bbd, cbd, pw, ssm_d[0], w_glu, tm=tm, seg=seg)

    pos_s = jnp.tile(PAST_LEN + jnp.arange(ss, dtype=jnp.int32), bs)
    tab_s = _rope_tables(pos_s)
    h0_s = tuple(t[None] for t in _state_to_slabs(state_ssm[0]))
    xs = _ffn(x_sample.reshape(bs * ss, D_MODEL), ffn_norm1, w1_in, w1_out, 0)
    xs, hl_re, hl_im = mixer(xs.reshape(1, bs * ss, D_MODEL), h0_s, pw_s, bs * ss, ss)
    xs = _ffn(xs.reshape(bs * ss, D_MODEL), ffn_norm2, w2_in, w2_out, 0)
    state_sample = _slabs_to_state(hl_re[0], hl_im[0])[None]
    kn_t, vn_t = (t.reshape(bs, ss, ATTN_WIDTH) for t in _proj(xs, kv_norm, w_kv_b, tab_s, ATTN_WIDTH, 2))
    ck_t, cv_t = _cache_t(cache_k), _cache_t(cache_v)
    half = bs // 2
    tm_roll = bp * sp // half

    tab_p = _rope_tables(jnp.arange(sp, dtype=jnp.int32))
    zeros = jnp.zeros((bp, SSM_SLABS, 1, LANES), F32)
    xp, k_roll = _ffn_roll(x_prompt.reshape(bp * sp, D_MODEL), ffn_norm1, w1_in, w1_out, 0, ck_t, kn_t, None, 0,
                           tm=tm_roll)
    xp, hl_re, hl_im = mixer(xp.reshape(bp, sp, D_MODEL), (zeros, zeros), pw_p, tm_scan, tm_scan)
    xp, k_roll = _ffn_roll(xp.reshape(bp * sp, D_MODEL), ffn_norm2, w2_in, w2_out, 0, ck_t, kn_t, k_roll, half,
                           tm=tm_roll)
    cache_k_sample = _cache_from_t(k_roll)
    state_prompt = _slabs_to_state(hl_re[:, :, 0].transpose(1, 0, 2), hl_im[:, :, 0].transpose(1, 0, 2))[None]
    keep_p = min(MAX_WINDOW, sp)
    *kv_d, kt_p, vt_p = _proj_heads(xp.reshape(bp, sp, D_MODEL), kv_norm, w_kv_b, tab_p, 2, keep=keep_p)
    cache_k_prompt = kt_p.reshape(bp, N_HEADS, HEAD_DIM, keep_p).transpose(0, 3, 1, 2)
    cache_v_prompt = vt_p.reshape(bp, N_HEADS, HEAD_DIM, keep_p).transpose(0, 3, 1, 2)
    xp, v_roll = _ffn_roll(xp, ffn_norm1, w1_in, w1_out, 1, cv_t, vn_t, None, 0, tm=tm_roll)
    xp = xp.reshape(bp, sp, D_MODEL)
    q_d = _proj_heads(xp, mix_norm[1], w_q_b, tab_p, 1)
    outs, lses = [], []
    for br, d in enumerate(DILATIONS):
        flat = lambda t: t.reshape(bp, sp, MERGED_WIDTH)
        o, lse = _swa(flat(q_d[br]), flat(kv_d[br]), flat(kv_d[N_BRANCHES + br]), d)
        outs.append(o.reshape(bp, d, sp // d, MERGED_WIDTH))
        lses.append(lse.reshape(bp, d, sp // d, MERGED_WIDTH))
    xp = _merge(xp, outs, lses, w_o_b).reshape(bp * sp, D_MODEL)
    y_prompt, v_roll = _ffn_roll(xp, ffn_norm2, w2_in, w2_out, 1, cv_t, vn_t, v_roll, half, final_g=final_norm,
                                 tm=tm_roll)
    y_prompt = y_prompt.reshape(bp, sp, D_MODEL)
    cache_v_sample = _cache_from_t(v_roll)

    xs = _ffn(xs, ffn_norm1, w1_in, w1_out, 1)
    q_s = _proj(xs, mix_norm[1], w_q_b, tab_s, ATTN_WIDTH, 1)[0].reshape(bs, ss, ATTN_WIDTH)
    xs = _addproj(xs, _decode(q_s, kn_t, vn_t, ck_t, cv_t), w_o_b)
    y_sample = _ffn(xs, ffn_norm2, w2_in, w2_out, 1, final_g=final_norm).reshape(bs, ss, D_MODEL)

    return (y_prompt, y_sample, state_prompt, cache_k_prompt, cache_v_prompt,
            state_sample, cache_k_sample, cache_v_sample)
```

```python
import functools

import jax
import jax.numpy as jnp
from jax import lax
from jax.experimental import pallas as pl
from jax.experimental.pallas import tpu as pltpu

F32 = jnp.float32
BF16 = jnp.bfloat16

D_MODEL = 1024
D_FF = 2816
SSM_GROUP = 16
N_SSM_GROUPS = D_MODEL // SSM_GROUP
SSM_STATE = 64
HEAD_DIM = 64
WINDOWS = (128, 512, 2048)
DILATIONS = (1, 4, 16)
N_BRANCHES = 3
HEADS_PER_BRANCH = 8
N_HEADS = N_BRANCHES * HEADS_PER_BRANCH
ATTN_WIDTH = N_HEADS * HEAD_DIM
MERGED_WIDTH = HEADS_PER_BRANCH * HEAD_DIM
MAX_WINDOW = max(WINDOWS)
ROT_DIM = HEAD_DIM // 4
ROPE_THETA = 500000.0
PAST_LEN = 16384
RMS_EPS = 1e-6
NEG_INF = -1e30

LANES = 128
SUBLANES = 8
BRANCH_TILES = MERGED_WIDTH // LANES

SSM_BLOCKS = 4
SSM_GROUPS_PER_BLOCK = N_SSM_GROUPS // SSM_BLOCKS
SSM_BLOCK_IN = SSM_GROUPS_PER_BLOCK * SSM_GROUP
SSM_BLOCK_HALF = SSM_GROUPS_PER_BLOCK * SSM_STATE
SSM_BLOCK_STATE = 2 * SSM_BLOCK_HALF
SSM_SLABS = N_SSM_GROUPS * SSM_STATE // LANES
SSM_SLABS_PER_BLOCK = SSM_SLABS // SSM_BLOCKS
SCAN_RADIX = 4
SCAN_BASE = SUBLANES
SCAN_UNROLL = 4

FF_CHUNK = 256
SWA_BLOCK = 128
SWA_SUB = 4
ROLL_GROUP = 4

VMEM_LIMIT = 56 * 1024 * 1024


def _params(semantics):
    return pltpu.CompilerParams(dimension_semantics=semantics, vmem_limit_bytes=VMEM_LIMIT)


def _const_spec(shape):
    nd = len(shape)
    return pl.BlockSpec(shape, lambda *_: (0,) * nd, pipeline_mode=pl.Buffered(1))


def _rms(x, g):
    return x * lax.rsqrt(jnp.mean(x * x, axis=-1, keepdims=True) + RMS_EPS) * g


def _dot(a, b):
    return jnp.dot(a, b, preferred_element_type=F32)


def _dot_nt(a, b):
    return lax.dot_general(a, b, (((1,), (1,)), ((), ())), preferred_element_type=F32)


def _ffn_body(x_ref, g_ref, win_ref, wout_ref, *rest, final):
    if final:
        fg_ref, o_ref = rest
    else:
        (o_ref,) = rest
    x = x_ref[...]
    h = _rms(x, g_ref[...]).astype(BF16)
    acc = x
    for j in range(D_FF // FF_CHUNK):
        lo = j * FF_CHUNK
        gate = _dot(h, win_ref[:, lo:lo + FF_CHUNK])
        up = _dot(h, win_ref[:, D_FF + lo:D_FF + lo + FF_CHUNK])
        act = gate * jax.nn.sigmoid(gate) * up
        acc = acc + _dot((0.5 * act).astype(BF16), wout_ref[lo:lo + FF_CHUNK, :])
    if final:
        acc = _rms(acc, fg_ref[...])
    o_ref[...] = acc


def _layer_spec(w, layer):
    return pl.BlockSpec((None,) + w.shape[1:], lambda *_: (layer, 0, 0), pipeline_mode=pl.Buffered(1))


def _ffn(x, g, w_in, w_out, layer, final_g=None, tm=1024):
    n = x.shape[0]
    tm = min(tm, n)
    row = pl.BlockSpec((tm, D_MODEL), lambda i: (i, 0))
    in_specs = [row, _const_spec((1, D_MODEL)), _layer_spec(w_in, layer), _layer_spec(w_out, layer)]
    args = [x, g[layer].reshape(1, D_MODEL), w_in, w_out]
    if final_g is not None:
        in_specs.append(_const_spec((1, D_MODEL)))
        args.append(final_g.reshape(1, D_MODEL))
    return pl.pallas_call(
        functools.partial(_ffn_body, final=final_g is not None),
        grid=(n // tm,),
        in_specs=in_specs,
        out_specs=row,
        out_shape=jax.ShapeDtypeStruct((n, D_MODEL), F32),
        compiler_params=_params(("parallel",)),
        name="ffn",
    )(*args)


def _ffn_roll_body(x_ref, g_ref, win_ref, wout_ref, *rest, final, aliased, batch0, past, steps):
    rest = list(rest)
    fg_ref = rest.pop(0) if final else None
    new_ref, cache_hbm = rest.pop(0), rest.pop(0)
    if aliased:
        rest.pop(0)
    o_ref, rolled_hbm, in_buf, out_buf, in_sem, out_sem = rest
    b = batch0 + pl.program_id(0)
    n_groups = N_HEADS // ROLL_GROUP
    n_chunks = D_FF // FF_CHUNK
    lane = lax.broadcasted_iota(jnp.int32, (HEAD_DIM, LANES), 1)
    tiles = past // LANES

    def copy(g, inward, seq=b):
        br, sl = divmod(g * ROLL_GROUP, HEADS_PER_BRANCH)
        half = pl.ds((g % 2) * ROLL_GROUP, ROLL_GROUP)
        if inward:
            return pltpu.make_async_copy(cache_hbm.at[seq, br, pl.ds(sl, ROLL_GROUP)], in_buf.at[half],
                                         in_sem.at[g % 2])
        return pltpu.make_async_copy(out_buf.at[half], rolled_hbm.at[seq, br, pl.ds(sl, ROLL_GROUP)],
                                     out_sem.at[g % 2])

    def arrive(g):
        copy(g, True).wait()
        if g >= 2:
            copy(g - 2, False).wait()

    def shift(g):
        for k in range(ROLL_GROUP):
            br, sl = divmod(g * ROLL_GROUP + k, HEADS_PER_BRANCH)
            src, dst = in_buf.at[(g % 2) * ROLL_GROUP + k], out_buf.at[(g % 2) * ROLL_GROUP + k]
            cur = pltpu.roll(src[:, 0:LANES], LANES - steps, 1)
            for t in range(tiles):
                if t + 1 < tiles:
                    nxt = pltpu.roll(src[:, (t + 1) * LANES:(t + 2) * LANES], LANES - steps, 1)
                else:
                    head = (br * HEADS_PER_BRANCH + sl) * HEAD_DIM
                    new_t = new_ref[0, :, head:head + HEAD_DIM].T
                    nxt = jnp.concatenate([jnp.zeros((HEAD_DIM, LANES - steps), F32), new_t], axis=1)
                dst[:, t * LANES:(t + 1) * LANES] = jnp.where(lane < LANES - steps, cur, nxt)
                cur = nxt
        copy(g, False).start()
        if g + 2 < n_groups:
            copy(g + 2, True).start()
        else:
            @pl.when(pl.program_id(0) + 1 < pl.num_programs(0))
            def _():
                copy(g + 2 - n_groups, True, b + 1).start()

    @pl.when(pl.program_id(0) == 0)
    def _():
        copy(0, True).start()
        copy(1, True).start()

    x = x_ref[...]
    h = _rms(x, g_ref[...]).astype(BF16)
    acc = x
    group_at = {1 + g * (n_chunks - 1) // n_groups: g for g in range(n_groups)}
    for j in range(n_chunks):
        lo = j * FF_CHUNK
        if j in group_at:
            arrive(group_at[j])
        gate = _dot(h, win_ref[:, lo:lo + FF_CHUNK])
        up = _dot(h, win_ref[:, D_FF + lo:D_FF + lo + FF_CHUNK])
        if j in group_at:
            shift(group_at[j])
        act = gate * jax.nn.sigmoid(gate) * up
        acc = acc + _dot((0.5 * act).astype(BF16), wout_ref[lo:lo + FF_CHUNK, :])
    for g in range(n_groups - 2, n_groups):
        copy(g, False).wait()
    if final:
        acc = _rms(acc, fg_ref[...])
    o_ref[...] = acc


def _ffn_roll(x, g, w_in, w_out, layer, cache, new, prev, batch0, final_g=None, tm=1024):
    n = x.shape[0]
    past, steps = cache.shape[-1], new.shape[1]
    n_groups = N_HEADS // ROLL_GROUP
    assert n % tm == 0 and batch0 + n // tm <= cache.shape[0]
    assert HEADS_PER_BRANCH % ROLL_GROUP == 0 and n_groups % 2 == 0 and n_groups < D_FF // FF_CHUNK
    row = pl.BlockSpec((tm, D_MODEL), lambda i: (i, 0))
    anywhere = pl.BlockSpec(memory_space=pl.ANY)
    in_specs = [row, _const_spec((1, D_MODEL)), _layer_spec(w_in, layer), _layer_spec(w_out, layer)]
    args = [x, g[layer].reshape(1, D_MODEL), w_in, w_out]
    if final_g is not None:
        in_specs.append(_const_spec((1, D_MODEL)))
        args.append(final_g.reshape(1, D_MODEL))
    in_specs += [pl.BlockSpec((1,) + new.shape[1:], lambda i: (batch0 + i, 0, 0)), anywhere]
    args += [new, cache]
    aliases = {}
    if prev is not None:
        aliases = {len(args): 1}
        in_specs.append(anywhere)
        args.append(prev)
    unit = pltpu.VMEM((2 * ROLL_GROUP, HEAD_DIM, past), F32)
    return pl.pallas_call(
        functools.partial(_ffn_roll_body, final=final_g is not None, aliased=prev is not None, batch0=batch0,
                          past=past, steps=steps),
        grid=(n // tm,),
        in_specs=in_specs,
        out_specs=[row, anywhere],
        out_shape=[jax.ShapeDtypeStruct((n, D_MODEL), F32), jax.ShapeDtypeStruct(cache.shape, F32)],
        scratch_shapes=[unit, unit, pltpu.SemaphoreType.DMA((2,)), pltpu.SemaphoreType.DMA((2,))],
        input_output_aliases=aliases,
        compiler_params=_params(("arbitrary",)),
        name="ffn_roll",
    )(*args)


def _scan_levels(tm):
    levels, n = 0, tm
    while n > 1:
        assert n % SCAN_RADIX == 0
        n //= SCAN_RADIX
        levels += 1
    return levels


def _scan_powers(tm):
    pows = [SCAN_RADIX ** lvl * m for lvl in range(_scan_levels(tm)) for m in range(1, SCAN_RADIX)]
    return pows + [tm]


def _cmul(ar, ai, br, bi):
    return ar * br - ai * bi, ar * bi + ai * br


def _blocked_scan(hre, him, pw_ref, lv_re, lv_im, slab, slot, tm):
    levels = _scan_levels(tm)
    top = SCAN_BASE - 1

    def views(lvl):
        if lvl == 0:
            return (lambda sl: hre[slab, sl, :], lambda sl: him[slab, sl, :],
                    lambda sl, v: hre.__setitem__((slab, sl, slice(None)), v),
                    lambda sl, v: him.__setitem__((slab, sl, slice(None)), v))
        a, b = lv_re[lvl - 1], lv_im[lvl - 1]
        return (lambda sl: a[slot, sl, :], lambda sl: b[slot, sl, :],
                lambda sl, v: a.__setitem__((slot, sl, slice(None)), v),
                lambda sl, v: b.__setitem__((slot, sl, slice(None)), v))

    def power(k):
        return pw_ref[slab, pl.ds(2 * k, 1), :], pw_ref[slab, pl.ds(2 * k + 1, 1), :]

    def elems(n, tau):
        return pl.ds(SCAN_BASE + tau, n, stride=SCAN_RADIX)

    s_re, s_im = hre[slab, pl.ds(top, 1), :], him[slab, pl.ds(top, 1), :]
    for lvl in range(levels):
        get_re, get_im, put_re, put_im = views(lvl)
        n = tm // SCAN_RADIX ** (lvl + 1)
        lr, li = power((SCAN_RADIX - 1) * lvl)
        pr, pi = get_re(elems(n, 0)), get_im(elems(n, 0))
        for tau in range(1, SCAN_RADIX):
            mr, mi = _cmul(lr, li, pr, pi)
            pr, pi = mr + get_re(elems(n, tau)), mi + get_im(elems(n, tau))
            put_re(elems(n, tau), pr)
            put_im(elems(n, tau), pi)
        if lvl + 1 < levels:
            _, _, nput_re, nput_im = views(lvl + 1)
            nput_re(pl.ds(SCAN_BASE, n), pr)
            nput_im(pl.ds(SCAN_BASE, n), pi)
            nput_re(pl.ds(top, 1), s_re)
            nput_im(pl.ds(top, 1), s_im)
    for lvl in reversed(range(levels)):
        get_re, get_im, put_re, put_im = views(lvl)
        n = tm // SCAN_RADIX ** (lvl + 1)
        last = lvl + 1 == levels
        if not last:
            nget_re, nget_im, _, _ = views(lvl + 1)
            put_re(elems(n, SCAN_RADIX - 1), nget_re(pl.ds(SCAN_BASE, n)))
            put_im(elems(n, SCAN_RADIX - 1), nget_im(pl.ds(SCAN_BASE, n)))
        before = pl.ds(top, n, stride=SCAN_RADIX)
        tr, ti = get_re(before), get_im(before)
        for tau in range(SCAN_RADIX if last else SCAN_RADIX - 1):
            k = (SCAN_RADIX - 1) * lvl + tau if tau < SCAN_RADIX - 1 else (SCAN_RADIX - 1) * levels
            mr, mi = _cmul(*power(k), tr, ti)
            put_re(elems(n, tau), get_re(elems(n, tau)) + mr)
            put_im(elems(n, tau), get_im(elems(n, tau)) + mi)
    hre[slab, pl.ds(top, 1), :] = hre[slab, pl.ds(top + tm, 1), :]
    him[slab, pl.ds(top, 1), :] = him[slab, pl.ds(top + tm, 1), :]


def _s5_body(x_ref, h0re_ref, h0im_ref, g_ref, bbd_ref, cbd_ref, pw_ref, d_ref, wglu_ref,
             xo_ref, hlre_ref, hlim_ref, hre, him, *levels_scr, tm, seg):
    nseq = tm // seg
    rows = pl.ds(SCAN_BASE, tm)
    top = SCAN_BASE - 1
    n_lv = len(levels_scr) // 2
    lv_re, lv_im = levels_scr[:n_lv], levels_scr[n_lv:]

    if nseq == 1:
        @pl.when(pl.program_id(1) == 0)
        def _():
            hre[:, pl.ds(top, 1), :] = h0re_ref[0]
            him[:, pl.ds(top, 1), :] = h0im_ref[0]

    x = x_ref[0]
    u = _rms(x, g_ref[...])
    ub = u.astype(BF16)

    def project_in(k):
        bu = _dot(ub[:, k * SSM_BLOCK_IN:(k + 1) * SSM_BLOCK_IN], bbd_ref[k])
        for j in range(SSM_SLABS_PER_BLOCK):
            hre[k * SSM_SLABS_PER_BLOCK + j, rows, :] = bu[:, j * LANES:(j + 1) * LANES]
            him[k * SSM_SLABS_PER_BLOCK + j, rows, :] = bu[:, SSM_BLOCK_HALF + j * LANES:
                                                          SSM_BLOCK_HALF + (j + 1) * LANES]

    def project_out(k):
        sl = range(k * SSM_SLABS_PER_BLOCK, (k + 1) * SSM_SLABS_PER_BLOCK)
        hk = jnp.concatenate([hre[p, rows, :] for p in sl] + [him[p, rows, :] for p in sl], axis=-1)
        return _dot(hk.astype(BF16), cbd_ref[k])

    if nseq == 1:
        ys = []
        project_in(0)
        for k in range(SSM_BLOCKS):
            if k + 1 < SSM_BLOCKS:
                project_in(k + 1)
            for j in range(SSM_SLABS_PER_BLOCK):
                _blocked_scan(hre, him, pw_ref, lv_re, lv_im, k * SSM_SLABS_PER_BLOCK + j, j % SCAN_UNROLL, tm)
            ys.append(project_out(k))
        hlre_ref[0] = hre[:, pl.ds(top, 1), :]
        hlim_ref[0] = him[:, pl.ds(top, 1), :]
    else:
        for k in range(SSM_BLOCKS):
            project_in(k)

        def slab_body(p, carry):
            lr, li = pw_ref[p, pl.ds(0, 1), :], pw_ref[p, pl.ds(1, 1), :]
            hr, hi = h0re_ref[0, p], h0im_ref[0, p]
            for t in range(seg):
                step = pl.ds(SCAN_BASE + t, nseq, stride=seg)
                mr, mi = _cmul(lr, li, hr, hi)
                hr, hi = mr + hre[p, step, :], mi + him[p, step, :]
                hre[p, step, :] = hr
                him[p, step, :] = hi
            hlre_ref[0, p] = hr
            hlim_ref[0, p] = hi
            return carry

        lax.fori_loop(0, SSM_SLABS, slab_body, 0)
        ys = [project_out(k) for k in range(SSM_BLOCKS)]

    y = jnp.concatenate(ys, axis=-1) + d_ref[...] * u
    z = jax.nn.gelu(y, approximate=True).astype(BF16)
    zz = _dot(z, wglu_ref[...])
    xo_ref[0] = x + zz[:, :D_MODEL] * jax.nn.sigmoid(zz[:, D_MODEL:])


def _s5_weights(a_re, a_im, log_dt, b_re, b_im, c_re, c_im, powers):
    a_re, a_im = a_re.astype(F32), a_im.astype(F32)
    dt = jnp.exp(log_dt.astype(F32))[:, None]
    mag = jnp.exp(a_re * dt)
    lre, lim = mag * jnp.cos(a_im * dt), mag * jnp.sin(a_im * dt)
    den = a_re * a_re + a_im * a_im
    fre = ((lre - 1.0) * a_re + lim * a_im) / den
    fim = (lim * a_re - (lre - 1.0) * a_im) / den
    b_re, b_im = b_re.astype(F32), b_im.astype(F32)
    bb = jnp.stack([fre[:, :, None] * b_re - fim[:, :, None] * b_im,
                    fre[:, :, None] * b_im + fim[:, :, None] * b_re])
    nb, gb = SSM_BLOCKS, SSM_GROUPS_PER_BLOCK
    bb = bb.reshape(2, nb, gb, SSM_STATE, SSM_GROUP).transpose(1, 2, 4, 0, 3)
    bb = jnp.tile(bb.reshape(nb, SSM_BLOCK_IN, 2, SSM_STATE), (1, 1, 1, gb))
    row_g = lax.broadcasted_iota(jnp.int32, bb.shape, 1) // SSM_GROUP
    lane_g = lax.broadcasted_iota(jnp.int32, bb.shape, 3) // SSM_STATE
    bbd = jnp.where(row_g == lane_g, bb, 0.0).reshape(nb, SSM_BLOCK_IN, SSM_BLOCK_STATE)
    cc = jnp.stack([c_re.astype(F32), -c_im.astype(F32)])
    cc = cc.reshape(2, nb, gb, SSM_GROUP, SSM_STATE).transpose(1, 0, 2, 4, 3)
    cc = jnp.tile(cc.reshape(nb, SSM_BLOCK_STATE, SSM_GROUP), (1, 1, gb))
    row_g = lax.broadcasted_iota(jnp.int32, cc.shape, 1) % SSM_BLOCK_HALF // SSM_STATE
    lane_g = lax.broadcasted_iota(jnp.int32, cc.shape, 2) // SSM_GROUP
    cbd = jnp.where(row_g == lane_g, cc, 0.0)
    tables = []
    for pows in powers:
        k = jnp.asarray(pows, F32)[:, None, None]
        mag_k = jnp.exp(k * (a_re * dt))
        pw = jnp.stack([mag_k * jnp.cos(k * (a_im * dt)), mag_k * jnp.sin(k * (a_im * dt))], axis=1)
        tables.append(pw.reshape(2 * len(pows), SSM_SLABS, LANES).transpose(1, 0, 2))
    return bbd.astype(BF16), cbd.astype(BF16), tables


def _state_to_slabs(h):
    b = h.shape[0]
    h = h.astype(F32).reshape(b, SSM_SLABS, LANES, 2).transpose(3, 1, 0, 2)
    return h[0], h[1]


def _slabs_to_state(re, im):
    b = re.shape[1]
    h = jnp.stack([re, im], axis=-1).transpose(1, 0, 2, 3)
    return h.reshape(b, N_SSM_GROUPS, SSM_STATE, 2)


def _s5(x, h0re, h0im, g, bbd, cbd, pw, d_skip, w_glu, *, tm, seg):
    nb, rows, _ = x.shape
    nseq = tm // seg
    assert nseq == 1 or rows == tm
    row = pl.BlockSpec((1, tm, D_MODEL), lambda b, i: (b, i, 0))
    st = pl.BlockSpec((1, SSM_SLABS, nseq, LANES), lambda b, i: (b, 0, 0, 0))
    slab = pltpu.VMEM((SSM_SLABS, SCAN_BASE + tm, LANES), F32)
    levels = []
    if nseq == 1:
        levels = [pltpu.VMEM((SCAN_UNROLL, SCAN_BASE + max(tm // SCAN_RADIX ** lvl, SUBLANES), LANES), F32)
                  for lvl in range(1, _scan_levels(tm))] * 2
    st_shape = jax.ShapeDtypeStruct(h0re.shape, F32)
    return pl.pallas_call(
        functools.partial(_s5_body, tm=tm, seg=seg),
        grid=(nb, rows // tm),
        in_specs=[row, st, st, _const_spec((1, D_MODEL)), _const_spec(bbd.shape), _const_spec(cbd.shape),
                  _const_spec(pw.shape), _const_spec((1, D_MODEL)), _const_spec(w_glu.shape)],
        out_specs=[row, st, st],
        out_shape=[jax.ShapeDtypeStruct(x.shape, F32), st_shape, st_shape],
        scratch_shapes=[slab, slab] + levels,
        compiler_params=_params(("parallel", "arbitrary")),
        name="s5_mixer",
    )(x, h0re, h0im, g.reshape(1, D_MODEL), bbd, cbd, pw, d_skip.reshape(1, D_MODEL), w_glu)


def _rope_tables(pos):
    half = ROT_DIM // 2
    inv = ROPE_THETA ** (-jnp.arange(0, ROT_DIM, 2, dtype=F32) / ROT_DIM)
    ang = pos.astype(F32)[:, None] * inv[None, :]
    cos, sin = jnp.cos(ang), jnp.sin(ang)
    n = pos.shape[0]
    pad = HEAD_DIM - ROT_DIM
    c = jnp.concatenate([cos, cos, jnp.ones((n, pad), F32)], axis=-1)
    s1 = jnp.concatenate([-sin, jnp.zeros((n, half + pad), F32)], axis=-1)
    s2 = jnp.concatenate([jnp.zeros((n, half), F32), sin, jnp.zeros((n, pad), F32)], axis=-1)
    rep = LANES // HEAD_DIM
    return tuple(jnp.tile(t, (1, rep)) for t in (c, s1, s2))


def _rope_tile(y, c, s1, s2):
    half = ROT_DIM // 2
    return y * c + pltpu.roll(y, LANES - half, 1) * s1 + pltpu.roll(y, half, 1) * s2


def _proj_body(x_ref, g_ref, w_ref, c_ref, s1_ref, s2_ref, *o_refs, rope_tiles):
    h = _rms(x_ref[...], g_ref[...]).astype(BF16)
    y = _dot(h, w_ref[...])
    c, s1, s2 = c_ref[...], s1_ref[...], s2_ref[...]
    tiles_per_out = y.shape[1] // LANES // len(o_refs)
    for j in range(y.shape[1] // LANES):
        yj = y[:, j * LANES:(j + 1) * LANES]
        if j < rope_tiles:
            yj = _rope_tile(yj, c, s1, s2)
        jo = j % tiles_per_out
        o_refs[j // tiles_per_out][:, jo * LANES:(jo + 1) * LANES] = yj


def _proj(x, g, w, tables, rope_width, n_out, tm=512):
    n = x.shape[0]
    tm = min(tm, n)
    width = w.shape[1] // n_out
    row = pl.BlockSpec((tm, D_MODEL), lambda i: (i, 0))
    tab = pl.BlockSpec((tm, LANES), lambda i: (i, 0))
    return pl.pallas_call(
        functools.partial(_proj_body, rope_tiles=rope_width // LANES),
        grid=(n // tm,),
        in_specs=[row, _const_spec((1, D_MODEL)), _const_spec(w.shape), tab, tab, tab],
        out_specs=[pl.BlockSpec((tm, width), lambda i: (i, 0))] * n_out,
        out_shape=[jax.ShapeDtypeStruct((n, width), F32)] * n_out,
        compiler_params=_params(("parallel",)),
        name="proj_rope",
    )(x, g.reshape(1, D_MODEL), w, *tables)


def _proj_heads_body(x_ref, g_ref, w_ref, c_ref, s1_ref, s2_ref, *refs, n_mat, tm, cache_from):
    n_t = n_mat if cache_from is not None else 0
    outs = refs[:N_BRANCHES * n_mat]
    t_refs = refs[N_BRANCHES * n_mat:N_BRANCHES * n_mat + n_t]
    slab = refs[-1]
    h = _rms(x_ref[0], g_ref[...]).astype(BF16)
    y = _dot(h, w_ref[...])
    c, s1, s2 = c_ref[...], s1_ref[...], s2_ref[...]
    for m in range(n_mat):
        tiles = []
        for br, d in enumerate(DILATIONS):
            lo = m * ATTN_WIDTH + br * MERGED_WIDTH
            part = [y[:, lo + j * LANES:lo + (j + 1) * LANES] for j in range(BRANCH_TILES)]
            if m == 0:
                part = [_rope_tile(t, c, s1, s2) for t in part]
            tiles += part
            out = outs[N_BRANCHES * m + br]
            if d == 1:
                for j in range(BRANCH_TILES):
                    out[0, 0, :, j * LANES:(j + 1) * LANES] = part[j].astype(BF16)
                continue
            for j in range(BRANCH_TILES):
                slab[(br - 1) * BRANCH_TILES + j] = part[j]
            for r in range(d):
                for j in range(BRANCH_TILES):
                    rows = slab[(br - 1) * BRANCH_TILES + j, pl.ds(r, tm // d, stride=d), :]
                    out[0, r, :, j * LANES:(j + 1) * LANES] = rows.astype(BF16)
        if cache_from is not None:
            @pl.when(pl.program_id(1) >= cache_from)
            def _(m=m, tiles=tiles):
                t_refs[m][0] = jnp.concatenate(tiles, axis=-1).T


def _proj_heads(x, g, w, tables, n_mat, keep=None, tm=512):
    b, l, _ = x.shape
    assert w.shape[1] == n_mat * ATTN_WIDTH and tm % (DILATIONS[-1] * 2 * SUBLANES) == 0
    cache_from = None if keep is None else (l - keep) // tm
    row = pl.BlockSpec((1, tm, D_MODEL), lambda bi, i: (bi, i, 0))
    tab = pl.BlockSpec((tm, LANES), lambda bi, i: (i, 0))
    out_specs, out_shape = [], []
    for _ in range(n_mat):
        for d in DILATIONS:
            out_specs.append(pl.BlockSpec((1, d, tm // d, MERGED_WIDTH), lambda bi, i: (bi, 0, i, 0)))
            out_shape.append(jax.ShapeDtypeStruct((b, d, l // d, MERGED_WIDTH), BF16))
    if keep is not None:
        for _ in range(n_mat):
            out_specs.append(pl.BlockSpec((1, ATTN_WIDTH, tm),
                                          lambda bi, i: (bi, 0, jnp.maximum(i - cache_from, 0))))
            out_shape.append(jax.ShapeDtypeStruct((b, ATTN_WIDTH, keep), F32))
    return pl.pallas_call(
        functools.partial(_proj_heads_body, n_mat=n_mat, tm=tm, cache_from=cache_from),
        grid=(b, l // tm),
        in_specs=[row, _const_spec((1, D_MODEL)), _const_spec(w.shape), tab, tab, tab],
        out_specs=out_specs,
        out_shape=out_shape,
        scratch_shapes=[pltpu.VMEM(((N_BRANCHES - 1) * BRANCH_TILES, tm, LANES), F32)],
        compiler_params=_params(("parallel", "arbitrary")),
        name="proj_heads",
    )(x, g.reshape(1, D_MODEL), w, *tables)


def _swa_body(q_ref, kp_ref, kc_ref, vp_ref, vc_ref, o_ref, lse_ref, *, steps_per_seq):
    n = pl.program_id(1)
    k = jnp.concatenate([kp_ref[0], kc_ref[0]], axis=0)
    v = jnp.concatenate([vp_ref[0], vc_ref[0]], axis=0)
    qi = lax.broadcasted_iota(jnp.int32, (SWA_BLOCK, 2 * SWA_BLOCK), 0)
    kj = lax.broadcasted_iota(jnp.int32, (SWA_BLOCK, 2 * SWA_BLOCK), 1)
    band = (kj >= qi) & (kj <= qi + SWA_BLOCK)
    has_prev = (n % steps_per_seq) != 0
    first = band & ((kj >= SWA_BLOCK) | has_prev)
    low = lax.broadcasted_iota(jnp.int32, (SWA_BLOCK, LANES), 1) < HEAD_DIM
    zero = jnp.zeros((SWA_BLOCK, LANES), BF16)
    for j in range(SWA_SUB):
        rows = slice(j * SWA_BLOCK, (j + 1) * SWA_BLOCK)
        keys = slice(j * SWA_BLOCK, (j + 2) * SWA_BLOCK)
        valid = first if j == 0 else band
        for t in range(BRANCH_TILES):
            cols = slice(t * LANES, (t + 1) * LANES)
            q = q_ref[0, rows, cols] * (HEAD_DIM ** -0.5)
            kt, vt = k[keys, cols], v[keys, cols]
            outs, lses = [], []
            for qh in (jnp.where(low, q, zero), jnp.where(low, zero, q)):
                s = jnp.where(valid, _dot_nt(qh, kt), NEG_INF)
                m = jnp.max(s, axis=-1, keepdims=True)
                p = jnp.exp(s - m)
                l = jnp.sum(p, axis=-1, keepdims=True)
                outs.append(_dot(p.astype(BF16), vt) / l)
                lses.append(jnp.broadcast_to(m + jnp.log(l), (SWA_BLOCK, LANES)))
            o_ref[0, rows, cols] = jnp.where(low, outs[0], outs[1])
            lse_ref[0, rows, cols] = jnp.where(low, lses[0], lses[1])


def _swa(q, k, v, dilation):
    b, l, w = q.shape
    step = SWA_SUB * SWA_BLOCK
    assert (l // dilation) % step == 0
    cur = pl.BlockSpec((1, step, w), lambda bi, n: (bi, n, 0))
    prev = pl.BlockSpec((1, SWA_BLOCK, w), lambda bi, n: (bi, jnp.maximum(SWA_SUB * n - 1, 0), 0))
    out = jax.ShapeDtypeStruct((b, l, w), F32)
    return pl.pallas_call(
        functools.partial(_swa_body, steps_per_seq=l // dilation // step),
        grid=(b, l // step),
        in_specs=[cur, prev, cur, prev, cur],
        out_specs=[cur, cur],
        out_shape=[out, out],
        compiler_params=_params(("parallel", "arbitrary")),
        name="swa",
    )(q, k, k, v, v)


def _merge_body(x_ref, *refs, tm):
    pairs, wo_ref, out_ref, slab = refs[:2 * N_BRANCHES], refs[2 * N_BRANCHES], refs[-2], refs[-1]
    for a, ref in enumerate(pairs[2:]):
        d = DILATIONS[1 + a // 2]
        for r in range(d):
            for j in range(BRANCH_TILES):
                slab[a * BRANCH_TILES + j, pl.ds(r, tm // d, stride=d), :] = ref[0, r, :, j * LANES:(j + 1) * LANES]
    tiles = []
    for j in range(BRANCH_TILES):
        cols = slice(j * LANES, (j + 1) * LANES)
        o0, l0 = pairs[0][0, 0, :, cols], pairs[1][0, 0, :, cols]
        o1, l1, o2, l2 = (slab[a * BRANCH_TILES + j] for a in range(4))
        m = jnp.maximum(jnp.maximum(l0, l1), l2)
        e0, e1, e2 = jnp.exp(l0 - m), jnp.exp(l1 - m), jnp.exp(l2 - m)
        tiles.append(((e0 * o0 + e1 * o1 + e2 * o2) / (e0 + e1 + e2)).astype(BF16))
    out_ref[0] = x_ref[0] + _dot(jnp.concatenate(tiles, axis=-1), wo_ref[...])


def _merge(x, os, ls, w_o, tm=512):
    b, l, _ = x.shape
    row = pl.BlockSpec((1, tm, D_MODEL), lambda bi, i: (bi, i, 0))
    specs, args = [], []
    for d, o, lse in zip(DILATIONS, os, ls):
        spec = pl.BlockSpec((1, d, tm // d, MERGED_WIDTH), lambda bi, i: (bi, 0, i, 0))
        specs += [spec, spec]
        args += [o, lse]
    return pl.pallas_call(
        functools.partial(_merge_body, tm=tm),
        grid=(b, l // tm),
        in_specs=[row] + specs + [_const_spec(w_o.shape)],
        out_specs=row,
        out_shape=jax.ShapeDtypeStruct(x.shape, F32),
        scratch_shapes=[pltpu.VMEM((2 * (N_BRANCHES - 1) * BRANCH_TILES, tm, LANES), F32)],
        compiler_params=_params(("parallel", "parallel")),
        name="merge_wo",
    )(x, *args, w_o)


def _decode_body(q_ref, kn_ref, vn_ref, k0_ref, k1_ref, k2_ref, v0_ref, v1_ref, v2_ref, o_ref, *, steps):
    scale = HEAD_DIM ** -0.5
    qi = lax.broadcasted_iota(jnp.int32, (steps, steps), 0)
    kj = lax.broadcasted_iota(jnp.int32, (steps, steps), 1)
    k_refs, v_refs = (k0_ref, k1_ref, k2_ref), (v0_ref, v1_ref, v2_ref)

    def slot(sl):
        scores, values = [], []
        for br, d in enumerate(DILATIONS):
            w = WINDOWS[br]
            head = slice((br * HEADS_PER_BRANCH + sl) * HEAD_DIM, (br * HEADS_PER_BRANCH + sl + 1) * HEAD_DIM)
            q = q_ref[0, :, head].astype(BF16)
            back = (w + lax.broadcasted_iota(jnp.int32, (steps, w), 0)
                    - lax.broadcasted_iota(jnp.int32, (steps, w), 1))
            ok = ((back & (d - 1)) == 0) & (back <= w)
            scores.append(jnp.where(ok, _dot(q, k_refs[br][0, 0, sl].astype(BF16)) * scale, NEG_INF))
            values.append((v_refs[br][0, 0, sl].astype(BF16), True))
            ok = (kj <= qi) & (((qi - kj) & (d - 1)) == 0)
            scores.append(jnp.where(ok, _dot_nt(q, kn_ref[0, :, head].astype(BF16)) * scale, NEG_INF))
            values.append((vn_ref[0, :, head].astype(BF16), False))
        m = functools.reduce(jnp.maximum, [jnp.max(s, axis=-1, keepdims=True) for s in scores])
        num = jnp.zeros((steps, HEAD_DIM), F32)
        den = jnp.zeros((steps, 1), F32)
        for s, (v, transposed) in zip(scores, values):
            p = jnp.exp(s - m)
            den = den + jnp.sum(p, axis=-1, keepdims=True)
            num = num + (_dot_nt(p.astype(BF16), v) if transposed else _dot(p.astype(BF16), v))
        o_ref[0, :, sl * HEAD_DIM:(sl + 1) * HEAD_DIM] = num / den

    for sl in range(HEADS_PER_BRANCH):
        slot(sl)


def _cache_t(c):
    b, past = c.shape[0], c.shape[1]
    return c.transpose(0, 2, 3, 1).reshape(b, N_BRANCHES, HEADS_PER_BRANCH, HEAD_DIM, past)


def _cache_from_t(c):
    b, past = c.shape[0], c.shape[-1]
    return c.reshape(b, N_HEADS, HEAD_DIM, past).transpose(0, 3, 1, 2)


def _decode(q, kn, vn, cache_k, cache_v):
    b, steps = q.shape[0], q.shape[1]
    past = cache_k.shape[-1]
    assert past == MAX_WINDOW
    hb = HEADS_PER_BRANCH
    new = pl.BlockSpec((1, steps, ATTN_WIDTH), lambda i: (i, 0, 0))
    windows = [pl.BlockSpec((1, 1, hb, HEAD_DIM, w), functools.partial(lambda br, last, i: (i, br, 0, 0, last),
                                                                      br, past // w - 1))
               for br, w in enumerate(WINDOWS)]
    o = pl.pallas_call(
        functools.partial(_decode_body, steps=steps),
        grid=(b,),
        in_specs=[new, new, new] + windows + windows,
        out_specs=pl.BlockSpec((1, steps, MERGED_WIDTH), lambda i: (i, 0, 0)),
        out_shape=jax.ShapeDtypeStruct((b, steps, MERGED_WIDTH), F32),
        compiler_params=_params(("parallel",)),
        name="decode_attn",
    )(q, kn, vn, *([cache_k] * N_BRANCHES), *([cache_v] * N_BRANCHES))
    return o.reshape(b * steps, MERGED_WIDTH)


def _addproj_body(x_ref, o_ref, wo_ref, out_ref):
    out_ref[...] = x_ref[...] + _dot(o_ref[...].astype(BF16), wo_ref[...])


def _addproj(x, o, w_o):
    n = x.shape[0]
    return pl.pallas_call(
        _addproj_body,
        grid=(1,),
        in_specs=[_const_spec(x.shape), _const_spec(o.shape), _const_spec(w_o.shape)],
        out_specs=pl.BlockSpec(x.shape, lambda i: (0, 0)),
        out_shape=jax.ShapeDtypeStruct((n, D_MODEL), F32),
        compiler_params=_params(("arbitrary",)),
        name="add_wo",
    )(x, o, w_o)


def kernel(x_prompt, x_sample, state_ssm, cache_k, cache_v, ffn_norm1, ffn_w1_in, ffn_w1_out, ffn_norm2,
           ffn_w2_in, ffn_w2_out, mix_norm, ssm_a_re, ssm_a_im, ssm_log_dt, ssm_b_re, ssm_b_im, ssm_c_re,
           ssm_c_im, ssm_d, ssm_w_glu, kv_norm, w_kv, w_q, w_o, final_norm):
    bp, sp, _ = x_prompt.shape
    bs, ss, _ = x_sample.shape
    tm_scan = 256

    w1_in, w1_out, w2_in, w2_out = (w.astype(BF16) for w in (ffn_w1_in, ffn_w1_out, ffn_w2_in, ffn_w2_out))
    w_glu, w_kv_b, w_q_b, w_o_b = (ssm_w_glu[0].astype(BF16), w_kv.astype(BF16), w_q[0].astype(BF16),
                                   w_o[0].astype(BF16))
    bbd, cbd, (pw_p, pw_s) = _s5_weights(ssm_a_re[0], ssm_a_im[0], ssm_log_dt[0], ssm_b_re[0], ssm_b_im[0],
                                         ssm_c_re[0], ssm_c_im[0], (_scan_powers(tm_scan), [1]))

    def mixer(x, h0, pw, tm, seg):
        return _s5(x, *h0, mix_norm[0], bbd, cbd, pw, ssm_d[0], w_glu, tm=tm, seg=seg)

    pos_s = jnp.tile(PAST_LEN + jnp.arange(ss, dtype=jnp.int32), bs)
    tab_s = _rope_tables(pos_s)
    h0_s = tuple(t[None] for t in _state_to_slabs(state_ssm[0]))
    xs = _ffn(x_sample.reshape(bs * ss, D_MODEL), ffn_norm1, w1_in, w1_out, 0)
    xs, hl_re, hl_im = mixer(xs.reshape(1, bs * ss, D_MODEL), h0_s, pw_s, bs * ss, ss)
    xs = _ffn(xs.reshape(bs * ss, D_MODEL), ffn_norm2, w2_in, w2_out, 0)
    state_sample = _slabs_to_state(hl_re[0], hl_im[0])[None]
    kn_t, vn_t = (t.reshape(bs, ss, ATTN_WIDTH) for t in _proj(xs, kv_norm, w_kv_b, tab_s, ATTN_WIDTH, 2))
    ck_t, cv_t = _cache_t(cache_k), _cache_t(cache_v)
    half = bs // 2
    tm_roll = bp * sp // half

    tab_p = _rope_tables(jnp.arange(sp, dtype=jnp.int32))
    zeros = jnp.zeros((bp, SSM_SLABS, 1, LANES), F32)
    xp, k_roll = _ffn_roll(x_prompt.reshape(bp * sp, D_MODEL), ffn_norm1, w1_in, w1_out, 0, ck_t, kn_t, None, 0,
                           tm=tm_roll)
    xp, hl_re, hl_im = mixer(xp.reshape(bp, sp, D_MODEL), (zeros, zeros), pw_p, tm_scan, tm_scan)
    xp, k_roll = _ffn_roll(xp.reshape(bp * sp, D_MODEL), ffn_norm2, w2_in, w2_out, 0, ck_t, kn_t, k_roll, half,
                           tm=tm_roll)
    cache_k_sample = _cache_from_t(k_roll)
    state_prompt = _slabs_to_state(hl_re[:, :, 0].transpose(1, 0, 2), hl_im[:, :, 0].transpose(1, 0, 2))[None]
    keep_p = min(MAX_WINDOW, sp)
    *kv_d, kt_p, vt_p = _proj_heads(xp.reshape(bp, sp, D_MODEL), kv_norm, w_kv_b, tab_p, 2, keep=keep_p)
    cache_k_prompt = kt_p.reshape(bp, N_HEADS, HEAD_DIM, keep_p).transpose(0, 3, 1, 2)
    cache_v_prompt = vt_p.reshape(bp, N_HEADS, HEAD_DIM, keep_p).transpose(0, 3, 1, 2)
    xp, v_roll = _ffn_roll(xp, ffn_norm1, w1_in, w1_out, 1, cv_t, vn_t, None, 0, tm=tm_roll)
    xp = xp.reshape(bp, sp, D_MODEL)
    q_d = _proj_heads(xp, mix_norm[1], w_q_b, tab_p, 1)
    outs, lses = [], []
    for br, d in enumerate(DILATIONS):
        flat = lambda t: t.reshape(bp, sp, MERGED_WIDTH)
        o, lse = _swa(flat(q_d[br]), flat(kv_d[br]), flat(kv_d[N_BRANCHES + br]), d)
        outs.append(o.reshape(bp, d, sp // d, MERGED_WIDTH))
        lses.append(lse.reshape(bp, d, sp // d, MERGED_WIDTH))
    xp = _merge(xp, outs, lses, w_o_b).reshape(bp * sp, D_MODEL)
    y_prompt, v_roll = _ffn_roll(xp, ffn_norm2, w2_in, w2_out, 1, cv_t, vn_t, v_roll, half, final_g=final_norm,
                                 tm=tm_roll)
    y_prompt = y_prompt.reshape(bp, sp, D_MODEL)
    cache_v_sample = _cache_from_t(v_roll)

    xs = _ffn(xs, ffn_norm1, w1_in, w1_out, 1)
    q_s = _proj(xs, mix_norm[1], w_q_b, tab_s, ATTN_WIDTH, 1)[0].reshape(bs, ss, ATTN_WIDTH)
    xs = _addproj(xs, _decode(q_s, kn_t, vn_t, ck_t, cv_t), w_o_b)
    y_sample = _ffn(xs, ffn_norm2, w2_in, w2_out, 1, final_g=final_norm).reshape(bs, ss, D_MODEL)

    return (y_prompt, y_sample, state_prompt, cache_k_prompt, cache_v_prompt,
            state_sample, cache_k_sample, cache_v_sample)
```

```python
import functools

import jax
import jax.numpy as jnp
from jax import lax
from jax.experimental import pallas as pl
from jax.experimental.pallas import tpu as pltpu

F32 = jnp.float32
BF16 = jnp.bfloat16

D_MODEL = 1024
D_FF = 2816
SSM_GROUP = 16
N_SSM_GROUPS = D_MODEL // SSM_GROUP
SSM_STATE = 64
HEAD_DIM = 64
WINDOWS = (128, 512, 2048)
DILATIONS = (1, 4, 16)
N_BRANCHES = 3
HEADS_PER_BRANCH = 8
N_HEADS = N_BRANCHES * HEADS_PER_BRANCH
ATTN_WIDTH = N_HEADS * HEAD_DIM
MERGED_WIDTH = HEADS_PER_BRANCH * HEAD_DIM
MAX_WINDOW = max(WINDOWS)
ROT_DIM = HEAD_DIM // 4
ROPE_THETA = 500000.0
PAST_LEN = 16384
RMS_EPS = 1e-6
NEG_INF = -1e30

LANES = 128
SUBLANES = 8
BRANCH_TILES = MERGED_WIDTH // LANES

SSM_BLOCKS = 4
SSM_GROUPS_PER_BLOCK = N_SSM_GROUPS // SSM_BLOCKS
SSM_BLOCK_IN = SSM_GROUPS_PER_BLOCK * SSM_GROUP
SSM_BLOCK_HALF = SSM_GROUPS_PER_BLOCK * SSM_STATE
SSM_BLOCK_STATE = 2 * SSM_BLOCK_HALF
SSM_SLABS = N_SSM_GROUPS * SSM_STATE // LANES
SSM_SLABS_PER_BLOCK = SSM_SLABS // SSM_BLOCKS
SCAN_RADIX = 4
SCAN_BASE = SUBLANES
SCAN_UNROLL = 4

FF_CHUNK = 256
SWA_BLOCK = 128
SWA_SUB = 4
ROLL_GROUP = 4

VMEM_LIMIT = 56 * 1024 * 1024


def _params(semantics):
    return pltpu.CompilerParams(dimension_semantics=semantics, vmem_limit_bytes=VMEM_LIMIT)


def _const_spec(shape):
    nd = len(shape)
    return pl.BlockSpec(shape, lambda *_: (0,) * nd, pipeline_mode=pl.Buffered(1))


def _rms(x, g):
    return x * lax.rsqrt(jnp.mean(x * x, axis=-1, keepdims=True) + RMS_EPS) * g


def _dot(a, b):
    return jnp.dot(a, b, preferred_element_type=F32)


def _dot_nt(a, b):
    return lax.dot_general(a, b, (((1,), (1,)), ((), ())), preferred_element_type=F32)


def _ffn_body(x_ref, g_ref, win_ref, wout_ref, *rest, final):
    if final:
        fg_ref, o_ref = rest
    else:
        (o_ref,) = rest
    x = x_ref[...]
    h = _rms(x, g_ref[...]).astype(BF16)
    acc = x
    for j in range(D_FF // FF_CHUNK):
        lo = j * FF_CHUNK
        gate = _dot(h, win_ref[:, lo:lo + FF_CHUNK])
        up = _dot(h, win_ref[:, D_FF + lo:D_FF + lo + FF_CHUNK])
        act = gate * jax.nn.sigmoid(gate) * up
        acc = acc + _dot((0.5 * act).astype(BF16), wout_ref[lo:lo + FF_CHUNK, :])
    if final:
        acc = _rms(acc, fg_ref[...])
    o_ref[...] = acc


def _layer_spec(w, layer):
    return pl.BlockSpec((None,) + w.shape[1:], lambda *_: (layer, 0, 0), pipeline_mode=pl.Buffered(1))


def _ffn(x, g, w_in, w_out, layer, final_g=None, tm=1024):
    n = x.shape[0]
    tm = min(tm, n)
    row = pl.BlockSpec((tm, D_MODEL), lambda i: (i, 0))
    in_specs = [row, _const_spec((1, D_MODEL)), _layer_spec(w_in, layer), _layer_spec(w_out, layer)]
    args = [x, g[layer].reshape(1, D_MODEL), w_in, w_out]
    if final_g is not None:
        in_specs.append(_const_spec((1, D_MODEL)))
        args.append(final_g.reshape(1, D_MODEL))
    return pl.pallas_call(
        functools.partial(_ffn_body, final=final_g is not None),
        grid=(n // tm,),
        in_specs=in_specs,
        out_specs=row,
        out_shape=jax.ShapeDtypeStruct((n, D_MODEL), F32),
        compiler_params=_params(("parallel",)),
        name="ffn",
    )(*args)


def _ffn_roll_body(x_ref, g_ref, win_ref, wout_ref, *rest, final, aliased, batch0, past, steps):
    rest = list(rest)
    fg_ref = rest.pop(0) if final else None
    new_ref, cache_hbm = rest.pop(0), rest.pop(0)
    if aliased:
        rest.pop(0)
    o_ref, rolled_hbm, in_buf, out_buf, in_sem, out_sem = rest
    b = batch0 + pl.program_id(0)
    n_groups = N_HEADS // ROLL_GROUP
    n_chunks = D_FF // FF_CHUNK
    lane = lax.broadcasted_iota(jnp.int32, (HEAD_DIM, LANES), 1)
    tiles = past // LANES

    def copy(g, inward, seq=b):
        br, sl = divmod(g * ROLL_GROUP, HEADS_PER_BRANCH)
        half = pl.ds((g % 2) * ROLL_GROUP, ROLL_GROUP)
        if inward:
            return pltpu.make_async_copy(cache_hbm.at[seq, br, pl.ds(sl, ROLL_GROUP)], in_buf.at[half],
                                         in_sem.at[g % 2])
        return pltpu.make_async_copy(out_buf.at[half], rolled_hbm.at[seq, br, pl.ds(sl, ROLL_GROUP)],
                                     out_sem.at[g % 2])

    def arrive(g):
        copy(g, True).wait()
        if g >= 2:
            copy(g - 2, False).wait()

    def shift(g):
        for k in range(ROLL_GROUP):
            br, sl = divmod(g * ROLL_GROUP + k, HEADS_PER_BRANCH)
            src, dst = in_buf.at[(g % 2) * ROLL_GROUP + k], out_buf.at[(g % 2) * ROLL_GROUP + k]
            cur = pltpu.roll(src[:, 0:LANES], LANES - steps, 1)
            for t in range(tiles):
                if t + 1 < tiles:
                    nxt = pltpu.roll(src[:, (t + 1) * LANES:(t + 2) * LANES], LANES - steps, 1)
                else:
                    head = (br * HEADS_PER_BRANCH + sl) * HEAD_DIM
                    new_t = new_ref[0, :, head:head + HEAD_DIM].T
                    nxt = jnp.concatenate([jnp.zeros((HEAD_DIM, LANES - steps), F32), new_t], axis=1)
                dst[:, t * LANES:(t + 1) * LANES] = jnp.where(lane < LANES - steps, cur, nxt)
                cur = nxt
        copy(g, False).start()
        if g + 2 < n_groups:
            copy(g + 2, True).start()
        else:
            @pl.when(pl.program_id(0) + 1 < pl.num_programs(0))
            def _():
                copy(g + 2 - n_groups, True, b + 1).start()

    @pl.when(pl.program_id(0) == 0)
    def _():
        copy(0, True).start()
        copy(1, True).start()

    x = x_ref[...]
    h = _rms(x, g_ref[...]).astype(BF16)
    acc = x
    group_at = {1 + g * (n_chunks - 1) // n_groups: g for g in range(n_groups)}
    for j in range(n_chunks):
        lo = j * FF_CHUNK
        gate = _dot(h, win_ref[:, lo:lo + FF_CHUNK])
        up = _dot(h, win_ref[:, D_FF + lo:D_FF + lo + FF_CHUNK])
        if j in group_at:
            arrive(group_at[j])
            shift(group_at[j])
        act = gate * jax.nn.sigmoid(gate) * up
        acc = acc + _dot((0.5 * act).astype(BF16), wout_ref[lo:lo + FF_CHUNK, :])
    for g in range(n_groups - 2, n_groups):
        copy(g, False).wait()
    if final:
        acc = _rms(acc, fg_ref[...])
    o_ref[...] = acc


def _ffn_roll(x, g, w_in, w_out, layer, cache, new, prev, batch0, final_g=None, tm=1024):
    n = x.shape[0]
    past, steps = cache.shape[-1], new.shape[1]
    n_groups = N_HEADS // ROLL_GROUP
    assert n % tm == 0 and batch0 + n // tm <= cache.shape[0]
    assert HEADS_PER_BRANCH % ROLL_GROUP == 0 and n_groups % 2 == 0 and n_groups < D_FF // FF_CHUNK
    row = pl.BlockSpec((tm, D_MODEL), lambda i: (i, 0))
    anywhere = pl.BlockSpec(memory_space=pl.ANY)
    in_specs = [row, _const_spec((1, D_MODEL)), _layer_spec(w_in, layer), _layer_spec(w_out, layer)]
    args = [x, g[layer].reshape(1, D_MODEL), w_in, w_out]
    if final_g is not None:
        in_specs.append(_const_spec((1, D_MODEL)))
        args.append(final_g.reshape(1, D_MODEL))
    in_specs += [pl.BlockSpec((1,) + new.shape[1:], lambda i: (batch0 + i, 0, 0)), anywhere]
    args += [new, cache]
    aliases = {}
    if prev is not None:
        aliases = {len(args): 1}
        in_specs.append(anywhere)
        args.append(prev)
    unit = pltpu.VMEM((2 * ROLL_GROUP, HEAD_DIM, past), F32)
    return pl.pallas_call(
        functools.partial(_ffn_roll_body, final=final_g is not None, aliased=prev is not None, batch0=batch0,
                          past=past, steps=steps),
        grid=(n // tm,),
        in_specs=in_specs,
        out_specs=[row, anywhere],
        out_shape=[jax.ShapeDtypeStruct((n, D_MODEL), F32), jax.ShapeDtypeStruct(cache.shape, F32)],
        scratch_shapes=[unit, unit, pltpu.SemaphoreType.DMA((2,)), pltpu.SemaphoreType.DMA((2,))],
        input_output_aliases=aliases,
        compiler_params=_params(("arbitrary",)),
        name="ffn_roll",
    )(*args)


def _scan_levels(tm):
    levels, n = 0, tm
    while n > 1:
        assert n % SCAN_RADIX == 0
        n //= SCAN_RADIX
        levels += 1
    return levels


def _scan_powers(tm):
    pows = [SCAN_RADIX ** lvl * m for lvl in range(_scan_levels(tm)) for m in range(1, SCAN_RADIX)]
    return pows + [tm]


def _cmul(ar, ai, br, bi):
    return ar * br - ai * bi, ar * bi + ai * br


def _blocked_scan(hre, him, pw_ref, lv_re, lv_im, slab, slot, tm):
    levels = _scan_levels(tm)
    top = SCAN_BASE - 1

    def views(lvl):
        if lvl == 0:
            return (lambda sl: hre[slab, sl, :], lambda sl: him[slab, sl, :],
                    lambda sl, v: hre.__setitem__((slab, sl, slice(None)), v),
                    lambda sl, v: him.__setitem__((slab, sl, slice(None)), v))
        a, b = lv_re[lvl - 1], lv_im[lvl - 1]
        return (lambda sl: a[slot, sl, :], lambda sl: b[slot, sl, :],
                lambda sl, v: a.__setitem__((slot, sl, slice(None)), v),
                lambda sl, v: b.__setitem__((slot, sl, slice(None)), v))

    def power(k):
        return pw_ref[slab, pl.ds(2 * k, 1), :], pw_ref[slab, pl.ds(2 * k + 1, 1), :]

    def elems(n, tau):
        return pl.ds(SCAN_BASE + tau, n, stride=SCAN_RADIX)

    s_re, s_im = hre[slab, pl.ds(top, 1), :], him[slab, pl.ds(top, 1), :]
    for lvl in range(levels):
        get_re, get_im, put_re, put_im = views(lvl)
        n = tm // SCAN_RADIX ** (lvl + 1)
        lr, li = power((SCAN_RADIX - 1) * lvl)
        pr, pi = get_re(elems(n, 0)), get_im(elems(n, 0))
        for tau in range(1, SCAN_RADIX):
            mr, mi = _cmul(lr, li, pr, pi)
            pr, pi = mr + get_re(elems(n, tau)), mi + get_im(elems(n, tau))
            put_re(elems(n, tau), pr)
            put_im(elems(n, tau), pi)
        if lvl + 1 < levels:
            _, _, nput_re, nput_im = views(lvl + 1)
            nput_re(pl.ds(SCAN_BASE, n), pr)
            nput_im(pl.ds(SCAN_BASE, n), pi)
            nput_re(pl.ds(top, 1), s_re)
            nput_im(pl.ds(top, 1), s_im)
    for lvl in reversed(range(levels)):
        get_re, get_im, put_re, put_im = views(lvl)
        n = tm // SCAN_RADIX ** (lvl + 1)
        last = lvl + 1 == levels
        if not last:
            nget_re, nget_im, _, _ = views(lvl + 1)
            put_re(elems(n, SCAN_RADIX - 1), nget_re(pl.ds(SCAN_BASE, n)))
            put_im(elems(n, SCAN_RADIX - 1), nget_im(pl.ds(SCAN_BASE, n)))
        before = pl.ds(top, n, stride=SCAN_RADIX)
        tr, ti = get_re(before), get_im(before)
        for tau in range(SCAN_RADIX if last else SCAN_RADIX - 1):
            k = (SCAN_RADIX - 1) * lvl + tau if tau < SCAN_RADIX - 1 else (SCAN_RADIX - 1) * levels
            mr, mi = _cmul(*power(k), tr, ti)
            put_re(elems(n, tau), get_re(elems(n, tau)) + mr)
            put_im(elems(n, tau), get_im(elems(n, tau)) + mi)
    hre[slab, pl.ds(top, 1), :] = hre[slab, pl.ds(top + tm, 1), :]
    him[slab, pl.ds(top, 1), :] = him[slab, pl.ds(top + tm, 1), :]


def _s5_body(x_ref, h0re_ref, h0im_ref, g_ref, bbd_ref, cbd_ref, pw_ref, d_ref, wglu_ref,
             xo_ref, hlre_ref, hlim_ref, hre, him, *levels_scr, tm, seg):
    nseq = tm // seg
    rows = pl.ds(SCAN_BASE, tm)
    top = SCAN_BASE - 1
    n_lv = len(levels_scr) // 2
    lv_re, lv_im = levels_scr[:n_lv], levels_scr[n_lv:]

    if nseq == 1:
        @pl.when(pl.program_id(1) == 0)
        def _():
            hre[:, pl.ds(top, 1), :] = h0re_ref[0]
            him[:, pl.ds(top, 1), :] = h0im_ref[0]

    x = x_ref[0]
    u = _rms(x, g_ref[...])
    ub = u.astype(BF16)

    def project_in(k):
        bu = _dot(ub[:, k * SSM_BLOCK_IN:(k + 1) * SSM_BLOCK_IN], bbd_ref[k])
        for j in range(SSM_SLABS_PER_BLOCK):
            hre[k * SSM_SLABS_PER_BLOCK + j, rows, :] = bu[:, j * LANES:(j + 1) * LANES]
            him[k * SSM_SLABS_PER_BLOCK + j, rows, :] = bu[:, SSM_BLOCK_HALF + j * LANES:
                                                          SSM_BLOCK_HALF + (j + 1) * LANES]

    def project_out(k):
        sl = range(k * SSM_SLABS_PER_BLOCK, (k + 1) * SSM_SLABS_PER_BLOCK)
        hk = jnp.concatenate([hre[p, rows, :] for p in sl] + [him[p, rows, :] for p in sl], axis=-1)
        return _dot(hk.astype(BF16), cbd_ref[k])

    if nseq == 1:
        ys = []
        project_in(0)
        for k in range(SSM_BLOCKS):
            if k + 1 < SSM_BLOCKS:
                project_in(k + 1)
            for j in range(SSM_SLABS_PER_BLOCK):
                _blocked_scan(hre, him, pw_ref, lv_re, lv_im, k * SSM_SLABS_PER_BLOCK + j, j % SCAN_UNROLL, tm)
            ys.append(project_out(k))
        hlre_ref[0] = hre[:, pl.ds(top, 1), :]
        hlim_ref[0] = him[:, pl.ds(top, 1), :]
    else:
        for k in range(SSM_BLOCKS):
            project_in(k)

        def slab_body(p, carry):
            lr, li = pw_ref[p, pl.ds(0, 1), :], pw_ref[p, pl.ds(1, 1), :]
            hr, hi = h0re_ref[0, p], h0im_ref[0, p]
            for t in range(seg):
                step = pl.ds(SCAN_BASE + t, nseq, stride=seg)
                mr, mi = _cmul(lr, li, hr, hi)
                hr, hi = mr + hre[p, step, :], mi + him[p, step, :]
                hre[p, step, :] = hr
                him[p, step, :] = hi
            hlre_ref[0, p] = hr
            hlim_ref[0, p] = hi
            return carry

        lax.fori_loop(0, SSM_SLABS, slab_body, 0)
        ys = [project_out(k) for k in range(SSM_BLOCKS)]

    y = jnp.concatenate(ys, axis=-1) + d_ref[...] * u
    z = jax.nn.gelu(y, approximate=True).astype(BF16)
    zz = _dot(z, wglu_ref[...])
    xo_ref[0] = x + zz[:, :D_MODEL] * jax.nn.sigmoid(zz[:, D_MODEL:])


def _s5_weights(a_re, a_im, log_dt, b_re, b_im, c_re, c_im, powers):
    a_re, a_im = a_re.astype(F32), a_im.astype(F32)
    dt = jnp.exp(log_dt.astype(F32))[:, None]
    mag = jnp.exp(a_re * dt)
    lre, lim = mag * jnp.cos(a_im * dt), mag * jnp.sin(a_im * dt)
    den = a_re * a_re + a_im * a_im
    fre = ((lre - 1.0) * a_re + lim * a_im) / den
    fim = (lim * a_re - (lre - 1.0) * a_im) / den
    b_re, b_im = b_re.astype(F32), b_im.astype(F32)
    bb = jnp.stack([fre[:, :, None] * b_re - fim[:, :, None] * b_im,
                    fre[:, :, None] * b_im + fim[:, :, None] * b_re])
    nb, gb = SSM_BLOCKS, SSM_GROUPS_PER_BLOCK
    bb = bb.reshape(2, nb, gb, SSM_STATE, SSM_GROUP).transpose(1, 2, 4, 0, 3)
    bb = jnp.tile(bb.reshape(nb, SSM_BLOCK_IN, 2, SSM_STATE), (1, 1, 1, gb))
    row_g = lax.broadcasted_iota(jnp.int32, bb.shape, 1) // SSM_GROUP
    lane_g = lax.broadcasted_iota(jnp.int32, bb.shape, 3) // SSM_STATE
    bbd = jnp.where(row_g == lane_g, bb, 0.0).reshape(nb, SSM_BLOCK_IN, SSM_BLOCK_STATE)
    cc = jnp.stack([c_re.astype(F32), -c_im.astype(F32)])
    cc = cc.reshape(2, nb, gb, SSM_GROUP, SSM_STATE).transpose(1, 0, 2, 4, 3)
    cc = jnp.tile(cc.reshape(nb, SSM_BLOCK_STATE, SSM_GROUP), (1, 1, gb))
    row_g = lax.broadcasted_iota(jnp.int32, cc.shape, 1) % SSM_BLOCK_HALF // SSM_STATE
    lane_g = lax.broadcasted_iota(jnp.int32, cc.shape, 2) // SSM_GROUP
    cbd = jnp.where(row_g == lane_g, cc, 0.0)
    tables = []
    for pows in powers:
        k = jnp.asarray(pows, F32)[:, None, None]
        mag_k = jnp.exp(k * (a_re * dt))
        pw = jnp.stack([mag_k * jnp.cos(k * (a_im * dt)), mag_k * jnp.sin(k * (a_im * dt))], axis=1)
        tables.append(pw.reshape(2 * len(pows), SSM_SLABS, LANES).transpose(1, 0, 2))
    return bbd.astype(BF16), cbd.astype(BF16), tables


def _state_to_slabs(h):
    b = h.shape[0]
    h = h.astype(F32).reshape(b, SSM_SLABS, LANES, 2).transpose(3, 1, 0, 2)
    return h[0], h[1]


def _slabs_to_state(re, im):
    b = re.shape[1]
    h = jnp.stack([re, im], axis=-1).transpose(1, 0, 2, 3)
    return h.reshape(b, N_SSM_GROUPS, SSM_STATE, 2)


def _s5(x, h0re, h0im, g, bbd, cbd, pw, d_skip, w_glu, *, tm, seg):
    nb, rows, _ = x.shape
    nseq = tm // seg
    assert nseq == 1 or rows == tm
    row = pl.BlockSpec((1, tm, D_MODEL), lambda b, i: (b, i, 0))
    st = pl.BlockSpec((1, SSM_SLABS, nseq, LANES), lambda b, i: (b, 0, 0, 0))
    slab = pltpu.VMEM((SSM_SLABS, SCAN_BASE + tm, LANES), F32)
    levels = []
    if nseq == 1:
        levels = [pltpu.VMEM((SCAN_UNROLL, SCAN_BASE + max(tm // SCAN_RADIX ** lvl, SUBLANES), LANES), F32)
                  for lvl in range(1, _scan_levels(tm))] * 2
    st_shape = jax.ShapeDtypeStruct(h0re.shape, F32)
    return pl.pallas_call(
        functools.partial(_s5_body, tm=tm, seg=seg),
        grid=(nb, rows // tm),
        in_specs=[row, st, st, _const_spec((1, D_MODEL)), _const_spec(bbd.shape), _const_spec(cbd.shape),
                  _const_spec(pw.shape), _const_spec((1, D_MODEL)), _const_spec(w_glu.shape)],
        out_specs=[row, st, st],
        out_shape=[jax.ShapeDtypeStruct(x.shape, F32), st_shape, st_shape],
        scratch_shapes=[slab, slab] + levels,
        compiler_params=_params(("parallel", "arbitrary")),
        name="s5_mixer",
    )(x, h0re, h0im, g.reshape(1, D_MODEL), bbd, cbd, pw, d_skip.reshape(1, D_MODEL), w_glu)


def _rope_tables(pos):
    half = ROT_DIM // 2
    inv = ROPE_THETA ** (-jnp.arange(0, ROT_DIM, 2, dtype=F32) / ROT_DIM)
    ang = pos.astype(F32)[:, None] * inv[None, :]
    cos, sin = jnp.cos(ang), jnp.sin(ang)
    n = pos.shape[0]
    pad = HEAD_DIM - ROT_DIM
    c = jnp.concatenate([cos, cos, jnp.ones((n, pad), F32)], axis=-1)
    s1 = jnp.concatenate([-sin, jnp.zeros((n, half + pad), F32)], axis=-1)
    s2 = jnp.concatenate([jnp.zeros((n, half), F32), sin, jnp.zeros((n, pad), F32)], axis=-1)
    rep = LANES // HEAD_DIM
    return tuple(jnp.tile(t, (1, rep)) for t in (c, s1, s2))


def _rope_tile(y, c, s1, s2):
    half = ROT_DIM // 2
    return y * c + pltpu.roll(y, LANES - half, 1) * s1 + pltpu.roll(y, half, 1) * s2


def _proj_body(x_ref, g_ref, w_ref, c_ref, s1_ref, s2_ref, *o_refs, rope_tiles):
    h = _rms(x_ref[...], g_ref[...]).astype(BF16)
    y = _dot(h, w_ref[...])
    c, s1, s2 = c_ref[...], s1_ref[...], s2_ref[...]
    tiles_per_out = y.shape[1] // LANES // len(o_refs)
    for j in range(y.shape[1] // LANES):
        yj = y[:, j * LANES:(j + 1) * LANES]
        if j < rope_tiles:
            yj = _rope_tile(yj, c, s1, s2)
        jo = j % tiles_per_out
        o_refs[j // tiles_per_out][:, jo * LANES:(jo + 1) * LANES] = yj


def _proj(x, g, w, tables, rope_width, n_out, tm=512):
    n = x.shape[0]
    tm = min(tm, n)
    width = w.shape[1] // n_out
    row = pl.BlockSpec((tm, D_MODEL), lambda i: (i, 0))
    tab = pl.BlockSpec((tm, LANES), lambda i: (i, 0))
    return pl.pallas_call(
        functools.partial(_proj_body, rope_tiles=rope_width // LANES),
        grid=(n // tm,),
        in_specs=[row, _const_spec((1, D_MODEL)), _const_spec(w.shape), tab, tab, tab],
        out_specs=[pl.BlockSpec((tm, width), lambda i: (i, 0))] * n_out,
        out_shape=[jax.ShapeDtypeStruct((n, width), F32)] * n_out,
        compiler_params=_params(("parallel",)),
        name="proj_rope",
    )(x, g.reshape(1, D_MODEL), w, *tables)


def _proj_heads_body(x_ref, g_ref, w_ref, c_ref, s1_ref, s2_ref, *refs, n_mat, tm, cache_from):
    n_t = n_mat if cache_from is not None else 0
    outs = refs[:N_BRANCHES * n_mat]
    t_refs = refs[N_BRANCHES * n_mat:N_BRANCHES * n_mat + n_t]
    slab = refs[-1]
    h = _rms(x_ref[0], g_ref[...]).astype(BF16)
    y = _dot(h, w_ref[...])
    c, s1, s2 = c_ref[...], s1_ref[...], s2_ref[...]
    for m in range(n_mat):
        tiles = []
        for br, d in enumerate(DILATIONS):
            lo = m * ATTN_WIDTH + br * MERGED_WIDTH
            part = [y[:, lo + j * LANES:lo + (j + 1) * LANES] for j in range(BRANCH_TILES)]
            if m == 0:
                part = [_rope_tile(t, c, s1, s2) for t in part]
            tiles += part
            out = outs[N_BRANCHES * m + br]
            if d == 1:
                for j in range(BRANCH_TILES):
                    out[0, 0, :, j * LANES:(j + 1) * LANES] = part[j].astype(BF16)
                continue
            for j in range(BRANCH_TILES):
                slab[(br - 1) * BRANCH_TILES + j] = part[j]
            for r in range(d):
                for j in range(BRANCH_TILES):
                    rows = slab[(br - 1) * BRANCH_TILES + j, pl.ds(r, tm // d, stride=d), :]
                    out[0, r, :, j * LANES:(j + 1) * LANES] = rows.astype(BF16)
        if cache_from is not None:
            @pl.when(pl.program_id(1) >= cache_from)
            def _(m=m, tiles=tiles):
                t_refs[m][0] = jnp.concatenate(tiles, axis=-1).T


def _proj_heads(x, g, w, tables, n_mat, keep=None, tm=512):
    b, l, _ = x.shape
    assert w.shape[1] == n_mat * ATTN_WIDTH and tm % (DILATIONS[-1] * 2 * SUBLANES) == 0
    cache_from = None if keep is None else (l - keep) // tm
    row = pl.BlockSpec((1, tm, D_MODEL), lambda bi, i: (bi, i, 0))
    tab = pl.BlockSpec((tm, LANES), lambda bi, i: (i, 0))
    out_specs, out_shape = [], []
    for _ in range(n_mat):
        for d in DILATIONS:
            out_specs.append(pl.BlockSpec((1, d, tm // d, MERGED_WIDTH), lambda bi, i: (bi, 0, i, 0)))
            out_shape.append(jax.ShapeDtypeStruct((b, d, l // d, MERGED_WIDTH), BF16))
    if keep is not None:
        for _ in range(n_mat):
            out_specs.append(pl.BlockSpec((1, ATTN_WIDTH, tm),
                                          lambda bi, i: (bi, 0, jnp.maximum(i - cache_from, 0))))
            out_shape.append(jax.ShapeDtypeStruct((b, ATTN_WIDTH, keep), F32))
    return pl.pallas_call(
        functools.partial(_proj_heads_body, n_mat=n_mat, tm=tm, cache_from=cache_from),
        grid=(b, l // tm),
        in_specs=[row, _const_spec((1, D_MODEL)), _const_spec(w.shape), tab, tab, tab],
        out_specs=out_specs,
        out_shape=out_shape,
        scratch_shapes=[pltpu.VMEM(((N_BRANCHES - 1) * BRANCH_TILES, tm, LANES), F32)],
        compiler_params=_params(("parallel", "arbitrary")),
        name="proj_heads",
    )(x, g.reshape(1, D_MODEL), w, *tables)


def _swa_body(q_ref, kp_ref, kc_ref, vp_ref, vc_ref, o_ref, lse_ref, *, steps_per_seq):
    n = pl.program_id(1)
    k = jnp.concatenate([kp_ref[0], kc_ref[0]], axis=0)
    v = jnp.concatenate([vp_ref[0], vc_ref[0]], axis=0)
    qi = lax.broadcasted_iota(jnp.int32, (SWA_BLOCK, 2 * SWA_BLOCK), 0)
    kj = lax.broadcasted_iota(jnp.int32, (SWA_BLOCK, 2 * SWA_BLOCK), 1)
    band = (kj >= qi) & (kj <= qi + SWA_BLOCK)
    has_prev = (n % steps_per_seq) != 0
    first = band & ((kj >= SWA_BLOCK) | has_prev)
    low = lax.broadcasted_iota(jnp.int32, (SWA_BLOCK, LANES), 1) < HEAD_DIM
    zero = jnp.zeros((SWA_BLOCK, LANES), BF16)
    for j in range(SWA_SUB):
        rows = slice(j * SWA_BLOCK, (j + 1) * SWA_BLOCK)
        keys = slice(j * SWA_BLOCK, (j + 2) * SWA_BLOCK)
        valid = first if j == 0 else band
        for t in range(BRANCH_TILES):
            cols = slice(t * LANES, (t + 1) * LANES)
            q = q_ref[0, rows, cols] * (HEAD_DIM ** -0.5)
            kt, vt = k[keys, cols], v[keys, cols]
            outs, lses = [], []
            for qh in (jnp.where(low, q, zero), jnp.where(low, zero, q)):
                s = jnp.where(valid, _dot_nt(qh, kt), NEG_INF)
                m = jnp.max(s, axis=-1, keepdims=True)
                p = jnp.exp(s - m)
                l = jnp.sum(p, axis=-1, keepdims=True)
                outs.append(_dot(p.astype(BF16), vt) / l)
                lses.append(jnp.broadcast_to(m + jnp.log(l), (SWA_BLOCK, LANES)))
            o_ref[0, rows, cols] = jnp.where(low, outs[0], outs[1])
            lse_ref[0, rows, cols] = jnp.where(low, lses[0], lses[1])


def _swa(q, k, v, dilation):
    b, l, w = q.shape
    step = SWA_SUB * SWA_BLOCK
    assert (l // dilation) % step == 0
    cur = pl.BlockSpec((1, step, w), lambda bi, n: (bi, n, 0))
    prev = pl.BlockSpec((1, SWA_BLOCK, w), lambda bi, n: (bi, jnp.maximum(SWA_SUB * n - 1, 0), 0))
    out = jax.ShapeDtypeStruct((b, l, w), F32)
    return pl.pallas_call(
        functools.partial(_swa_body, steps_per_seq=l // dilation // step),
        grid=(b, l // step),
        in_specs=[cur, prev, cur, prev, cur],
        out_specs=[cur, cur],
        out_shape=[out, out],
        compiler_params=_params(("parallel", "arbitrary")),
        name="swa",
    )(q, k, k, v, v)


def _merge_body(x_ref, *refs, tm):
    pairs, wo_ref, out_ref, slab = refs[:2 * N_BRANCHES], refs[2 * N_BRANCHES], refs[-2], refs[-1]
    for a, ref in enumerate(pairs[2:]):
        d = DILATIONS[1 + a // 2]
        for r in range(d):
            for j in range(BRANCH_TILES):
                slab[a * BRANCH_TILES + j, pl.ds(r, tm // d, stride=d), :] = ref[0, r, :, j * LANES:(j + 1) * LANES]
    tiles = []
    for j in range(BRANCH_TILES):
        cols = slice(j * LANES, (j + 1) * LANES)
        o0, l0 = pairs[0][0, 0, :, cols], pairs[1][0, 0, :, cols]
        o1, l1, o2, l2 = (slab[a * BRANCH_TILES + j] for a in range(4))
        m = jnp.maximum(jnp.maximum(l0, l1), l2)
        e0, e1, e2 = jnp.exp(l0 - m), jnp.exp(l1 - m), jnp.exp(l2 - m)
        tiles.append(((e0 * o0 + e1 * o1 + e2 * o2) / (e0 + e1 + e2)).astype(BF16))
    out_ref[0] = x_ref[0] + _dot(jnp.concatenate(tiles, axis=-1), wo_ref[...])


def _merge(x, os, ls, w_o, tm=512):
    b, l, _ = x.shape
    row = pl.BlockSpec((1, tm, D_MODEL), lambda bi, i: (bi, i, 0))
    specs, args = [], []
    for d, o, lse in zip(DILATIONS, os, ls):
        spec = pl.BlockSpec((1, d, tm // d, MERGED_WIDTH), lambda bi, i: (bi, 0, i, 0))
        specs += [spec, spec]
        args += [o, lse]
    return pl.pallas_call(
        functools.partial(_merge_body, tm=tm),
        grid=(b, l // tm),
        in_specs=[row] + specs + [_const_spec(w_o.shape)],
        out_specs=row,
        out_shape=jax.ShapeDtypeStruct(x.shape, F32),
        scratch_shapes=[pltpu.VMEM((2 * (N_BRANCHES - 1) * BRANCH_TILES, tm, LANES), F32)],
        compiler_params=_params(("parallel", "parallel")),
        name="merge_wo",
    )(x, *args, w_o)


def _decode_body(q_ref, kn_ref, vn_ref, k0_ref, k1_ref, k2_ref, v0_ref, v1_ref, v2_ref, o_ref, *, steps):
    scale = HEAD_DIM ** -0.5
    qi = lax.broadcasted_iota(jnp.int32, (steps, steps), 0)
    kj = lax.broadcasted_iota(jnp.int32, (steps, steps), 1)
    k_refs, v_refs = (k0_ref, k1_ref, k2_ref), (v0_ref, v1_ref, v2_ref)

    def slot(sl):
        scores, values = [], []
        for br, d in enumerate(DILATIONS):
            w = WINDOWS[br]
            head = slice((br * HEADS_PER_BRANCH + sl) * HEAD_DIM, (br * HEADS_PER_BRANCH + sl + 1) * HEAD_DIM)
            q = q_ref[0, :, head].astype(BF16)
            back = (w + lax.broadcasted_iota(jnp.int32, (steps, w), 0)
                    - lax.broadcasted_iota(jnp.int32, (steps, w), 1))
            ok = ((back & (d - 1)) == 0) & (back <= w)
            scores.append(jnp.where(ok, _dot(q, k_refs[br][0, 0, sl].astype(BF16)) * scale, NEG_INF))
            values.append((v_refs[br][0, 0, sl].astype(BF16), True))
            ok = (kj <= qi) & (((qi - kj) & (d - 1)) == 0)
            scores.append(jnp.where(ok, _dot_nt(q, kn_ref[0, :, head].astype(BF16)) * scale, NEG_INF))
            values.append((vn_ref[0, :, head].astype(BF16), False))
        m = functools.reduce(jnp.maximum, [jnp.max(s, axis=-1, keepdims=True) for s in scores])
        num = jnp.zeros((steps, HEAD_DIM), F32)
        den = jnp.zeros((steps, 1), F32)
        for s, (v, transposed) in zip(scores, values):
            p = jnp.exp(s - m)
            den = den + jnp.sum(p, axis=-1, keepdims=True)
            num = num + (_dot_nt(p.astype(BF16), v) if transposed else _dot(p.astype(BF16), v))
        o_ref[0, :, sl * HEAD_DIM:(sl + 1) * HEAD_DIM] = num / den

    for sl in range(HEADS_PER_BRANCH):
        slot(sl)


def _cache_t(c):
    b, past = c.shape[0], c.shape[1]
    return c.transpose(0, 2, 3, 1).reshape(b, N_BRANCHES, HEADS_PER_BRANCH, HEAD_DIM, past)


def _cache_from_t(c):
    b, past = c.shape[0], c.shape[-1]
    return c.reshape(b, N_HEADS, HEAD_DIM, past).transpose(0, 3, 1, 2)


def _decode(q, kn, vn, cache_k, cache_v):
    b, steps = q.shape[0], q.shape[1]
    past = cache_k.shape[-1]
    assert past == MAX_WINDOW
    hb = HEADS_PER_BRANCH
    new = pl.BlockSpec((1, steps, ATTN_WIDTH), lambda i: (i, 0, 0))
    windows = [pl.BlockSpec((1, 1, hb, HEAD_DIM, w), functools.partial(lambda br, last, i: (i, br, 0, 0, last),
                                                                      br, past // w - 1))
               for br, w in enumerate(WINDOWS)]
    o = pl.pallas_call(
        functools.partial(_decode_body, steps=steps),
        grid=(b,),
        in_specs=[new, new, new] + windows + windows,
        out_specs=pl.BlockSpec((1, steps, MERGED_WIDTH), lambda i: (i, 0, 0)),
        out_shape=jax.ShapeDtypeStruct((b, steps, MERGED_WIDTH), F32),
        compiler_params=_params(("parallel",)),
        name="decode_attn",
    )(q, kn, vn, *([cache_k] * N_BRANCHES), *([cache_v] * N_BRANCHES))
    return o.reshape(b * steps, MERGED_WIDTH)


def _addproj_body(x_ref, o_ref, wo_ref, out_ref):
    out_ref[...] = x_ref[...] + _dot(o_ref[...].astype(BF16), wo_ref[...])


def _addproj(x, o, w_o):
    n = x.shape[0]
    return pl.pallas_call(
        _addproj_body,
        grid=(1,),
        in_specs=[_const_spec(x.shape), _const_spec(o.shape), _const_spec(w_o.shape)],
        out_specs=pl.BlockSpec(x.shape, lambda i: (0, 0)),
        out_shape=jax.ShapeDtypeStruct((n, D_MODEL), F32),
        compiler_params=_params(("arbitrary",)),
        name="add_wo",
    )(x, o, w_o)


def kernel(x_prompt, x_sample, state_ssm, cache_k, cache_v, ffn_norm1, ffn_w1_in, ffn_w1_out, ffn_norm2,
           ffn_w2_in, ffn_w2_out, mix_norm, ssm_a_re, ssm_a_im, ssm_log_dt, ssm_b_re, ssm_b_im, ssm_c_re,
           ssm_c_im, ssm_d, ssm_w_glu, kv_norm, w_kv, w_q, w_o, final_norm):
    bp, sp, _ = x_prompt.shape
    bs, ss, _ = x_sample.shape
    tm_scan = 256

    w1_in, w1_out, w2_in, w2_out = (w.astype(BF16) for w in (ffn_w1_in, ffn_w1_out, ffn_w2_in, ffn_w2_out))
    w_glu, w_kv_b, w_q_b, w_o_b = (ssm_w_glu[0].astype(BF16), w_kv.astype(BF16), w_q[0].astype(BF16),
                                   w_o[0].astype(BF16))
    bbd, cbd, (pw_p, pw_s) = _s5_weights(ssm_a_re[0], ssm_a_im[0], ssm_log_dt[0], ssm_b_re[0], ssm_b_im[0],
                                         ssm_c_re[0], ssm_c_im[0], (_scan_powers(tm_scan), [1]))

    def mixer(x, h0, pw, tm, seg):
        return _s5(x, *h0, mix_norm[0], bbd, cbd, pw, ssm_d[0], w_glu, tm=tm, seg=seg)

    pos_s = jnp.tile(PAST_LEN + jnp.arange(ss, dtype=jnp.int32), bs)
    tab_s = _rope_tables(pos_s)
    h0_s = tuple(t[None] for t in _state_to_slabs(state_ssm[0]))
    xs = _ffn(x_sample.reshape(bs * ss, D_MODEL), ffn_norm1, w1_in, w1_out, 0)
    xs, hl_re, hl_im = mixer(xs.reshape(1, bs * ss, D_MODEL), h0_s, pw_s, bs * ss, ss)
    xs = _ffn(xs.reshape(bs * ss, D_MODEL), ffn_norm2, w2_in, w2_out, 0)
    state_sample = _slabs_to_state(hl_re[0], hl_im[0])[None]
    kn_t, vn_t = (t.reshape(bs, ss, ATTN_WIDTH) for t in _proj(xs, kv_norm, w_kv_b, tab_s, ATTN_WIDTH, 2))
    ck_t, cv_t = _cache_t(cache_k), _cache_t(cache_v)
    half = bs // 2
    tm_roll = bp * sp // half

    tab_p = _rope_tables(jnp.arange(sp, dtype=jnp.int32))
    zeros = jnp.zeros((bp, SSM_SLABS, 1, LANES), F32)
    xp, k_roll = _ffn_roll(x_prompt.reshape(bp * sp, D_MODEL), ffn_norm1, w1_in, w1_out, 0, ck_t, kn_t, None, 0,
                           tm=tm_roll)
    xp, hl_re, hl_im = mixer(xp.reshape(bp, sp, D_MODEL), (zeros, zeros), pw_p, tm_scan, tm_scan)
    xp, k_roll = _ffn_roll(xp.reshape(bp * sp, D_MODEL), ffn_norm2, w2_in, w2_out, 0, ck_t, kn_t, k_roll, half,
                           tm=tm_roll)
    cache_k_sample = _cache_from_t(k_roll)
    state_prompt = _slabs_to_state(hl_re[:, :, 0].transpose(1, 0, 2), hl_im[:, :, 0].transpose(1, 0, 2))[None]
    keep_p = min(MAX_WINDOW, sp)
    *kv_d, kt_p, vt_p = _proj_heads(xp.reshape(bp, sp, D_MODEL), kv_norm, w_kv_b, tab_p, 2, keep=keep_p)
    cache_k_prompt = kt_p.reshape(bp, N_HEADS, HEAD_DIM, keep_p).transpose(0, 3, 1, 2)
    cache_v_prompt = vt_p.reshape(bp, N_HEADS, HEAD_DIM, keep_p).transpose(0, 3, 1, 2)
    xp, v_roll = _ffn_roll(xp, ffn_norm1, w1_in, w1_out, 1, cv_t, vn_t, None, 0, tm=tm_roll)
    xp = xp.reshape(bp, sp, D_MODEL)
    q_d = _proj_heads(xp, mix_norm[1], w_q_b, tab_p, 1)
    outs, lses = [], []
    for br, d in enumerate(DILATIONS):
        flat = lambda t: t.reshape(bp, sp, MERGED_WIDTH)
        o, lse = _swa(flat(q_d[br]), flat(kv_d[br]), flat(kv_d[N_BRANCHES + br]), d)
        outs.append(o.reshape(bp, d, sp // d, MERGED_WIDTH))
        lses.append(lse.reshape(bp, d, sp // d, MERGED_WIDTH))
    xp = _merge(xp, outs, lses, w_o_b).reshape(bp * sp, D_MODEL)
    y_prompt, v_roll = _ffn_roll(xp, ffn_norm2, w2_in, w2_out, 1, cv_t, vn_t, v_roll, half, final_g=final_norm,
                                 tm=tm_roll)
    y_prompt = y_prompt.reshape(bp, sp, D_MODEL)
    cache_v_sample = _cache_from_t(v_roll)

    xs = _ffn(xs, ffn_norm1, w1_in, w1_out, 1)
    q_s = _proj(xs, mix_norm[1], w_q_b, tab_s, ATTN_WIDTH, 1)[0].reshape(bs, ss, ATTN_WIDTH)
    xs = _addproj(xs, _decode(q_s, kn_t, vn_t, ck_t, cv_t), w_o_b)
    y_sample = _ffn(xs, ffn_norm2, w2_in, w2_out, 1, final_g=final_norm).reshape(bs, ss, D_MODEL)

    return (y_prompt, y_sample, state_prompt, cache_k_prompt, cache_v_prompt,
            state_sample, cache_k_sample, cache_v_sample)
```

```python
import functools

import jax
import jax.numpy as jnp
from jax import lax
from jax.experimental import pallas as pl
from jax.experimental.pallas import tpu as pltpu

F32 = jnp.float32
BF16 = jnp.bfloat16

D_MODEL = 1024
D_FF = 2816
SSM_GROUP = 16
N_SSM_GROUPS = D_MODEL // SSM_GROUP
SSM_STATE = 64
HEAD_DIM = 64
WINDOWS = (128, 512, 2048)
DILATIONS = (1, 4, 16)
N_BRANCHES = 3
HEADS_PER_BRANCH = 8
N_HEADS = N_BRANCHES * HEADS_PER_BRANCH
ATTN_WIDTH = N_HEADS * HEAD_DIM
MERGED_WIDTH = HEADS_PER_BRANCH * HEAD_DIM
MAX_WINDOW = max(WINDOWS)
ROT_DIM = HEAD_DIM // 4
ROPE_THETA = 500000.0
PAST_LEN = 16384
RMS_EPS = 1e-6
NEG_INF = -1e30

LANES = 128
SUBLANES = 8
BRANCH_TILES = MERGED_WIDTH // LANES

SSM_BLOCKS = 4
SSM_GROUPS_PER_BLOCK = N_SSM_GROUPS // SSM_BLOCKS
SSM_BLOCK_IN = SSM_GROUPS_PER_BLOCK * SSM_GROUP
SSM_BLOCK_HALF = SSM_GROUPS_PER_BLOCK * SSM_STATE
SSM_BLOCK_STATE = 2 * SSM_BLOCK_HALF
SSM_SLABS = N_SSM_GROUPS * SSM_STATE // LANES
SSM_SLABS_PER_BLOCK = SSM_SLABS // SSM_BLOCKS
SCAN_RADIX = 4
SCAN_BASE = SUBLANES
SCAN_UNROLL = 4

FF_CHUNK = 256
SWA_BLOCK = 128
SWA_SUB = 4
ROLL_GROUP = 4

VMEM_LIMIT = 56 * 1024 * 1024


def _params(semantics):
    return pltpu.CompilerParams(dimension_semantics=semantics, vmem_limit_bytes=VMEM_LIMIT)


def _const_spec(shape):
    nd = len(shape)
    return pl.BlockSpec(shape, lambda *_: (0,) * nd, pipeline_mode=pl.Buffered(1))


def _rms(x, g):
    return x * lax.rsqrt(jnp.mean(x * x, axis=-1, keepdims=True) + RMS_EPS) * g


def _dot(a, b):
    return jnp.dot(a, b, preferred_element_type=F32)


def _dot_nt(a, b):
    return lax.dot_general(a, b, (((1,), (1,)), ((), ())), preferred_element_type=F32)


def _ffn_body(x_ref, g_ref, win_ref, wout_ref, *rest, final):
    if final:
        fg_ref, o_ref = rest
    else:
        (o_ref,) = rest
    x = x_ref[...]
    h = _rms(x, g_ref[...]).astype(BF16)
    acc = x
    for j in range(D_FF // FF_CHUNK):
        lo = j * FF_CHUNK
        gate = _dot(h, win_ref[:, lo:lo + FF_CHUNK])
        up = _dot(h, win_ref[:, D_FF + lo:D_FF + lo + FF_CHUNK])
        act = gate * jax.nn.sigmoid(gate) * up
        acc = acc + _dot((0.5 * act).astype(BF16), wout_ref[lo:lo + FF_CHUNK, :])
    if final:
        acc = _rms(acc, fg_ref[...])
    o_ref[...] = acc


def _layer_spec(w, layer):
    return pl.BlockSpec((None,) + w.shape[1:], lambda *_: (layer, 0, 0), pipeline_mode=pl.Buffered(1))


def _ffn(x, g, w_in, w_out, layer, final_g=None, tm=1024):
    n = x.shape[0]
    tm = min(tm, n)
    row = pl.BlockSpec((tm, D_MODEL), lambda i: (i, 0))
    in_specs = [row, _const_spec((1, D_MODEL)), _layer_spec(w_in, layer), _layer_spec(w_out, layer)]
    args = [x, g[layer].reshape(1, D_MODEL), w_in, w_out]
    if final_g is not None:
        in_specs.append(_const_spec((1, D_MODEL)))
        args.append(final_g.reshape(1, D_MODEL))
    return pl.pallas_call(
        functools.partial(_ffn_body, final=final_g is not None),
        grid=(n // tm,),
        in_specs=in_specs,
        out_specs=row,
        out_shape=jax.ShapeDtypeStruct((n, D_MODEL), F32),
        compiler_params=_params(("parallel",)),
        name="ffn",
    )(*args)


def _ffn_roll_body(x_ref, g_ref, win_ref, wout_ref, *rest, final, aliased, batch0, past, steps):
    rest = list(rest)
    fg_ref = rest.pop(0) if final else None
    new_ref, cache_hbm = rest.pop(0), rest.pop(0)
    if aliased:
        rest.pop(0)
    o_ref, rolled_hbm, in_buf, out_buf, in_sem, out_sem = rest
    b = batch0 + pl.program_id(0)
    n_groups = N_HEADS // ROLL_GROUP
    n_chunks = D_FF // FF_CHUNK
    lane = lax.broadcasted_iota(jnp.int32, (HEAD_DIM, LANES), 1)
    tiles = past // LANES

    def copy(g, inward, seq=b):
        br, sl = divmod(g * ROLL_GROUP, HEADS_PER_BRANCH)
        half = pl.ds((g % 2) * ROLL_GROUP, ROLL_GROUP)
        if inward:
            return pltpu.make_async_copy(cache_hbm.at[seq, br, pl.ds(sl, ROLL_GROUP)], in_buf.at[half],
                                         in_sem.at[g % 2])
        return pltpu.make_async_copy(out_buf.at[half], rolled_hbm.at[seq, br, pl.ds(sl, ROLL_GROUP)],
                                     out_sem.at[g % 2])

    def arrive(g):
        copy(g, True).wait()
        if g >= 2:
            copy(g - 2, False).wait()

    def shift(g):
        for k in range(ROLL_GROUP):
            br, sl = divmod(g * ROLL_GROUP + k, HEADS_PER_BRANCH)
            src, dst = in_buf.at[(g % 2) * ROLL_GROUP + k], out_buf.at[(g % 2) * ROLL_GROUP + k]
            cur = pltpu.roll(src[:, 0:LANES], LANES - steps, 1)
            for t in range(tiles):
                if t + 1 < tiles:
                    nxt = pltpu.roll(src[:, (t + 1) * LANES:(t + 2) * LANES], LANES - steps, 1)
                else:
                    head = (br * HEADS_PER_BRANCH + sl) * HEAD_DIM
                    new_t = new_ref[0, :, head:head + HEAD_DIM].T
                    nxt = jnp.concatenate([jnp.zeros((HEAD_DIM, LANES - steps), F32), new_t], axis=1)
                dst[:, t * LANES:(t + 1) * LANES] = jnp.where(lane < LANES - steps, cur, nxt)
                cur = nxt
        copy(g, False).start()
        if g + 2 < n_groups:
            copy(g + 2, True).start()
        else:
            @pl.when(pl.program_id(0) + 1 < pl.num_programs(0))
            def _():
                copy(g + 2 - n_groups, True, b + 1).start()

    @pl.when(pl.program_id(0) == 0)
    def _():
        copy(0, True).start()
        copy(1, True).start()

    x = x_ref[...]
    h = _rms(x, g_ref[...]).astype(BF16)
    acc = x
    group_at = {1 + g * (n_chunks - 1) // n_groups: g for g in range(n_groups)}
    for j in range(n_chunks):
        lo = j * FF_CHUNK
        if j in group_at:
            arrive(group_at[j])
        gate = _dot(h, win_ref[:, lo:lo + FF_CHUNK])
        up = _dot(h, win_ref[:, D_FF + lo:D_FF + lo + FF_CHUNK])
        if j in group_at:
            shift(group_at[j])
        act = gate * jax.nn.sigmoid(gate) * up
        acc = acc + _dot((0.5 * act).astype(BF16), wout_ref[lo:lo + FF_CHUNK, :])
    for g in range(n_groups - 2, n_groups):
        copy(g, False).wait()
    if final:
        acc = _rms(acc, fg_ref[...])
    o_ref[...] = acc


def _ffn_roll(x, g, w_in, w_out, layer, cache, new, prev, batch0, final_g=None, tm=1024):
    n = x.shape[0]
    past, steps = cache.shape[-1], new.shape[1]
    n_groups = N_HEADS // ROLL_GROUP
    assert n % tm == 0 and batch0 + n // tm <= cache.shape[0]
    assert HEADS_PER_BRANCH % ROLL_GROUP == 0 and n_groups % 2 == 0 and n_groups < D_FF // FF_CHUNK
    row = pl.BlockSpec((tm, D_MODEL), lambda i: (i, 0))
    anywhere = pl.BlockSpec(memory_space=pl.ANY)
    in_specs = [row, _const_spec((1, D_MODEL)), _layer_spec(w_in, layer), _layer_spec(w_out, layer)]
    args = [x, g[layer].reshape(1, D_MODEL), w_in, w_out]
    if final_g is not None:
        in_specs.append(_const_spec((1, D_MODEL)))
        args.append(final_g.reshape(1, D_MODEL))
    in_specs += [pl.BlockSpec((1,) + new.shape[1:], lambda i: (batch0 + i, 0, 0)), anywhere]
    args += [new, cache]
    aliases = {}
    if prev is not None:
        aliases = {len(args): 1}
        in_specs.append(anywhere)
        args.append(prev)
    unit = pltpu.VMEM((2 * ROLL_GROUP, HEAD_DIM, past), F32)
    return pl.pallas_call(
        functools.partial(_ffn_roll_body, final=final_g is not None, aliased=prev is not None, batch0=batch0,
                          past=past, steps=steps),
        grid=(n // tm,),
        in_specs=in_specs,
        out_specs=[row, anywhere],
        out_shape=[jax.ShapeDtypeStruct((n, D_MODEL), F32), jax.ShapeDtypeStruct(cache.shape, F32)],
        scratch_shapes=[unit, unit, pltpu.SemaphoreType.DMA((2,)), pltpu.SemaphoreType.DMA((2,))],
        input_output_aliases=aliases,
        compiler_params=_params(("arbitrary",)),
        name="ffn_roll",
    )(*args)


def _scan_levels(tm):
    levels, n = 0, tm
    while n > 1:
        assert n % SCAN_RADIX == 0
        n //= SCAN_RADIX
        levels += 1
    return levels


def _scan_powers(tm):
    pows = [SCAN_RADIX ** lvl * m for lvl in range(_scan_levels(tm)) for m in range(1, SCAN_RADIX)]
    return pows + [tm]


def _cmul(ar, ai, br, bi):
    return ar * br - ai * bi, ar * bi + ai * br


def _blocked_scan(hre, him, pw_ref, lv_re, lv_im, slab, slot, tm):
    levels = _scan_levels(tm)
    top = SCAN_BASE - 1

    def views(lvl):
        if lvl == 0:
            return (lambda sl: hre[slab, sl, :], lambda sl: him[slab, sl, :],
                    lambda sl, v: hre.__setitem__((slab, sl, slice(None)), v),
                    lambda sl, v: him.__setitem__((slab, sl, slice(None)), v))
        a, b = lv_re[lvl - 1], lv_im[lvl - 1]
        return (lambda sl: a[slot, sl, :], lambda sl: b[slot, sl, :],
                lambda sl, v: a.__setitem__((slot, sl, slice(None)), v),
                lambda sl, v: b.__setitem__((slot, sl, slice(None)), v))

    def power(k):
        return pw_ref[slab, pl.ds(2 * k, 1), :], pw_ref[slab, pl.ds(2 * k + 1, 1), :]

    def elems(n, tau):
        return pl.ds(SCAN_BASE + tau, n, stride=SCAN_RADIX)

    s_re, s_im = hre[slab, pl.ds(top, 1), :], him[slab, pl.ds(top, 1), :]
    for lvl in range(levels):
        get_re, get_im, put_re, put_im = views(lvl)
        n = tm // SCAN_RADIX ** (lvl + 1)
        lr, li = power((SCAN_RADIX - 1) * lvl)
        pr, pi = get_re(elems(n, 0)), get_im(elems(n, 0))
        for tau in range(1, SCAN_RADIX):
            mr, mi = _cmul(lr, li, pr, pi)
            pr, pi = mr + get_re(elems(n, tau)), mi + get_im(elems(n, tau))
            put_re(elems(n, tau), pr)
            put_im(elems(n, tau), pi)
        if lvl + 1 < levels:
            _, _, nput_re, nput_im = views(lvl + 1)
            nput_re(pl.ds(SCAN_BASE, n), pr)
            nput_im(pl.ds(SCAN_BASE, n), pi)
            nput_re(pl.ds(top, 1), s_re)
            nput_im(pl.ds(top, 1), s_im)
    for lvl in reversed(range(levels)):
        get_re, get_im, put_re, put_im = views(lvl)
        n = tm // SCAN_RADIX ** (lvl + 1)
        last = lvl + 1 == levels
        if not last:
            nget_re, nget_im, _, _ = views(lvl + 1)
            put_re(elems(n, SCAN_RADIX - 1), nget_re(pl.ds(SCAN_BASE, n)))
            put_im(elems(n, SCAN_RADIX - 1), nget_im(pl.ds(SCAN_BASE, n)))
        before = pl.ds(top, n, stride=SCAN_RADIX)
        tr, ti = get_re(before), get_im(before)
        for tau in range(SCAN_RADIX if last else SCAN_RADIX - 1):
            k = (SCAN_RADIX - 1) * lvl + tau if tau < SCAN_RADIX - 1 else (SCAN_RADIX - 1) * levels
            mr, mi = _cmul(*power(k), tr, ti)
            put_re(elems(n, tau), get_re(elems(n, tau)) + mr)
            put_im(elems(n, tau), get_im(elems(n, tau)) + mi)
    hre[slab, pl.ds(top, 1), :] = hre[slab, pl.ds(top + tm, 1), :]
    him[slab, pl.ds(top, 1), :] = him[slab, pl.ds(top + tm, 1), :]


def _s5_body(x_ref, h0re_ref, h0im_ref, g_ref, bbd_ref, cbd_ref, pw_ref, d_ref, wglu_ref,
             xo_ref, hlre_ref, hlim_ref, hre, him, *levels_scr, tm, seg):
    nseq = tm // seg
    rows = pl.ds(SCAN_BASE, tm)
    top = SCAN_BASE - 1
    n_lv = len(levels_scr) // 2
    lv_re, lv_im = levels_scr[:n_lv], levels_scr[n_lv:]

    if nseq == 1:
        @pl.when(pl.program_id(1) == 0)
        def _():
            hre[:, pl.ds(top, 1), :] = h0re_ref[0]
            him[:, pl.ds(top, 1), :] = h0im_ref[0]

    x = x_ref[0]
    u = _rms(x, g_ref[...])
    ub = u.astype(BF16)

    def project_in(k):
        bu = _dot(ub[:, k * SSM_BLOCK_IN:(k + 1) * SSM_BLOCK_IN], bbd_ref[k])
        for j in range(SSM_SLABS_PER_BLOCK):
            hre[k * SSM_SLABS_PER_BLOCK + j, rows, :] = bu[:, j * LANES:(j + 1) * LANES]
            him[k * SSM_SLABS_PER_BLOCK + j, rows, :] = bu[:, SSM_BLOCK_HALF + j * LANES:
                                                          SSM_BLOCK_HALF + (j + 1) * LANES]

    def project_out(k):
        sl = range(k * SSM_SLABS_PER_BLOCK, (k + 1) * SSM_SLABS_PER_BLOCK)
        hk = jnp.concatenate([hre[p, rows, :] for p in sl] + [him[p, rows, :] for p in sl], axis=-1)
        return _dot(hk.astype(BF16), cbd_ref[k])

    if nseq == 1:
        ys = []
        project_in(0)
        for k in range(SSM_BLOCKS):
            if k + 1 < SSM_BLOCKS:
                project_in(k + 1)
            for j in range(SSM_SLABS_PER_BLOCK):
                _blocked_scan(hre, him, pw_ref, lv_re, lv_im, k * SSM_SLABS_PER_BLOCK + j, j % SCAN_UNROLL, tm)
            ys.append(project_out(k))
        hlre_ref[0] = hre[:, pl.ds(top, 1), :]
        hlim_ref[0] = him[:, pl.ds(top, 1), :]
    else:
        for k in range(SSM_BLOCKS):
            project_in(k)

        def slab_body(p, carry):
            lr, li = pw_ref[p, pl.ds(0, 1), :], pw_ref[p, pl.ds(1, 1), :]
            hr, hi = h0re_ref[0, p], h0im_ref[0, p]
            for t in range(seg):
                step = pl.ds(SCAN_BASE + t, nseq, stride=seg)
                mr, mi = _cmul(lr, li, hr, hi)
                hr, hi = mr + hre[p, step, :], mi + him[p, step, :]
                hre[p, step, :] = hr
                him[p, step, :] = hi
            hlre_ref[0, p] = hr
            hlim_ref[0, p] = hi
            return carry

        lax.fori_loop(0, SSM_SLABS, slab_body, 0)
        ys = [project_out(k) for k in range(SSM_BLOCKS)]

    y = jnp.concatenate(ys, axis=-1) + d_ref[...] * u
    z = jax.nn.gelu(y, approximate=True).astype(BF16)
    zz = _dot(z, wglu_ref[...])
    xo_ref[0] = x + zz[:, :D_MODEL] * jax.nn.sigmoid(zz[:, D_MODEL:])


def _s5_weights(a_re, a_im, log_dt, b_re, b_im, c_re, c_im, powers):
    a_re, a_im = a_re.astype(F32), a_im.astype(F32)
    dt = jnp.exp(log_dt.astype(F32))[:, None]
    mag = jnp.exp(a_re * dt)
    lre, lim = mag * jnp.cos(a_im * dt), mag * jnp.sin(a_im * dt)
    den = a_re * a_re + a_im * a_im
    fre = ((lre - 1.0) * a_re + lim * a_im) / den
    fim = (lim * a_re - (lre - 1.0) * a_im) / den
    b_re, b_im = b_re.astype(F32), b_im.astype(F32)
    bb = jnp.stack([fre[:, :, None] * b_re - fim[:, :, None] * b_im,
                    fre[:, :, None] * b_im + fim[:, :, None] * b_re])
    nb, gb = SSM_BLOCKS, SSM_GROUPS_PER_BLOCK
    bb = bb.reshape(2, nb, gb, SSM_STATE, SSM_GROUP).transpose(1, 2, 4, 0, 3)
    bb = jnp.tile(bb.reshape(nb, SSM_BLOCK_IN, 2, SSM_STATE), (1, 1, 1, gb))
    row_g = lax.broadcasted_iota(jnp.int32, bb.shape, 1) // SSM_GROUP
    lane_g = lax.broadcasted_iota(jnp.int32, bb.shape, 3) // SSM_STATE
    bbd = jnp.where(row_g == lane_g, bb, 0.0).reshape(nb, SSM_BLOCK_IN, SSM_BLOCK_STATE)
    cc = jnp.stack([c_re.astype(F32), -c_im.astype(F32)])
    cc = cc.reshape(2, nb, gb, SSM_GROUP, SSM_STATE).transpose(1, 0, 2, 4, 3)
    cc = jnp.tile(cc.reshape(nb, SSM_BLOCK_STATE, SSM_GROUP), (1, 1, gb))
    row_g = lax.broadcasted_iota(jnp.int32, cc.shape, 1) % SSM_BLOCK_HALF // SSM_STATE
    lane_g = lax.broadcasted_iota(jnp.int32, cc.shape, 2) // SSM_GROUP
    cbd = jnp.where(row_g == lane_g, cc, 0.0)
    tables = []
    for pows in powers:
        k = jnp.asarray(pows, F32)[:, None, None]
        mag_k = jnp.exp(k * (a_re * dt))
        pw = jnp.stack([mag_k * jnp.cos(k * (a_im * dt)), mag_k * jnp.sin(k * (a_im * dt))], axis=1)
        tables.append(pw.reshape(2 * len(pows), SSM_SLABS, LANES).transpose(1, 0, 2))
    return bbd.astype(BF16), cbd.astype(BF16), tables


def _state_to_slabs(h):
    b = h.shape[0]
    h = h.astype(F32).reshape(b, SSM_SLABS, LANES, 2).transpose(3, 1, 0, 2)
    return h[0], h[1]


def _slabs_to_state(re, im):
    b = re.shape[1]
    h = jnp.stack([re, im], axis=-1).transpose(1, 0, 2, 3)
    return h.reshape(b, N_SSM_GROUPS, SSM_STATE, 2)


def _s5(x, h0re, h0im, g, bbd, cbd, pw, d_skip, w_glu, *, tm, seg):
    nb, rows, _ = x.shape
    nseq = tm // seg
    assert nseq == 1 or rows == tm
    row = pl.BlockSpec((1, tm, D_MODEL), lambda b, i: (b, i, 0))
    st = pl.BlockSpec((1, SSM_SLABS, nseq, LANES), lambda b, i: (b, 0, 0, 0))
    slab = pltpu.VMEM((SSM_SLABS, SCAN_BASE + tm, LANES), F32)
    levels = []
    if nseq == 1:
        levels = [pltpu.VMEM((SCAN_UNROLL, SCAN_BASE + max(tm // SCAN_RADIX ** lvl, SUBLANES), LANES), F32)
                  for lvl in range(1, _scan_levels(tm))] * 2
    st_shape = jax.ShapeDtypeStruct(h0re.shape, F32)
    return pl.pallas_call(
        functools.partial(_s5_body, tm=tm, seg=seg),
        grid=(nb, rows // tm),
        in_specs=[row, st, st, _const_spec((1, D_MODEL)), _const_spec(bbd.shape), _const_spec(cbd.shape),
                  _const_spec(pw.shape), _const_spec((1, D_MODEL)), _const_spec(w_glu.shape)],
        out_specs=[row, st, st],
        out_shape=[jax.ShapeDtypeStruct(x.shape, F32), st_shape, st_shape],
        scratch_shapes=[slab, slab] + levels,
        compiler_params=_params(("parallel", "arbitrary")),
        name="s5_mixer",
    )(x, h0re, h0im, g.reshape(1, D_MODEL), bbd, cbd, pw, d_skip.reshape(1, D_MODEL), w_glu)


def _rope_tables(pos):
    half = ROT_DIM // 2
    inv = ROPE_THETA ** (-jnp.arange(0, ROT_DIM, 2, dtype=F32) / ROT_DIM)
    ang = pos.astype(F32)[:, None] * inv[None, :]
    cos, sin = jnp.cos(ang), jnp.sin(ang)
    n = pos.shape[0]
    pad = HEAD_DIM - ROT_DIM
    c = jnp.concatenate([cos, cos, jnp.ones((n, pad), F32)], axis=-1)
    s1 = jnp.concatenate([-sin, jnp.zeros((n, half + pad), F32)], axis=-1)
    s2 = jnp.concatenate([jnp.zeros((n, half), F32), sin, jnp.zeros((n, pad), F32)], axis=-1)
    rep = LANES // HEAD_DIM
    return tuple(jnp.tile(t, (1, rep)) for t in (c, s1, s2))


def _rope_tile(y, c, s1, s2):
    half = ROT_DIM // 2
    return y * c + pltpu.roll(y, LANES - half, 1) * s1 + pltpu.roll(y, half, 1) * s2


def _proj_body(x_ref, g_ref, w_ref, c_ref, s1_ref, s2_ref, *o_refs, rope_tiles):
    h = _rms(x_ref[...], g_ref[...]).astype(BF16)
    y = _dot(h, w_ref[...])
    c, s1, s2 = c_ref[...], s1_ref[...], s2_ref[...]
    tiles_per_out = y.shape[1] // LANES // len(o_refs)
    for j in range(y.shape[1] // LANES):
        yj = y[:, j * LANES:(j + 1) * LANES]
        if j < rope_tiles:
            yj = _rope_tile(yj, c, s1, s2)
        jo = j % tiles_per_out
        o_refs[j // tiles_per_out][:, jo * LANES:(jo + 1) * LANES] = yj


def _proj(x, g, w, tables, rope_width, n_out, tm=512):
    n = x.shape[0]
    tm = min(tm, n)
    width = w.shape[1] // n_out
    row = pl.BlockSpec((tm, D_MODEL), lambda i: (i, 0))
    tab = pl.BlockSpec((tm, LANES), lambda i: (i, 0))
    return pl.pallas_call(
        functools.partial(_proj_body, rope_tiles=rope_width // LANES),
        grid=(n // tm,),
        in_specs=[row, _const_spec((1, D_MODEL)), _const_spec(w.shape), tab, tab, tab],
        out_specs=[pl.BlockSpec((tm, width), lambda i: (i, 0))] * n_out,
        out_shape=[jax.ShapeDtypeStruct((n, width), F32)] * n_out,
        compiler_params=_params(("parallel",)),
        name="proj_rope",
    )(x, g.reshape(1, D_MODEL), w, *tables)


def _proj_heads_body(x_ref, g_ref, w_ref, c_ref, s1_ref, s2_ref, *refs, n_mat, tm, cache_from):
    n_t = n_mat if cache_from is not None else 0
    outs = refs[:N_BRANCHES * n_mat]
    t_refs = refs[N_BRANCHES * n_mat:N_BRANCHES * n_mat + n_t]
    slab = refs[-1]
    h = _rms(x_ref[0], g_ref[...]).astype(BF16)
    y = _dot(h, w_ref[...])
    c, s1, s2 = c_ref[...], s1_ref[...], s2_ref[...]
    for m in range(n_mat):
        tiles = []
        for br, d in enumerate(DILATIONS):
            lo = m * ATTN_WIDTH + br * MERGED_WIDTH
            part = [y[:, lo + j * LANES:lo + (j + 1) * LANES] for j in range(BRANCH_TILES)]
            if m == 0:
                part = [_rope_tile(t, c, s1, s2) for t in part]
            tiles += part
            out = outs[N_BRANCHES * m + br]
            if d == 1:
                for j in range(BRANCH_TILES):
                    out[0, 0, :, j * LANES:(j + 1) * LANES] = part[j].astype(BF16)
                continue
            for j in range(BRANCH_TILES):
                slab[(br - 1) * BRANCH_TILES + j] = part[j]
            for r in range(d):
                for j in range(BRANCH_TILES):
                    rows = slab[(br - 1) * BRANCH_TILES + j, pl.ds(r, tm // d, stride=d), :]
                    out[0, r, :, j * LANES:(j + 1) * LANES] = rows.astype(BF16)
        if cache_from is not None:
            @pl.when(pl.program_id(1) >= cache_from)
            def _(m=m, tiles=tiles):
                t_refs[m][0] = jnp.concatenate(tiles, axis=-1).T


def _proj_heads(x, g, w, tables, n_mat, keep=None, tm=512):
    b, l, _ = x.shape
    assert w.shape[1] == n_mat * ATTN_WIDTH and tm % (DILATIONS[-1] * 2 * SUBLANES) == 0
    cache_from = None if keep is None else (l - keep) // tm
    row = pl.BlockSpec((1, tm, D_MODEL), lambda bi, i: (bi, i, 0))
    tab = pl.BlockSpec((tm, LANES), lambda bi, i: (i, 0))
    out_specs, out_shape = [], []
    for _ in range(n_mat):
        for d in DILATIONS:
            out_specs.append(pl.BlockSpec((1, d, tm // d, MERGED_WIDTH), lambda bi, i: (bi, 0, i, 0)))
            out_shape.append(jax.ShapeDtypeStruct((b, d, l // d, MERGED_WIDTH), BF16))
    if keep is not None:
        for _ in range(n_mat):
            out_specs.append(pl.BlockSpec((1, ATTN_WIDTH, tm),
                                          lambda bi, i: (bi, 0, jnp.maximum(i - cache_from, 0))))
            out_shape.append(jax.ShapeDtypeStruct((b, ATTN_WIDTH, keep), F32))
    return pl.pallas_call(
        functools.partial(_proj_heads_body, n_mat=n_mat, tm=tm, cache_from=cache_from),
        grid=(b, l // tm),
        in_specs=[row, _const_spec((1, D_MODEL)), _const_spec(w.shape), tab, tab, tab],
        out_specs=out_specs,
        out_shape=out_shape,
        scratch_shapes=[pltpu.VMEM(((N_BRANCHES - 1) * BRANCH_TILES, tm, LANES), F32)],
        compiler_params=_params(("parallel", "arbitrary")),
        name="proj_heads",
    )(x, g.reshape(1, D_MODEL), w, *tables)


def _swa_body(q_ref, kp_ref, kc_ref, vp_ref, vc_ref, o_ref, lse_ref, *, steps_per_seq):
    n = pl.program_id(1)
    k = jnp.concatenate([kp_ref[0], kc_ref[0]], axis=0)
    v = jnp.concatenate([vp_ref[0], vc_ref[0]], axis=0)
    qi = lax.broadcasted_iota(jnp.int32, (SWA_BLOCK, 2 * SWA_BLOCK), 0)
    kj = lax.broadcasted_iota(jnp.int32, (SWA_BLOCK, 2 * SWA_BLOCK), 1)
    band = (kj >= qi) & (kj <= qi + SWA_BLOCK)
    has_prev = (n % steps_per_seq) != 0
    first = band & ((kj >= SWA_BLOCK) | has_prev)
    low = lax.broadcasted_iota(jnp.int32, (SWA_BLOCK, LANES), 1) < HEAD_DIM
    zero = jnp.zeros((SWA_BLOCK, LANES), BF16)
    for j in range(SWA_SUB):
        rows = slice(j * SWA_BLOCK, (j + 1) * SWA_BLOCK)
        keys = slice(j * SWA_BLOCK, (j + 2) * SWA_BLOCK)
        valid = first if j == 0 else band
        for t in range(BRANCH_TILES):
            cols = slice(t * LANES, (t + 1) * LANES)
            q = q_ref[0, rows, cols] * (HEAD_DIM ** -0.5)
            kt, vt = k[keys, cols], v[keys, cols]
            outs, lses = [], []
            for qh in (jnp.where(low, q, zero), jnp.where(low, zero, q)):
                s = jnp.where(valid, _dot_nt(qh, kt), NEG_INF)
                m = jnp.max(s, axis=-1, keepdims=True)
                p = jnp.exp(s - m)
                l = jnp.sum(p, axis=-1, keepdims=True)
                outs.append(_dot(p.astype(BF16), vt) / l)
                lses.append(jnp.broadcast_to(m + jnp.log(l), (SWA_BLOCK, LANES)))
            o_ref[0, rows, cols] = jnp.where(low, outs[0], outs[1])
            lse_ref[0, rows, cols] = jnp.where(low, lses[0], lses[1])


def _swa(q, k, v, dilation):
    b, l, w = q.shape
    step = SWA_SUB * SWA_BLOCK
    assert (l // dilation) % step == 0
    cur = pl.BlockSpec((1, step, w), lambda bi, n: (bi, n, 0))
    prev = pl.BlockSpec((1, SWA_BLOCK, w), lambda bi, n: (bi, jnp.maximum(SWA_SUB * n - 1, 0), 0))
    out = jax.ShapeDtypeStruct((b, l, w), F32)
    return pl.pallas_call(
        functools.partial(_swa_body, steps_per_seq=l // dilation // step),
        grid=(b, l // step),
        in_specs=[cur, prev, cur, prev, cur],
        out_specs=[cur, cur],
        out_shape=[out, out],
        compiler_params=_params(("parallel", "arbitrary")),
        name="swa",
    )(q, k, k, v, v)


def _merge_body(x_ref, *refs, tm):
    pairs, wo_ref, out_ref, slab = refs[:2 * N_BRANCHES], refs[2 * N_BRANCHES], refs[-2], refs[-1]
    for a, ref in enumerate(pairs[2:]):
        d = DILATIONS[1 + a // 2]
        for r in range(d):
            for j in range(BRANCH_TILES):
                slab[a * BRANCH_TILES + j, pl.ds(r, tm // d, stride=d), :] = ref[0, r, :, j * LANES:(j + 1) * LANES]
    tiles = []
    for j in range(BRANCH_TILES):
        cols = slice(j * LANES, (j + 1) * LANES)
        o0, l0 = pairs[0][0, 0, :, cols], pairs[1][0, 0, :, cols]
        o1, l1, o2, l2 = (slab[a * BRANCH_TILES + j] for a in range(4))
        m = jnp.maximum(jnp.maximum(l0, l1), l2)
        e0, e1, e2 = jnp.exp(l0 - m), jnp.exp(l1 - m), jnp.exp(l2 - m)
        tiles.append(((e0 * o0 + e1 * o1 + e2 * o2) / (e0 + e1 + e2)).astype(BF16))
    out_ref[0] = x_ref[0] + _dot(jnp.concatenate(tiles, axis=-1), wo_ref[...])


def _merge(x, os, ls, w_o, tm=1024):
    b, l, _ = x.shape
    row = pl.BlockSpec((1, tm, D_MODEL), lambda bi, i: (bi, i, 0))
    specs, args = [], []
    for d, o, lse in zip(DILATIONS, os, ls):
        spec = pl.BlockSpec((1, d, tm // d, MERGED_WIDTH), lambda bi, i: (bi, 0, i, 0))
        specs += [spec, spec]
        args += [o, lse]
    return pl.pallas_call(
        functools.partial(_merge_body, tm=tm),
        grid=(b, l // tm),
        in_specs=[row] + specs + [_const_spec(w_o.shape)],
        out_specs=row,
        out_shape=jax.ShapeDtypeStruct(x.shape, F32),
        scratch_shapes=[pltpu.VMEM((2 * (N_BRANCHES - 1) * BRANCH_TILES, tm, LANES), F32)],
        compiler_params=_params(("parallel", "parallel")),
        name="merge_wo",
    )(x, *args, w_o)


def _decode_body(q_ref, kn_ref, vn_ref, k0_ref, k1_ref, k2_ref, v0_ref, v1_ref, v2_ref, o_ref, *, steps):
    scale = HEAD_DIM ** -0.5
    qi = lax.broadcasted_iota(jnp.int32, (steps, steps), 0)
    kj = lax.broadcasted_iota(jnp.int32, (steps, steps), 1)
    k_refs, v_refs = (k0_ref, k1_ref, k2_ref), (v0_ref, v1_ref, v2_ref)

    def slot(sl):
        scores, values = [], []
        for br, d in enumerate(DILATIONS):
            w = WINDOWS[br]
            head = slice((br * HEADS_PER_BRANCH + sl) * HEAD_DIM, (br * HEADS_PER_BRANCH + sl + 1) * HEAD_DIM)
            q = q_ref[0, :, head].astype(BF16)
            back = (w + lax.broadcasted_iota(jnp.int32, (steps, w), 0)
                    - lax.broadcasted_iota(jnp.int32, (steps, w), 1))
            ok = ((back & (d - 1)) == 0) & (back <= w)
            scores.append(jnp.where(ok, _dot(q, k_refs[br][0, 0, sl].astype(BF16)) * scale, NEG_INF))
            values.append((v_refs[br][0, 0, sl].astype(BF16), True))
            ok = (kj <= qi) & (((qi - kj) & (d - 1)) == 0)
            scores.append(jnp.where(ok, _dot_nt(q, kn_ref[0, :, head].astype(BF16)) * scale, NEG_INF))
            values.append((vn_ref[0, :, head].astype(BF16), False))
        m = functools.reduce(jnp.maximum, [jnp.max(s, axis=-1, keepdims=True) for s in scores])
        num = jnp.zeros((steps, HEAD_DIM), F32)
        den = jnp.zeros((steps, 1), F32)
        for s, (v, transposed) in zip(scores, values):
            p = jnp.exp(s - m)
            den = den + jnp.sum(p, axis=-1, keepdims=True)
            num = num + (_dot_nt(p.astype(BF16), v) if transposed else _dot(p.astype(BF16), v))
        o_ref[0, :, sl * HEAD_DIM:(sl + 1) * HEAD_DIM] = num / den

    for sl in range(HEADS_PER_BRANCH):
        slot(sl)


def _cache_t(c):
    b, past = c.shape[0], c.shape[1]
    return c.transpose(0, 2, 3, 1).reshape(b, N_BRANCHES, HEADS_PER_BRANCH, HEAD_DIM, past)


def _cache_from_t(c):
    b, past = c.shape[0], c.shape[-1]
    return c.reshape(b, N_HEADS, HEAD_DIM, past).transpose(0, 3, 1, 2)


def _decode(q, kn, vn, cache_k, cache_v):
    b, steps = q.shape[0], q.shape[1]
    past = cache_k.shape[-1]
    assert past == MAX_WINDOW
    hb = HEADS_PER_BRANCH
    new = pl.BlockSpec((1, steps, ATTN_WIDTH), lambda i: (i, 0, 0))
    windows = [pl.BlockSpec((1, 1, hb, HEAD_DIM, w), functools.partial(lambda br, last, i: (i, br, 0, 0, last),
                                                                      br, past // w - 1))
               for br, w in enumerate(WINDOWS)]
    o = pl.pallas_call(
        functools.partial(_decode_body, steps=steps),
        grid=(b,),
        in_specs=[new, new, new] + windows + windows,
        out_specs=pl.BlockSpec((1, steps, MERGED_WIDTH), lambda i: (i, 0, 0)),
        out_shape=jax.ShapeDtypeStruct((b, steps, MERGED_WIDTH), F32),
        compiler_params=_params(("parallel",)),
        name="decode_attn",
    )(q, kn, vn, *([cache_k] * N_BRANCHES), *([cache_v] * N_BRANCHES))
    return o.reshape(b * steps, MERGED_WIDTH)


def _addproj_body(x_ref, o_ref, wo_ref, out_ref):
    out_ref[...] = x_ref[...] + _dot(o_ref[...].astype(BF16), wo_ref[...])


def _addproj(x, o, w_o):
    n = x.shape[0]
    return pl.pallas_call(
        _addproj_body,
        grid=(1,),
        in_specs=[_const_spec(x.shape), _const_spec(o.shape), _const_spec(w_o.shape)],
        out_specs=pl.BlockSpec(x.shape, lambda i: (0, 0)),
        out_shape=jax.ShapeDtypeStruct((n, D_MODEL), F32),
        compiler_params=_params(("arbitrary",)),
        name="add_wo",
    )(x, o, w_o)


def kernel(x_prompt, x_sample, state_ssm, cache_k, cache_v, ffn_norm1, ffn_w1_in, ffn_w1_out, ffn_norm2,
           ffn_w2_in, ffn_w2_out, mix_norm, ssm_a_re, ssm_a_im, ssm_log_dt, ssm_b_re, ssm_b_im, ssm_c_re,
           ssm_c_im, ssm_d, ssm_w_glu, kv_norm, w_kv, w_q, w_o, final_norm):
    bp, sp, _ = x_prompt.shape
    bs, ss, _ = x_sample.shape
    tm_scan = 256

    w1_in, w1_out, w2_in, w2_out = (w.astype(BF16) for w in (ffn_w1_in, ffn_w1_out, ffn_w2_in, ffn_w2_out))
    w_glu, w_kv_b, w_q_b, w_o_b = (ssm_w_glu[0].astype(BF16), w_kv.astype(BF16), w_q[0].astype(BF16),
                                   w_o[0].astype(BF16))
    bbd, cbd, (pw_p, pw_s) = _s5_weights(ssm_a_re[0], ssm_a_im[0], ssm_log_dt[0], ssm_b_re[0], ssm_b_im[0],
                                         ssm_c_re[0], ssm_c_im[0], (_scan_powers(tm_scan), [1]))

    def mixer(x, h0, pw, tm, seg):
        return _s5(x, *h0, mix_norm[0], bbd, cbd, pw, ssm_d[0], w_glu, tm=tm, seg=seg)

    pos_s = jnp.tile(PAST_LEN + jnp.arange(ss, dtype=jnp.int32), bs)
    tab_s = _rope_tables(pos_s)
    h0_s = tuple(t[None] for t in _state_to_slabs(state_ssm[0]))
    xs = _ffn(x_sample.reshape(bs * ss, D_MODEL), ffn_norm1, w1_in, w1_out, 0)
    xs, hl_re, hl_im = mixer(xs.reshape(1, bs * ss, D_MODEL), h0_s, pw_s, bs * ss, ss)
    xs = _ffn(xs.reshape(bs * ss, D_MODEL), ffn_norm2, w2_in, w2_out, 0)
    state_sample = _slabs_to_state(hl_re[0], hl_im[0])[None]
    kn_t, vn_t = (t.reshape(bs, ss, ATTN_WIDTH) for t in _proj(xs, kv_norm, w_kv_b, tab_s, ATTN_WIDTH, 2))
    ck_t, cv_t = _cache_t(cache_k), _cache_t(cache_v)
    half = bs // 2
    tm_roll = bp * sp // half

    tab_p = _rope_tables(jnp.arange(sp, dtype=jnp.int32))
    zeros = jnp.zeros((bp, SSM_SLABS, 1, LANES), F32)
    xp, k_roll = _ffn_roll(x_prompt.reshape(bp * sp, D_MODEL), ffn_norm1, w1_in, w1_out, 0, ck_t, kn_t, None, 0,
                           tm=tm_roll)
    xp, hl_re, hl_im = mixer(xp.reshape(bp, sp, D_MODEL), (zeros, zeros), pw_p, tm_scan, tm_scan)
    xp, k_roll = _ffn_roll(xp.reshape(bp * sp, D_MODEL), ffn_norm2, w2_in, w2_out, 0, ck_t, kn_t, k_roll, half,
                           tm=tm_roll)
    cache_k_sample = _cache_from_t(k_roll)
    state_prompt = _slabs_to_state(hl_re[:, :, 0].transpose(1, 0, 2), hl_im[:, :, 0].transpose(1, 0, 2))[None]
    keep_p = min(MAX_WINDOW, sp)
    *kv_d, kt_p, vt_p = _proj_heads(xp.reshape(bp, sp, D_MODEL), kv_norm, w_kv_b, tab_p, 2, keep=keep_p)
    cache_k_prompt = kt_p.reshape(bp, N_HEADS, HEAD_DIM, keep_p).transpose(0, 3, 1, 2)
    cache_v_prompt = vt_p.reshape(bp, N_HEADS, HEAD_DIM, keep_p).transpose(0, 3, 1, 2)
    xp, v_roll = _ffn_roll(xp, ffn_norm1, w1_in, w1_out, 1, cv_t, vn_t, None, 0, tm=tm_roll)
    xp = xp.reshape(bp, sp, D_MODEL)
    q_d = _proj_heads(xp, mix_norm[1], w_q_b, tab_p, 1)
    outs, lses = [], []
    for br, d in enumerate(DILATIONS):
        flat = lambda t: t.reshape(bp, sp, MERGED_WIDTH)
        o, lse = _swa(flat(q_d[br]), flat(kv_d[br]), flat(kv_d[N_BRANCHES + br]), d)
        outs.append(o.reshape(bp, d, sp // d, MERGED_WIDTH))
        lses.append(lse.reshape(bp, d, sp // d, MERGED_WIDTH))
    xp = _merge(xp, outs, lses, w_o_b).reshape(bp * sp, D_MODEL)
    y_prompt, v_roll = _ffn_roll(xp, ffn_norm2, w2_in, w2_out, 1, cv_t, vn_t, v_roll, half, final_g=final_norm,
                                 tm=tm_roll)
    y_prompt = y_prompt.reshape(bp, sp, D_MODEL)
    cache_v_sample = _cache_from_t(v_roll)

    xs = _ffn(xs, ffn_norm1, w1_in, w1_out, 1)
    q_s = _proj(xs, mix_norm[1], w_q_b, tab_s, ATTN_WIDTH, 1)[0].reshape(bs, ss, ATTN_WIDTH)
    xs = _addproj(xs, _decode(q_s, kn_t, vn_t, ck_t, cv_t), w_o_b)
    y_sample = _ffn(xs, ffn_norm2, w2_in, w2_out, 1, final_g=final_norm).reshape(bs, ss, D_MODEL)

    return (y_prompt, y_sample, state_prompt, cache_k_prompt, cache_v_prompt,
            state_sample, cache_k_sample, cache_v_sample)
```
